```python
import math
import jax, jax.numpy as jnp
from jax import lax
import numpy as np

D_MODEL = 1024
BATCH = 2
SEQ = 8192
DEPTH = 4

GRID_W = 64
CTX_LEN = 256
N_HEADS = 4
HEAD_DIM = 64
V_DIM = 2 * HEAD_DIM
QK_WIDTH = N_HEADS * 2 * HEAD_DIM
ATTN_WIDTH = N_HEADS * V_DIM
SSM_WIDTH = D_MODEL // 4
SSM_P = 16
SSM_G = SSM_WIDTH // SSM_P
SSM_N = 64
FNET_WIDTH = D_MODEL // 4
FNET_G = 4
FNET_C = FNET_WIDTH // FNET_G
MIX_WIDTH = ATTN_WIDTH + SSM_WIDTH + FNET_WIDTH
IN_WIDTH = 2 * QK_WIDTH + ATTN_WIDTH + SSM_WIDTH + FNET_WIDTH
D_FF = 4 * D_MODEL
Q_BLOCK = 128
ROPE_BASE = 10000.0
EPS = 1e-6

kernel_name = 'hybrid_diffattn_s5_fnet_dit_block'


def rms_norm(x, g):
    xf = x.astype(jnp.float32)
    y = xf * lax.rsqrt(jnp.mean(xf * xf, axis=-1, keepdims=True) + EPS)
    return (y * g.astype(jnp.float32)).astype(x.dtype)


def axial_rope_tables(n_tokens):
    rows = n_tokens // GRID_W
    t = jnp.arange(rows * GRID_W)
    row = (t // GRID_W).astype(jnp.float32)
    col = (t % GRID_W).astype(jnp.float32)
    n_freq = HEAD_DIM // 4
    inv = jnp.power(ROPE_BASE, -jnp.arange(n_freq, dtype=jnp.float32) / n_freq)
    ang = jnp.concatenate([row[:, None] * inv, col[:, None] * inv], axis=-1)
    return jnp.cos(ang), jnp.sin(ang)


def apply_rope(x, cos, sin):
    half = x.shape[-1] // 2
    x1 = x[..., :half].astype(jnp.float32)
    x2 = x[..., half:].astype(jnp.float32)
    cos = cos[None, :, None, None, :]
    sin = sin[None, :, None, None, :]
    return jnp.concatenate([x1 * cos - x2 * sin, x2 * cos + x1 * sin], axis=-1).astype(x.dtype)


def diff_attention(q, k, v, lam):
    b, lq, h, _, dh = q.shape
    nb = lq // Q_BLOCK
    qb = jnp.moveaxis(q.reshape(b, nb, Q_BLOCK, h, 2, dh), 1, 0)
    scale = dh ** -0.5

    def one_block(qblk):
        s = jnp.einsum('bqhmd,bkhmd->bmhqk', qblk, k).astype(jnp.float32) * scale
        p = jax.nn.softmax(s, axis=-1)
        w = p[:, 0] - lam * p[:, 1]
        return jnp.einsum('bhqk,bkhd->bqhd', w.astype(v.dtype), v)

    out = lax.map(one_block, qb)
    return jnp.moveaxis(out, 0, 1).reshape(b, lq, h, v.shape[-1])


def zoh_discretise(a_re, a_im, log_dt, b_re, b_im):
    f32 = jnp.float32
    a_re = a_re.astype(f32)
    a_im = a_im.astype(f32)
    dt = jnp.exp(log_dt.astype(f32))[:, None]
    mag = jnp.exp(dt * a_re)
    ab_re = mag * jnp.cos(dt * a_im)
    ab_im = mag * jnp.sin(dt * a_im)
    den = a_re * a_re + a_im * a_im
    n_re = ab_re - 1.0
    f_re = (n_re * a_re + ab_im * a_im) / den
    f_im = (ab_im * a_re - n_re * a_im) / den
    b_re = b_re.astype(f32)
    b_im = b_im.astype(f32)
    bb_re = f_re[..., None] * b_re - f_im[..., None] * b_im
    bb_im = f_re[..., None] * b_im + f_im[..., None] * b_re
    return ab_re, ab_im, bb_re, bb_im


def _affine_combine(e1, e2):
    a1r, a1i, b1r, b1i = e1
    a2r, a2i, b2r, b2i = e2
    return (a2r * a1r - a2i * a1i,
            a2r * a1i + a2i * a1r,
            a2r * b1r - a2i * b1i + b2r,
            a2r * b1i + a2i * b1r + b2i)


def s5_scan(u, disc, s0, reverse):
    ab_re, ab_im, bb_re, bb_im = disc
    b, l, _ = u.shape
    uf = u.astype(jnp.float32).reshape(b, l, SSM_G, SSM_P)
    if reverse:
        uf = uf[:, ::-1]
    bu_re = jnp.einsum('blgp,gnp->blgn', uf, bb_re)
    bu_im = jnp.einsum('blgp,gnp->blgn', uf, bb_im)
    if s0 is not None:
        s0_re, s0_im = s0
        bu_re = bu_re.at[:, 0].add(ab_re * s0_re - ab_im * s0_im)
        bu_im = bu_im.at[:, 0].add(ab_re * s0_im + ab_im * s0_re)
    a_re = jnp.broadcast_to(ab_re, bu_re.shape)
    a_im = jnp.broadcast_to(ab_im, bu_im.shape)
    _, _, s_re, s_im = lax.associative_scan(_affine_combine, (a_re, a_im, bu_re, bu_im), axis=1)
    return s_re, s_im


def s5_readout(states, c_re, c_im, reverse):
    s_re, s_im = states
    y = (jnp.einsum('blgn,gpn->blgp', s_re, c_re.astype(jnp.float32))
         - jnp.einsum('blgn,gpn->blgp', s_im, c_im.astype(jnp.float32)))
    if reverse:
        y = y[:, ::-1]
    b, l = y.shape[:2]
    return y.reshape(b, l, SSM_WIDTH)


def s5_glu(y, p):
    h = jax.nn.gelu(y)
    return h * jax.nn.sigmoid(h @ p['w_glu'] + p['b_glu'])


def s5_mixer(u_x, u_c, p, ctx_out):
    d_skip = p['ssm_d'].astype(jnp.float32)
    y_x = d_skip * u_x.astype(jnp.float32)
    y_c = d_skip * u_c.astype(jnp.float32) if ctx_out else None
    for direction in range(2):
        rev = direction == 1
        disc = zoh_discretise(p['ssm_a_re'][direction], p['ssm_a_im'][direction],
                              p['ssm_log_dt'][direction], p['ssm_b_re'][direction],
                              p['ssm_b_im'][direction])
        st_c = s5_scan(u_c, disc, None, rev)
        st_x = s5_scan(u_x, disc, (st_c[0][:, -1], st_c[1][:, -1]), rev)
        y_x = y_x + s5_readout(st_x, p['ssm_c_re'][direction], p['ssm_c_im'][direction], rev)
        if ctx_out:
            y_c = y_c + s5_readout(st_c, p['ssm_c_re'][direction], p['ssm_c_im'][direction], rev)
    out_x = s5_glu(y_x.astype(u_x.dtype), p)
    out_c = s5_glu(y_c.astype(u_c.dtype), p) if ctx_out else None
    return out_x, out_c


def fourier_mix(h, w_fnet):
    b, l, _ = h.shape
    hf = h.astype(jnp.float32).reshape(b, l, FNET_G, FNET_C)
    z = jnp.fft.fftn(hf, axes=(1, 3), norm='ortho').real.astype(h.dtype)
    return jnp.einsum('blgc,gcd->blgd', z, w_fnet).reshape(b, l, FNET_WIDTH)


def sq_relu_mlp(h, p):
    return jnp.square(jax.nn.relu(h @ p['w_ff1'])) @ p['w_ff2']


def split_proj(proj):
    b, l, _ = proj.shape
    o1 = QK_WIDTH
    o2 = 2 * QK_WIDTH
    o3 = o2 + ATTN_WIDTH
    o4 = o3 + SSM_WIDTH
    q = proj[..., :o1].reshape(b, l, N_HEADS, 2, HEAD_DIM)
    k = proj[..., o1:o2].reshape(b, l, N_HEADS, 2, HEAD_DIM)
    v = proj[..., o2:o3].reshape(b, l, N_HEADS, V_DIM)
    return q, k, v, proj[..., o3:o4], proj[..., o4:]


def trunk_layer(x, ctx, c_act, cctx_act, rope, p, layer_idx, ctx_out):
    cos, sin = rope
    b, l, _ = x.shape
    bc, lc, _ = ctx.shape
    mod_x = (c_act @ p['w_mod'] + p['b_mod'])[:, None, :]
    mod_c = (cctx_act @ p['w_mod'] + p['b_mod'])[None, None, :]
    sh1x, sc1x, g1x, sh2x, sc2x, g2x = jnp.split(mod_x, 6, axis=-1)
    sh1c, sc1c, g1c, sh2c, sc2c, g2c = jnp.split(mod_c, 6, axis=-1)

    hx = rms_norm(x, p['g_norm1']) * (1.0 + sc1x) + sh1x
    hc = rms_norm(ctx, p['g_norm1']) * (1.0 + sc1c) + sh1c
    qx, kx, vx, ux, fx = split_proj(hx @ p['w_in'])
    qc, kc, vc, uc, fc = split_proj(hc @ p['w_in'])

    f32 = jnp.float32
    lam_init = 0.8 - 0.6 * math.exp(-0.3 * layer_idx)
    lam = (jnp.exp(jnp.sum(p['lam_q1'].astype(f32) * p['lam_k1'].astype(f32)))
           - jnp.exp(jnp.sum(p['lam_q2'].astype(f32) * p['lam_k2'].astype(f32))) + lam_init)
    qx = apply_rope(rms_norm(qx, p['g_qnorm']), cos, sin)
    kx = apply_rope(rms_norm(kx, p['g_knorm']), cos, sin)
    kc = rms_norm(kc, p['g_knorm'])
    k_all = jnp.concatenate([kc, kx], axis=1)
    v_all = jnp.concatenate([vc, vx], axis=1)
    ax = rms_norm(diff_attention(qx, k_all, v_all, lam), p['g_subln']) * (1.0 - lam_init)

    sx, sc = s5_mixer(ux, uc, p, ctx_out)

    mix_x = jnp.concatenate([ax.reshape(b, l, ATTN_WIDTH), sx, fourier_mix(fx, p['w_fnet'])], axis=-1)
    x = x + g1x * (mix_x @ p['w_out'])

    x = x + g2x * sq_relu_mlp(rms_norm(x, p['g_norm2']) * (1.0 + sc2x) + sh2x, p)

    if ctx_out:
        qc = rms_norm(qc, p['g_qnorm'])
        ac = rms_norm(diff_attention(qc, kc, vc, lam), p['g_subln']) * (1.0 - lam_init)
        mix_c = jnp.concatenate([ac.reshape(bc, lc, ATTN_WIDTH), sc, fourier_mix(fc, p['w_fnet'])], axis=-1)
        ctx = ctx + g1c * (mix_c @ p['w_out'])
        ctx = ctx + g2c * sq_relu_mlp(rms_norm(ctx, p['g_norm2']) * (1.0 + sc2c) + sh2c, p)
    return x, ctx


def setup_inputs(seed: int = 0) -> dict:
    key = jax.random.key(seed)
    ks = jax.random.split(key, 32)
    f32 = jnp.float32

    def nrm(k, shape, s):
        return jax.random.normal(k, shape, f32) * s

    n_idx = jnp.arange(SSM_N, dtype=f32)
    return {
        'x': nrm(ks[0], (BATCH, SEQ, D_MODEL), 1.0),
        'c': nrm(ks[1], (BATCH, D_MODEL), 1.0),
        'ctx': nrm(ks[2], (BATCH, CTX_LEN, D_MODEL), 1.0),
        'c_ctx': nrm(ks[3], (D_MODEL,), 1.0),
        'w_mod': nrm(ks[4], (DEPTH, D_MODEL, 6 * D_MODEL), 0.5 * D_MODEL ** -0.5),
        'b_mod': nrm(ks[5], (DEPTH, 6 * D_MODEL), 0.02),
        'g_norm1': 1.0 + nrm(ks[6], (DEPTH, D_MODEL), 0.02),
        'w_in': nrm(ks[7], (DEPTH, D_MODEL, IN_WIDTH), D_MODEL ** -0.5),
        'g_qnorm': 1.0 + nrm(ks[8], (DEPTH, HEAD_DIM), 0.02),
        'g_knorm': 1.0 + nrm(ks[9], (DEPTH, HEAD_DIM), 0.02),
        'lam_q1': nrm(ks[10], (DEPTH, HEAD_DIM), 0.1),
        'lam_k1': nrm(ks[11], (DEPTH, HEAD_DIM), 0.1),
        'lam_q2': nrm(ks[12], (DEPTH, HEAD_DIM), 0.1),
        'lam_k2': nrm(ks[13], (DEPTH, HEAD_DIM), 0.1),
        'g_subln': 1.0 + nrm(ks[14], (DEPTH, V_DIM), 0.02),
        'ssm_a_re': -0.5 + nrm(ks[15], (DEPTH, 2, SSM_G, SSM_N), 0.01),
        'ssm_a_im': jnp.pi * n_idx + nrm(ks[16], (DEPTH, 2, SSM_G, SSM_N), 0.01),
        'ssm_log_dt': jax.random.uniform(ks[17], (DEPTH, 2, SSM_G), f32,
                                         minval=math.log(1e-3), maxval=math.log(1e-1)),
        'ssm_b_re': nrm(ks[18], (DEPTH, 2, SSM_G, SSM_N, SSM_P), (2 * SSM_P) ** -0.5),
        'ssm_b_im': nrm(ks[19], (DEPTH, 2, SSM_G, SSM_N, SSM_P), (2 * SSM_P) ** -0.5),
        'ssm_c_re': nrm(ks[20], (DEPTH, 2, SSM_G, SSM_P, SSM_N), 0.5),
        'ssm_c_im': nrm(ks[21], (DEPTH, 2, SSM_G, SSM_P, SSM_N), 0.5),
        'ssm_d': nrm(ks[22], (DEPTH, SSM_WIDTH), 0.5),
        'w_glu': nrm(ks[23], (DEPTH, SSM_WIDTH, SSM_WIDTH), SSM_WIDTH ** -0.5),
        'b_glu': nrm(ks[24], (DEPTH, SSM_WIDTH), 0.02),
        'w_fnet': nrm(ks[25], (DEPTH, FNET_G, FNET_C, FNET_C), FNET_C ** -0.5),
        'w_out': nrm(ks[26], (DEPTH, MIX_WIDTH, D_MODEL), MIX_WIDTH ** -0.5),
        'g_norm2': 1.0 + nrm(ks[27], (DEPTH, D_MODEL), 0.02),
        'w_ff1': nrm(ks[28], (DEPTH, D_MODEL, D_FF), D_MODEL ** -0.5),
        'w_ff2': nrm(ks[29], (DEPTH, D_FF, D_MODEL), D_FF ** -0.5),
    }


def reference(x, c, ctx, c_ctx, w_mod, b_mod, g_norm1, w_in, g_qnorm, g_knorm,
              lam_q1, lam_k1, lam_q2, lam_k2, g_subln, ssm_a_re, ssm_a_im, ssm_log_dt,
              ssm_b_re, ssm_b_im, ssm_c_re, ssm_c_im, ssm_d, w_glu, b_glu, w_fnet,
              w_out, g_norm2, w_ff1, w_ff2):
    rope = axial_rope_tables(x.shape[1])
    c_act = jax.nn.silu(c)
    cctx_act = jax.nn.silu(c_ctx)
    for i in range(DEPTH):
        p = {
            'w_mod': w_mod[i], 'b_mod': b_mod[i], 'g_norm1': g_norm1[i], 'w_in': w_in[i],
            'g_qnorm': g_qnorm[i], 'g_knorm': g_knorm[i],
            'lam_q1': lam_q1[i], 'lam_k1': lam_k1[i], 'lam_q2': lam_q2[i], 'lam_k2': lam_k2[i],
            'g_subln': g_subln[i],
            'ssm_a_re': ssm_a_re[i], 'ssm_a_im': ssm_a_im[i], 'ssm_log_dt': ssm_log_dt[i],
            'ssm_b_re': ssm_b_re[i], 'ssm_b_im': ssm_b_im[i],
            'ssm_c_re': ssm_c_re[i], 'ssm_c_im': ssm_c_im[i], 'ssm_d': ssm_d[i],
            'w_glu': w_glu[i], 'b_glu': b_glu[i], 'w_fnet': w_fnet[i], 'w_out': w_out[i],
            'g_norm2': g_norm2[i], 'w_ff1': w_ff1[i], 'w_ff2': w_ff2[i],
        }
        x, ctx = trunk_layer(x, ctx, c_act, cctx_act, rope, p, i, i < DEPTH - 1)
    return x
```

```python
import functools
import math

import numpy as np
import jax
import jax.numpy as jnp
from jax import lax
from jax.experimental import pallas as pl
from jax.experimental.pallas import tpu as pltpu

F32 = jnp.float32
BF16 = jnp.bfloat16

D_MODEL = 1024
GRID_W = 64
N_HEADS = 4
HEAD_DIM = 64
V_DIM = 2 * HEAD_DIM
QK_WIDTH = N_HEADS * 2 * HEAD_DIM
ATTN_WIDTH = N_HEADS * V_DIM
SSM_WIDTH = D_MODEL // 4
SSM_P = 16
SSM_G = SSM_WIDTH // SSM_P
SSM_N = 64
SSM_STATE = SSM_G * SSM_N
FNET_WIDTH = D_MODEL // 4
FNET_G = 4
FNET_C = FNET_WIDTH // FNET_G
IN_WIDTH = 2 * QK_WIDTH + ATTN_WIDTH + SSM_WIDTH + FNET_WIDTH
D_FF = 4 * D_MODEL
ROPE_BASE = 10000.0
EPS = 1e-6
SCALE = HEAD_DIM ** -0.5

TILE = 256
LANES = 128
SUBLANES = 8
FF_CHUNK = 1024
VMEM_LIMIT = 48 * 1024 * 1024


def _const_spec(shape, layer=None):
    nd = len(shape)
    if layer is None:
        return pl.BlockSpec(shape, lambda *_: (0,) * nd, pipeline_mode=pl.Buffered(1))
    return pl.BlockSpec((1,) + tuple(shape), lambda *_: (layer,) + (0,) * nd,
                        pipeline_mode=pl.Buffered(1))


def _params(sem):
    return pltpu.CompilerParams(dimension_semantics=sem, vmem_limit_bytes=VMEM_LIMIT)


def _mod_kernel(act_ref, w_ref, b_ref, o_ref):
    a = act_ref[...]
    a = a * jax.nn.sigmoid(a)
    o_ref[0] = jnp.dot(a.astype(BF16), w_ref[0].astype(BF16),
                       preferred_element_type=F32) + b_ref[0]


def _modulation(act, w_mod, b_mod):
    depth, d, n = w_mod.shape
    bn = 512
    return pl.pallas_call(
        _mod_kernel,
        grid=(depth, n // bn),
        in_specs=[pl.BlockSpec((SUBLANES, d), lambda l, j: (0, 0)),
                  pl.BlockSpec((1, d, bn), lambda l, j: (l, 0, j)),
                  pl.BlockSpec((1, 1, bn), lambda l, j: (l, 0, j))],
        out_specs=pl.BlockSpec((1, SUBLANES, bn), lambda l, j: (l, 0, j)),
        out_shape=jax.ShapeDtypeStruct((depth, SUBLANES, n), F32),
        compiler_params=_params(("parallel", "parallel")),
        name="modulation",
    )(act, w_mod, b_mod.reshape(depth, 1, n))


def _inproj_kernel(x_ref, mod_ref, g1_ref, w_ref, gq_ref, gk_ref, e_ref, cos_ref, sin_ref,
                   q_ref, kt_ref, ve_ref, u_ref, f_ref):
    d = D_MODEL
    x = x_ref[0]
    ms = jnp.mean(x * x, axis=-1, keepdims=True)
    xn = x * lax.rsqrt(ms + EPS) * g1_ref[0]
    mod = mod_ref[0, 0]
    h = xn * (1.0 + mod[:, d:2 * d]) + mod[:, :d]
    proj = jnp.dot(h.astype(BF16), w_ref[0], preferred_element_type=F32)
    e = e_ref[...]
    cos = cos_ref[...]
    sin = sin_ref[...]

    def qk_norm(z, g):
        sq = z * z
        hi = sq.astype(BF16)
        lo = (sq - hi.astype(F32)).astype(BF16)
        parts = []
        for j in range(QK_WIDTH // 256):
            sl = slice(j * 256, (j + 1) * 256)
            parts.append(jnp.dot(hi[:, sl], e, preferred_element_type=F32)
                         + jnp.dot(lo[:, sl], e, preferred_element_type=F32))
        ssum = jnp.concatenate(parts, axis=-1)
        return z * lax.rsqrt(ssum * (1.0 / HEAD_DIM) + EPS) * g

    def rope(zh):
        return zh * cos + pltpu.roll(zh, LANES // 2, 1) * sin

    qn = qk_norm(proj[:, :QK_WIDTH], gq_ref[0])
    kn = qk_norm(proj[:, QK_WIDTH:2 * QK_WIDTH], gk_ref[0])
    row_map = (lax.broadcasted_iota(jnp.int32, (LANES, TILE), 0) // 32) % 2
    ones = jnp.ones((TILE, LANES), BF16)
    for hh in range(N_HEADS):
        sl = slice(hh * LANES, (hh + 1) * LANES)
        q_ref[0, :, sl] = (rope(qn[:, sl]) * SCALE).astype(BF16)
        kt = rope(kn[:, sl]).T
        kt_ref[0, hh, 0, 0] = jnp.where(row_map == 0, kt, 0.0).astype(BF16)
        kt_ref[0, hh, 1, 0] = jnp.where(row_map == 1, kt, 0.0).astype(BF16)
        vo = 2 * QK_WIDTH + hh * V_DIM
        ve_ref[0, hh, :, :V_DIM] = proj[:, vo:vo + V_DIM].astype(BF16)
        ve_ref[0, hh, :, V_DIM:] = ones
    uo = 2 * QK_WIDTH + ATTN_WIDTH
    u_ref[0] = proj[:, uo:uo + SSM_WIDTH]
    f_ref[0] = proj[:, uo + SSM_WIDTH:]


def _in_projection(layer, xs, mods, g_norm1, w_in, gq, gk, e_mat, cos_t, sin_t):
    b, r, d = xs.shape
    nt = r // TILE
    depth = w_in.shape[0]
    return pl.pallas_call(
        _inproj_kernel,
        grid=(b, nt),
        in_specs=[
            pl.BlockSpec((1, TILE, d), lambda i, t: (i, t, 0)),
            pl.BlockSpec((1, 1, 1, 6 * d),
                         lambda i, t: (layer, jnp.where(t == nt - 1, b, i), 0, 0)),
            _const_spec((1, d), layer),
            _const_spec((d, IN_WIDTH), layer),
            _const_spec((1, QK_WIDTH), layer),
            _const_spec((1, QK_WIDTH), layer),
            _const_spec((256, 256)),
            pl.BlockSpec((TILE, LANES), lambda i, t: (t, 0)),
            pl.BlockSpec((TILE, LANES), lambda i, t: (t, 0)),
        ],
        out_specs=[
            pl.BlockSpec((1, TILE, QK_WIDTH), lambda i, t: (i, t, 0)),
            pl.BlockSpec((1, N_HEADS, 2, 1, LANES, TILE), lambda i, t: (i, 0, 0, t, 0, 0)),
            pl.BlockSpec((1, N_HEADS, TILE, 2 * V_DIM), lambda i, t: (i, 0, t, 0)),
            pl.BlockSpec((1, TILE, SSM_WIDTH), lambda i, t: (i, t, 0)),
            pl.BlockSpec((1, TILE, FNET_WIDTH), lambda i, t: (i, t, 0)),
        ],
        out_shape=[
            jax.ShapeDtypeStruct((b, r, QK_WIDTH), BF16),
            jax.ShapeDtypeStruct((b, N_HEADS, 2, nt, LANES, TILE), BF16),
            jax.ShapeDtypeStruct((b, N_HEADS, r, 2 * V_DIM), BF16),
            jax.ShapeDtypeStruct((b, r, SSM_WIDTH), F32),
            jax.ShapeDtypeStruct((b, r, FNET_WIDTH), F32),
        ],
        compiler_params=_params(("parallel", "parallel")),
        name="in_projection",
    )(xs, mods.reshape(depth, SUBLANES, 1, 6 * d), g_norm1.reshape(depth, 1, d), w_in,
      gq, gk, e_mat, cos_t, sin_t)


def _attn_kernel(q_ref, kt_ref, ve_ref, lam_ref, gs_ref, o_ref, m_ref, acc_ref, *, nt):
    t = pl.program_id(2)
    q = q_ref[0]
    m_ref[...] = jnp.full(m_ref.shape, -jnp.inf, F32)
    acc_ref[...] = jnp.zeros(acc_ref.shape, F32)

    def key_block(i):
        v = ve_ref[0, 0, pl.ds(pl.multiple_of(i * TILE, TILE), TILE), :]
        for mp in range(2):
            s = jnp.dot(q, kt_ref[0, 0, mp, i], preferred_element_type=F32)
            m_prev = m_ref[mp]
            m_new = jnp.maximum(m_prev, jnp.max(s, axis=1, keepdims=True))
            alpha = jnp.exp(m_prev - m_new)
            p = jnp.exp(s - jnp.tile(m_new, (1, TILE // LANES)))
            acc_ref[mp] = (acc_ref[mp] * jnp.tile(alpha, (1, 2))
                           + jnp.dot(p.astype(BF16), v, preferred_element_type=F32))
            m_ref[mp] = m_new

    @pl.when(t < nt - 1)
    def _latent_keys():
        def body(i, carry):
            key_block(i)
            return carry
        lax.fori_loop(0, nt - 1, body, 0)

    key_block(nt - 1)

    a0 = acc_ref[0]
    a1 = acc_ref[1]
    o1 = a0[:, :V_DIM] / a0[:, V_DIM:]
    o2 = a1[:, :V_DIM] / a1[:, V_DIM:]
    lp = lam_ref[0]
    s1 = jnp.sum(lp[0:1] * lp[1:2], axis=-1, keepdims=True)
    s2 = jnp.sum(lp[2:3] * lp[3:4], axis=-1, keepdims=True)
    lam = jnp.exp(s1) - jnp.exp(s2) + lp[4:5, 0:1]
    a = o1 - lam * o2
    a = a * lax.rsqrt(jnp.mean(a * a, axis=-1, keepdims=True) + EPS)
    o_ref[0] = (a * gs_ref[0] * lp[4:5, 1:2]).astype(BF16)


def _attention(layer, q, kt, ve, lam_p, g_subln):
    b, r, _ = q.shape
    nt = r // TILE
    return pl.pallas_call(
        functools.partial(_attn_kernel, nt=nt),
        grid=(b, N_HEADS, nt),
        in_specs=[
            pl.BlockSpec((1, TILE, LANES), lambda i, h, t: (i, t, h)),
            pl.BlockSpec((1, 1, 2, nt, LANES, TILE), lambda i, h, t: (i, h, 0, 0, 0, 0)),
            pl.BlockSpec((1, 1, r, 2 * V_DIM), lambda i, h, t: (i, h, 0, 0)),
            _const_spec((SUBLANES, LANES), layer),
            _const_spec((1, V_DIM), layer),
        ],
        out_specs=pl.BlockSpec((1, TILE, V_DIM), lambda i, h, t: (i, t, h)),
        out_shape=jax.ShapeDtypeStruct((b, r, ATTN_WIDTH), BF16),
        scratch_shapes=[pltpu.VMEM((2, TILE, LANES), F32),
                        pltpu.VMEM((2, TILE, 2 * V_DIM), F32)],
        compiler_params=_params(("parallel", "parallel", "arbitrary")),
        name="diff_attention",
    )(q, kt, ve, lam_p, g_subln)


def _s5_kernel(*refs, reverse, final):
    if final:
        (u_ref, are_ref, aim_ref, ldt_ref, bre_ref, bim_ref, c_ref,
         yprev_ref, d_ref, wglu_ref, bglu_ref,
         o_ref, bbar_ref, pw_ref, carry_ref, s_ref) = refs
    else:
        (u_ref, are_ref, aim_ref, ldt_ref, bre_ref, bim_ref, c_ref,
         o_ref, bbar_ref, pw_ref, carry_ref, s_ref) = refs
    ns = SSM_STATE
    shape = (SUBLANES, ns)

    @pl.when(pl.program_id(1) == 0)
    def _init():
        a_re = are_ref[0, 0]
        a_im = aim_ref[0, 0]
        dt = jnp.exp(ldt_ref[0, 0])
        mag = jnp.exp(dt * a_re)
        ang = dt * a_im
        ab_re = mag * jnp.cos(ang)
        ab_im = mag * jnp.sin(ang)
        den = a_re * a_re + a_im * a_im
        n_re = ab_re - 1.0
        f_re = (n_re * a_re + ab_im * a_im) / den
        f_im = (ab_im * a_re - n_re * a_im) / den
        bre = bre_ref[0, 0]
        bim = bim_ref[0, 0]
        bbar_ref[:, :ns] = (f_re * bre - f_im * bim).astype(BF16)
        bbar_ref[:, ns:] = (f_re * bim + f_im * bre).astype(BF16)
        pows = [(ab_re, ab_im)]
        for _ in range(SUBLANES - 1):
            pr, pi = pows[-1]
            pows.append((pr * ab_re - pi * ab_im, pr * ab_im + pi * ab_re))
        row = lax.broadcasted_iota(jnp.int32, shape, 0)
        zero = jnp.zeros(shape, F32)
        for idx, k in enumerate((1, 2, 4)):
            mask = (row + k <= SUBLANES - 1) if reverse else (row >= k)
            pw_ref[2 * idx] = jnp.where(mask, jnp.broadcast_to(pows[k - 1][0], shape), zero)
            pw_ref[2 * idx + 1] = jnp.where(mask, jnp.broadcast_to(pows[k - 1][1], shape), zero)
        pcr = zero
        pci = zero
        for tt in range(SUBLANES):
            e = (SUBLANES - 1 - tt) if reverse else tt
            pcr = jnp.where(row == tt, jnp.broadcast_to(pows[e][0], shape), pcr)
            pci = jnp.where(row == tt, jnp.broadcast_to(pows[e][1], shape), pci)
        pw_ref[6] = pcr
        pw_ref[7] = pci
        carry_ref[...] = jnp.zeros(carry_ref.shape, F32)

    u = u_ref[0]
    s_ref[...] = jnp.dot(u.astype(BF16), bbar_ref[...], preferred_element_type=F32)
    ngroups = TILE // SUBLANES

    def group(g, carry):
        cr, ci = carry
        gi = (ngroups - 1 - g) if reverse else g
        r0 = pl.multiple_of(gi * SUBLANES, SUBLANES)
        xr = s_ref[pl.ds(r0, SUBLANES), :ns]
        xi = s_ref[pl.ds(r0, SUBLANES), ns:]
        for idx, k in enumerate((1, 2, 4)):
            sh = (SUBLANES - k) if reverse else k
            sr = pltpu.roll(xr, sh, 0)
            si = pltpu.roll(xi, sh, 0)
            pr = pw_ref[2 * idx]
            pi = pw_ref[2 * idx + 1]
            xr, xi = xr + pr * sr - pi * si, xi + pr * si + pi * sr
        pr = pw_ref[6]
        pi = pw_ref[7]
        xr, xi = xr + pr * cr - pi * ci, xi + pr * ci + pi * cr
        s_ref[pl.ds(r0, SUBLANES), :ns] = xr
        s_ref[pl.ds(r0, SUBLANES), ns:] = xi
        last = 0 if reverse else SUBLANES - 1
        return (jnp.broadcast_to(xr[last:last + 1], shape),
                jnp.broadcast_to(xi[last:last + 1], shape))

    cr, ci = lax.fori_loop(0, ngroups, group, (carry_ref[0], carry_ref[1]))
    carry_ref[0] = cr
    carry_ref[1] = ci
    y = jnp.dot(s_ref[...].astype(BF16), c_ref[0, 0], preferred_element_type=F32)
    if final:
        y = y + yprev_ref[0] + d_ref[0] * u
        h = jax.nn.gelu(y)
        z = jnp.dot(h.astype(BF16), wglu_ref[0], preferred_element_type=F32) + bglu_ref[0]
        o_ref[0] = (h * jax.nn.sigmoid(z)).astype(BF16)
    else:
        o_ref[0] = y


def _s5_pass(layer, direction, u, a_re, a_im, ldt, b_re, b_im, c_stack, final_args=None):
    b, r, w = u.shape
    nt = r // TILE
    reverse = direction == 1
    final = final_args is not None
    if reverse:
        def chunk(i, s):
            return (i, nt - 1 - s, 0)
    else:
        def chunk(i, s):
            return (i, jnp.where(s == 0, nt - 1, s - 1), 0)

    def dir_spec(shape):
        nd = len(shape)
        return pl.BlockSpec((1, 1) + tuple(shape), lambda *_: (layer, direction) + (0,) * nd,
                            pipeline_mode=pl.Buffered(1))

    in_specs = [
        pl.BlockSpec((1, TILE, w), chunk),
        dir_spec((1, SSM_STATE)), dir_spec((1, SSM_STATE)), dir_spec((1, SSM_STATE)),
        dir_spec((w, SSM_STATE)), dir_spec((w, SSM_STATE)),
        dir_spec((2 * SSM_STATE, w)),
    ]
    args = [u, a_re, a_im, ldt, b_re, b_im, c_stack]
    if final:
        y_prev, d_skip, w_glu, b_glu = final_args
        in_specs += [pl.BlockSpec((1, TILE, w), chunk),
                     _const_spec((1, w), layer), _const_spec((w, w), layer),
                     _const_spec((1, w), layer)]
        args += [y_prev, d_skip, w_glu, b_glu]
    return pl.pallas_call(
        functools.partial(_s5_kernel, reverse=reverse, final=final),
        grid=(b, nt),
        in_specs=in_specs,
        out_specs=pl.BlockSpec((1, TILE, w), chunk),
        out_shape=jax.ShapeDtypeStruct((b, r, w), BF16 if final else F32),
        scratch_shapes=[pltpu.VMEM((w, 2 * SSM_STATE), BF16),
                        pltpu.VMEM((8, SUBLANES, SSM_STATE), F32),
                        pltpu.VMEM((2, SUBLANES, SSM_STATE), F32),
                        pltpu.VMEM((TILE, 2 * SSM_STATE), F32)],
        compiler_params=_params(("parallel", "arbitrary")),
        name="s5_reverse" if reverse else "s5_forward",
    )(*args)


def _fourier_kernel(cs_ref, f_ref, m_ref, c64_ref, wf_ref, *rest, l2n, blk0):
    o_ref, g_ref = rest[-2], rest[-1]
    k2 = pl.program_id(1)
    rows = 64
    for rb in range(TILE // rows):
        def body(l2, acc, rb=rb):
            gr, gi = acc
            r0 = pl.multiple_of((blk0 + l2) * TILE + rb * rows, rows)
            xb = f_ref[0, pl.ds(r0, rows), :]
            return gr + cs_ref[k2, l2] * xb, gi - cs_ref[l2n + k2, l2] * xb
        zero = jnp.zeros((rows, FNET_WIDTH), F32)
        gr, gi = lax.fori_loop(0, l2n, body, (zero, zero))
        g_ref[rb * rows:(rb + 1) * rows, :] = gr.astype(BF16)
        g_ref[TILE + rb * rows:TILE + (rb + 1) * rows, :] = gi.astype(BF16)
    y = jnp.dot(m_ref[0], g_ref[...], preferred_element_type=F32)
    z = (jnp.dot(y[:TILE].astype(BF16), c64_ref[:FNET_WIDTH], preferred_element_type=F32)
         + jnp.dot(y[TILE:].astype(BF16), c64_ref[FNET_WIDTH:], preferred_element_type=F32))
    o_ref[0] = jnp.dot(z.astype(BF16), wf_ref[0], preferred_element_type=F32).astype(BF16)


def _fourier_latent(layer, f, cs_tab, m_tab, c64_tab, wf_blk):
    b, r, w = f.shape
    l2n = cs_tab.shape[1]
    out = pl.pallas_call(
        functools.partial(_fourier_kernel, l2n=l2n, blk0=0),
        grid=(b, l2n),
        in_specs=[
            pl.BlockSpec(memory_space=pltpu.SMEM),
            pl.BlockSpec((1, r, w), lambda i, k: (i, 0, 0)),
            pl.BlockSpec((1, 2 * TILE, 2 * TILE), lambda i, k: (k, 0, 0)),
            _const_spec((2 * w, w)),
            _const_spec((w, w), layer),
        ],
        out_specs=pl.BlockSpec((1, TILE, w), lambda i, k: (i, 0, k)),
        out_shape=jax.ShapeDtypeStruct((b, r // l2n, l2n * w), BF16),
        scratch_shapes=[pltpu.VMEM((2 * TILE, w), BF16)],
        compiler_params=_params(("parallel", "arbitrary")),
        name="fourier_latent",
    )(cs_tab, f, m_tab, c64_tab, wf_blk)
    return out.reshape(b, r, w)


def _fourier_context(layer, f, fn, cs_tab, m_tab, c64_tab, wf_blk):
    b, r, w = f.shape
    nt = r // TILE
    return pl.pallas_call(
        functools.partial(_fourier_kernel, l2n=1, blk0=0),
        grid=(b, 1),
        in_specs=[
            pl.BlockSpec(memory_space=pltpu.SMEM),
            pl.BlockSpec((1, TILE, w), lambda i, k: (i, nt - 1, 0)),
            pl.BlockSpec((1, 2 * TILE, 2 * TILE), lambda i, k: (0, 0, 0)),
            _const_spec((2 * w, w)),
            _const_spec((w, w), layer),
            pl.BlockSpec(memory_space=pl.ANY),
        ],
        out_specs=pl.BlockSpec((1, TILE, w), lambda i, k: (i, nt - 1, 0)),
        out_shape=jax.ShapeDtypeStruct((b, r, w), BF16),
        scratch_shapes=[pltpu.VMEM((2 * TILE, w), BF16)],
        input_output_aliases={5: 0},
        compiler_params=_params(("parallel", "arbitrary")),
        name="fourier_context",
    )(cs_tab, f, m_tab, c64_tab, wf_blk, fn)


def _outffn_kernel(x_ref, a_ref, s_ref, fn_ref, mod_ref, wo_ref, g2_ref, w1_ref, w2_ref, o_ref):
    d = D_MODEL
    x = x_ref[0]
    mod = mod_ref[0, 0]
    o1 = ATTN_WIDTH
    o2 = ATTN_WIDTH + SSM_WIDTH
    mix = (jnp.dot(a_ref[0], wo_ref[0, :o1], preferred_element_type=F32)
           + jnp.dot(s_ref[0], wo_ref[0, o1:o2], preferred_element_type=F32)
           + jnp.dot(fn_ref[0], wo_ref[0, o2:], preferred_element_type=F32))
    x1 = x + mod[:, 2 * d:3 * d] * mix
    ms = jnp.mean(x1 * x1, axis=-1, keepdims=True)
    hn = x1 * lax.rsqrt(ms + EPS) * g2_ref[0]
    h = (hn * (1.0 + mod[:, 4 * d:5 * d]) + mod[:, 3 * d:4 * d]).astype(BF16)
    acc = jnp.zeros((TILE, d), F32)
    for cc in range(D_FF // FF_CHUNK):
        sl = slice(cc * FF_CHUNK, (cc + 1) * FF_CHUNK)
        t = jnp.dot(h, w1_ref[0, :, sl], preferred_element_type=F32)
        t = jnp.square(jnp.maximum(t, 0.0))
        acc = acc + jnp.dot(t.astype(BF16), w2_ref[0, sl, :], preferred_element_type=F32)
    o_ref[0] = x1 + mod[:, 5 * d:] * acc


def _out_ffn(layer, xs, attn, s5, fn, mods, w_out, g_norm2, w_ff1, w_ff2):
    b, r, d = xs.shape
    nt = r // TILE
    depth = w_out.shape[0]

    def tile(width):
        return pl.BlockSpec((1, TILE, width), lambda i, t: (i, t, 0))

    return pl.pallas_call(
        _outffn_kernel,
        grid=(b, nt),
        in_specs=[
            tile(d), tile(ATTN_WIDTH), tile(SSM_WIDTH), tile(FNET_WIDTH),
            pl.BlockSpec((1, 1, 1, 6 * d),
                         lambda i, t: (layer, jnp.where(t == nt - 1, b, i), 0, 0)),
            _const_spec((d, d), layer),
            _const_spec((1, d), layer),
            _const_spec((d, D_FF), layer),
            _const_spec((D_FF, d), layer),
        ],
        out_specs=tile(d),
        out_shape=jax.ShapeDtypeStruct((b, r, d), F32),
        input_output_aliases={0: 0},
        compiler_params=_params(("parallel", "parallel")),
        name="out_ffn",
    )(xs, attn, s5, fn, mods.reshape(depth, SUBLANES, 1, 6 * d), w_out,
      g_norm2.reshape(depth, 1, d), w_ff1, w_ff2)


def _head_lane_tables():
    j = np.arange(LANES)
    half = j // 64
    mp = (j // 32) % 2
    idx = j % 32
    src_in_head = mp * HEAD_DIM + half * 32 + idx
    gain_idx = half * 32 + idx
    return src_in_head, gain_idx, mp


def _in_proj_column_order():
    src_in_head, _, _ = _head_lane_tables()
    qk = np.concatenate([h * LANES + src_in_head for h in range(N_HEADS)])
    return np.concatenate([qk, QK_WIDTH + qk, np.arange(2 * QK_WIDTH, IN_WIDTH)])


def _same_map_matrix():
    i = np.arange(256)
    head = i // LANES
    mp = ((i % LANES) // 32) % 2
    same = (head[:, None] == head[None, :]) & (mp[:, None] == mp[None, :])
    return same.astype(np.float32)


def _rope_tables(seq, ctx_len):
    t = jnp.arange(seq)
    row = (t // GRID_W).astype(F32)
    col = (t % GRID_W).astype(F32)
    n_freq = HEAD_DIM // 4
    inv = jnp.power(ROPE_BASE, -jnp.arange(n_freq, dtype=F32) / n_freq)
    ang = jnp.concatenate([row[:, None] * inv, col[:, None] * inv], axis=-1)
    cos = jnp.tile(jnp.cos(ang), (1, 4))
    sign = np.where(np.arange(LANES) < LANES // 2, -1.0, 1.0).astype(np.float32)
    sin = jnp.tile(jnp.sin(ang), (1, 4)) * sign
    cos = jnp.concatenate([cos, jnp.ones((ctx_len, LANES), F32)], axis=0)
    sin = jnp.concatenate([sin, jnp.zeros((ctx_len, LANES), F32)], axis=0)
    return cos, sin


def _dft_tables(length):
    l2n = length // TILE
    k2 = np.arange(l2n)
    phi = 2.0 * np.pi * ((k2[:, None] * k2[None, :]) % l2n) / l2n
    cs = np.concatenate([np.cos(phi), np.sin(phi)], axis=0).astype(np.float32)
    l1 = np.arange(TILE)
    alpha = 2.0 * np.pi * ((k2[:, None] * l1[None, :]) % length) / length
    beta = 2.0 * np.pi * ((l1[:, None] * l1[None, :]) % TILE) / TILE
    ca = jnp.asarray(np.cos(alpha).astype(np.float32))[:, None, :]
    sa = jnp.asarray(np.sin(alpha).astype(np.float32))[:, None, :]
    cb = jnp.asarray(np.cos(beta).astype(np.float32))[None]
    sb = jnp.asarray(np.sin(beta).astype(np.float32))[None]
    norm = 1.0 / math.sqrt(length)
    ct = (ca * cb - sa * sb) * norm
    st = (sa * cb + ca * sb) * norm
    m = jnp.concatenate([jnp.concatenate([ct, st], axis=2),
                         jnp.concatenate([-st, ct], axis=2)], axis=1).astype(BF16)
    return jnp.asarray(cs), m


def _channel_dft_table():
    c = np.arange(FNET_C)
    th = 2.0 * np.pi * ((c[:, None] * c[None, :]) % FNET_C) / FNET_C
    eye = np.eye(FNET_G)
    norm = 1.0 / math.sqrt(FNET_C)
    cblk = np.kron(eye, np.cos(th)) * norm
    sblk = np.kron(eye, np.sin(th)) * norm
    return jnp.asarray(np.concatenate([cblk, sblk], axis=0).astype(np.float32)).astype(BF16)


def _block_diag(blocks):
    g = blocks.shape[-3]
    eye = jnp.eye(g, dtype=blocks.dtype)
    out = jnp.einsum('...gab,gh->...gahb', blocks, eye)
    return out.reshape(blocks.shape[:-3] + (g * blocks.shape[-2], g * blocks.shape[-1]))


def kernel(x, c, ctx, c_ctx, w_mod, b_mod, g_norm1, w_in, g_qnorm, g_knorm, lam_q1, lam_k1, lam_q2, lam_k2, g_subln, ssm_a_re, ssm_a_im, ssm_log_dt, ssm_b_re, ssm_b_im, ssm_c_re, ssm_c_im, ssm_d, w_glu, b_glu, w_fnet, w_out, g_norm2, w_ff1, w_ff2):
    bsz, seq, d = x.shape
    ctx_len = ctx.shape[1]
    depth = w_mod.shape[0]
    assert d == D_MODEL and ctx_len == TILE and seq % TILE == 0 and seq % GRID_W == 0
    assert bsz + 1 <= SUBLANES

    xs = jnp.concatenate([x, ctx], axis=1)
    act = jnp.concatenate([c, c_ctx[None], jnp.zeros((SUBLANES - bsz - 1, d), F32)], axis=0)
    mods = _modulation(act, w_mod, b_mod)

    w_in_p = jnp.take(w_in, jnp.asarray(_in_proj_column_order()), axis=2).astype(BF16)
    _, gain_idx, _ = _head_lane_tables()
    gq = jnp.tile(g_qnorm[:, gain_idx], (1, N_HEADS)).reshape(depth, 1, QK_WIDTH)
    gk = jnp.tile(g_knorm[:, gain_idx], (1, N_HEADS)).reshape(depth, 1, QK_WIDTH)
    e_mat = jnp.asarray(_same_map_matrix()).astype(BF16)
    cos_t, sin_t = _rope_tables(seq, ctx_len)
    lam_init = np.array([0.8 - 0.6 * math.exp(-0.3 * i) for i in range(depth)], np.float32)
    pad = jnp.zeros((depth, LANES - HEAD_DIM), F32)
    lam_rows = [jnp.concatenate([v, pad], axis=1) for v in (lam_q1, lam_k1, lam_q2, lam_k2)]
    const_row = np.zeros((depth, LANES), np.float32)
    const_row[:, 0] = lam_init
    const_row[:, 1] = 1.0 - lam_init
    lam_p = jnp.stack(lam_rows + [jnp.asarray(const_row)]
                      + [jnp.zeros((depth, LANES), F32)] * (SUBLANES - 5), axis=1)
    g_sub = g_subln.reshape(depth, 1, V_DIM)

    a_re = ssm_a_re.reshape(depth, 2, 1, SSM_STATE)
    a_im = ssm_a_im.reshape(depth, 2, 1, SSM_STATE)
    ldt = jnp.repeat(ssm_log_dt, SSM_N, axis=-1).reshape(depth, 2, 1, SSM_STATE)
    b_re = _block_diag(jnp.swapaxes(ssm_b_re, -1, -2))
    b_im = _block_diag(jnp.swapaxes(ssm_b_im, -1, -2))
    c_stack = jnp.concatenate([_block_diag(jnp.swapaxes(ssm_c_re, -1, -2)),
                               -_block_diag(jnp.swapaxes(ssm_c_im, -1, -2))],
                              axis=2).astype(BF16)
    d_skip = ssm_d.reshape(depth, 1, SSM_WIDTH)
    w_glu_b = w_glu.astype(BF16)
    b_glu_r = b_glu.reshape(depth, 1, SSM_WIDTH)

    cs_lat, m_lat = _dft_tables(seq)
    cs_ctx, m_ctx = _dft_tables(ctx_len)
    c64_tab = _channel_dft_table()
    wf_blk = _block_diag(w_fnet).astype(BF16)

    w_out_b = w_out.astype(BF16)
    w_ff1_b = w_ff1.astype(BF16)
    w_ff2_b = w_ff2.astype(BF16)

    for layer in range(depth):
        q, kt, ve, u, f = _in_projection(layer, xs, mods, g_norm1, w_in_p, gq, gk, e_mat,
                                         cos_t, sin_t)
        attn = _attention(layer, q, kt, ve, lam_p, g_sub)
        y_fwd = _s5_pass(layer, 0, u, a_re, a_im, ldt, b_re, b_im, c_stack)
        s5 = _s5_pass(layer, 1, u, a_re, a_im, ldt, b_re, b_im, c_stack,
                      final_args=(y_fwd, d_skip, w_glu_b, b_glu_r))
        fn = _fourier_latent(layer, f, cs_lat, m_lat, c64_tab, wf_blk)
        fn = _fourier_context(layer, f, fn, cs_ctx, m_ctx, c64_tab, wf_blk)
        xs = _out_ffn(layer, xs, attn, s5, fn, mods, w_out_b, g_norm2, w_ff1_b, w_ff2_b)
    return xs[:, :seq]
```

```python
import functools
import math

import numpy as np
import jax
import jax.numpy as jnp
from jax import lax
from jax.experimental import pallas as pl
from jax.experimental.pallas import tpu as pltpu

F32 = jnp.float32
BF16 = jnp.bfloat16

D_MODEL = 1024
GRID_W = 64
N_HEADS = 4
HEAD_DIM = 64
V_DIM = 2 * HEAD_DIM
QK_WIDTH = N_HEADS * 2 * HEAD_DIM
ATTN_WIDTH = N_HEADS * V_DIM
SSM_WIDTH = D_MODEL // 4
SSM_P = 16
SSM_G = SSM_WIDTH // SSM_P
SSM_N = 64
SSM_STATE = SSM_G * SSM_N
FNET_WIDTH = D_MODEL // 4
FNET_G = 4
FNET_C = FNET_WIDTH // FNET_G
IN_WIDTH = 2 * QK_WIDTH + ATTN_WIDTH + SSM_WIDTH + FNET_WIDTH
D_FF = 4 * D_MODEL
ROPE_BASE = 10000.0
EPS = 1e-6
SCALE = HEAD_DIM ** -0.5
LOG2E = math.log2(math.e)

TILE = 256
LANES = 128
SUBLANES = 8
FF_CHUNK = 1024
FOURIER_CHUNK = 8192
FOURIER_GROUP = 4
VMEM_LIMIT = 48 * 1024 * 1024


def _const_spec(shape, layer=None):
    nd = len(shape)
    if layer is None:
        return pl.BlockSpec(shape, lambda *_: (0,) * nd, pipeline_mode=pl.Buffered(1))
    return pl.BlockSpec((1,) + tuple(shape), lambda *_: (layer,) + (0,) * nd,
                        pipeline_mode=pl.Buffered(1))


def _params(sem):
    return pltpu.CompilerParams(dimension_semantics=sem, vmem_limit_bytes=VMEM_LIMIT)


def _mod_kernel(act_ref, w_ref, b_ref, o_ref):
    a = act_ref[...]
    a = a * jax.nn.sigmoid(a)
    o_ref[0] = jnp.dot(a.astype(BF16), w_ref[0].astype(BF16),
                       preferred_element_type=F32) + b_ref[0]


def _modulation(act, w_mod, b_mod):
    depth, d, n = w_mod.shape
    bn = 512
    return pl.pallas_call(
        _mod_kernel,
        grid=(depth, n // bn),
        in_specs=[pl.BlockSpec((SUBLANES, d), lambda l, j: (0, 0)),
                  pl.BlockSpec((1, d, bn), lambda l, j: (l, 0, j)),
                  pl.BlockSpec((1, 1, bn), lambda l, j: (l, 0, j))],
        out_specs=pl.BlockSpec((1, SUBLANES, bn), lambda l, j: (l, 0, j)),
        out_shape=jax.ShapeDtypeStruct((depth, SUBLANES, n), F32),
        compiler_params=_params(("parallel", "parallel")),
        name="modulation",
    )(act, w_mod, b_mod.reshape(depth, 1, n))


def _inproj_kernel(x_ref, mod_ref, g1_ref, w_ref, gq_ref, gk_ref, e_ref, cos_ref, sin_ref,
                   q_ref, kt_ref, ve_ref, u_ref, f_ref):
    d = D_MODEL
    x = x_ref[0]
    ms = jnp.mean(x * x, axis=-1, keepdims=True)
    xn = x * lax.rsqrt(ms + EPS) * g1_ref[0]
    mod = mod_ref[0, 0]
    h = xn * (1.0 + mod[:, d:2 * d]) + mod[:, :d]
    proj = jnp.dot(h.astype(BF16), w_ref[0], preferred_element_type=F32)
    e = e_ref[...]
    cos = cos_ref[...]
    sin = sin_ref[...]

    def qk_norm(z, g):
        sq = z * z
        hi = sq.astype(BF16)
        lo = (sq - hi.astype(F32)).astype(BF16)
        parts = []
        for j in range(QK_WIDTH // 256):
            sl = slice(j * 256, (j + 1) * 256)
            parts.append(jnp.dot(hi[:, sl], e, preferred_element_type=F32)
                         + jnp.dot(lo[:, sl], e, preferred_element_type=F32))
        ssum = jnp.concatenate(parts, axis=-1)
        return z * lax.rsqrt(ssum * (1.0 / HEAD_DIM) + EPS) * g

    def rope(zh):
        return zh * cos + pltpu.roll(zh, LANES // 2, 1) * sin

    qn = qk_norm(proj[:, :QK_WIDTH], gq_ref[0])
    kn = qk_norm(proj[:, QK_WIDTH:2 * QK_WIDTH], gk_ref[0])
    row_map = (lax.broadcasted_iota(jnp.int32, (LANES, TILE), 0) // 32) % 2
    ones = jnp.ones((TILE, LANES), BF16)
    for hh in range(N_HEADS):
        sl = slice(hh * LANES, (hh + 1) * LANES)
        q_ref[0, :, sl] = (rope(qn[:, sl]) * (SCALE * LOG2E)).astype(BF16)
        kt = rope(kn[:, sl]).T
        kt_ref[0, hh, 0] = jnp.where(row_map == 0, kt, 0.0).astype(BF16)
        kt_ref[0, hh, 1] = jnp.where(row_map == 1, kt, 0.0).astype(BF16)
        vo = 2 * QK_WIDTH + hh * V_DIM
        ve_ref[0, hh, :, :V_DIM] = proj[:, vo:vo + V_DIM].astype(BF16)
        ve_ref[0, hh, :, V_DIM:] = ones
    uo = 2 * QK_WIDTH + ATTN_WIDTH
    u_ref[0] = proj[:, uo:uo + SSM_WIDTH]
    f_ref[0] = proj[:, uo + SSM_WIDTH:]


def _in_projection(layer, xs, mods, g_norm1, w_in, gq, gk, e_mat, cos_t, sin_t):
    b, r, d = xs.shape
    nt = r // TILE
    depth = w_in.shape[0]
    return pl.pallas_call(
        _inproj_kernel,
        grid=(b, nt),
        in_specs=[
            pl.BlockSpec((1, TILE, d), lambda i, t: (i, t, 0)),
            pl.BlockSpec((1, 1, 1, 6 * d),
                         lambda i, t: (layer, jnp.where(t == nt - 1, b, i), 0, 0)),
            _const_spec((1, d), layer),
            _const_spec((d, IN_WIDTH), layer),
            _const_spec((1, QK_WIDTH), layer),
            _const_spec((1, QK_WIDTH), layer),
            _const_spec((256, 256)),
            pl.BlockSpec((TILE, LANES), lambda i, t: (t, 0)),
            pl.BlockSpec((TILE, LANES), lambda i, t: (t, 0)),
        ],
        out_specs=[
            pl.BlockSpec((1, TILE, QK_WIDTH), lambda i, t: (i, t, 0)),
            pl.BlockSpec((1, N_HEADS, 2, LANES, TILE), lambda i, t: (i, 0, 0, 0, t)),
            pl.BlockSpec((1, N_HEADS, TILE, 2 * V_DIM), lambda i, t: (i, 0, t, 0)),
            pl.BlockSpec((1, TILE, SSM_WIDTH), lambda i, t: (i, t, 0)),
            pl.BlockSpec((1, TILE, FNET_WIDTH), lambda i, t: (i, t, 0)),
        ],
        out_shape=[
            jax.ShapeDtypeStruct((b, r, QK_WIDTH), BF16),
            jax.ShapeDtypeStruct((b, N_HEADS, 2, LANES, r), BF16),
            jax.ShapeDtypeStruct((b, N_HEADS, r, 2 * V_DIM), BF16),
            jax.ShapeDtypeStruct((b, r, SSM_WIDTH), F32),
            jax.ShapeDtypeStruct((b, r, FNET_WIDTH), F32),
        ],
        compiler_params=_params(("parallel", "parallel")),
        name="in_projection",
    )(xs, mods.reshape(depth, SUBLANES, 1, 6 * d), g_norm1.reshape(depth, 1, d), w_in,
      gq, gk, e_mat, cos_t, sin_t)


def _attn_kernel(q_ref, kt_ref, ve_ref, lam_ref, gs_ref, o_ref, *, nt, ctx_len):
    t = pl.program_id(2)
    q = q_ref[0]
    r = ve_ref.shape[2]

    def attend(k0):
        outs = []
        ve = ve_ref[0, 0, k0:, :]
        scores = [jnp.dot(q, kt_ref[0, 0, mp, :, k0:], preferred_element_type=F32)
                  for mp in range(2)]
        for s in scores:
            p = jnp.exp2(s - jnp.max(s, axis=1, keepdims=True))
            acc = jnp.dot(p.astype(BF16), ve, preferred_element_type=F32)
            outs.append(acc[:, :V_DIM] / acc[:, V_DIM:])
        lp = lam_ref[0]
        s1 = jnp.sum(lp[0:1] * lp[1:2], axis=-1, keepdims=True)
        s2 = jnp.sum(lp[2:3] * lp[3:4], axis=-1, keepdims=True)
        lam = jnp.exp(s1) - jnp.exp(s2) + lp[4:5, 0:1]
        a = outs[0] - lam * outs[1]
        a = a * lax.rsqrt(jnp.mean(a * a, axis=-1, keepdims=True) + EPS)
        o_ref[0] = (a * gs_ref[0] * lp[4:5, 1:2]).astype(BF16)

    @pl.when(t < nt - 1)
    def _latent_queries():
        attend(0)

    @pl.when(t == nt - 1)
    def _context_queries():
        attend(r - ctx_len)


def _attention(layer, q, kt, ve, lam_p, g_subln):
    b, r, _ = q.shape
    nt = r // TILE
    return pl.pallas_call(
        functools.partial(_attn_kernel, nt=nt, ctx_len=TILE),
        grid=(b, N_HEADS, nt),
        in_specs=[
            pl.BlockSpec((1, TILE, LANES), lambda i, h, t: (i, t, h)),
            pl.BlockSpec((1, 1, 2, LANES, r), lambda i, h, t: (i, h, 0, 0, 0),
                         pipeline_mode=pl.Buffered(1)),
            pl.BlockSpec((1, 1, r, 2 * V_DIM), lambda i, h, t: (i, h, 0, 0),
                         pipeline_mode=pl.Buffered(1)),
            _const_spec((SUBLANES, LANES), layer),
            _const_spec((1, V_DIM), layer),
        ],
        out_specs=pl.BlockSpec((1, TILE, V_DIM), lambda i, h, t: (i, t, h)),
        out_shape=jax.ShapeDtypeStruct((b, r, ATTN_WIDTH), BF16),
        compiler_params=_params(("parallel", "parallel", "arbitrary")),
        name="diff_attention",
    )(q, kt, ve, lam_p, g_subln)


def _s5_kernel(*refs, reverse, final):
    if final:
        (u_ref, are_ref, aim_ref, ldt_ref, bre_ref, bim_ref, c_ref,
         yprev_ref, d_ref, wglu_ref, bglu_ref,
         o_ref, bbar_ref, pw_ref, carry_ref, s_ref) = refs
    else:
        (u_ref, are_ref, aim_ref, ldt_ref, bre_ref, bim_ref, c_ref,
         o_ref, bbar_ref, pw_ref, carry_ref, s_ref) = refs
    ns = SSM_STATE
    shape = (SUBLANES, ns)

    @pl.when(pl.program_id(1) == 0)
    def _init():
        a_re = are_ref[0, 0]
        a_im = aim_ref[0, 0]
        dt = jnp.exp(ldt_ref[0, 0])
        mag = jnp.exp(dt * a_re)
        ang = dt * a_im
        ab_re = mag * jnp.cos(ang)
        ab_im = mag * jnp.sin(ang)
        den = a_re * a_re + a_im * a_im
        n_re = ab_re - 1.0
        f_re = (n_re * a_re + ab_im * a_im) / den
        f_im = (ab_im * a_re - n_re * a_im) / den
        bre = bre_ref[0, 0]
        bim = bim_ref[0, 0]
        bbar_ref[:, :ns] = (f_re * bre - f_im * bim).astype(BF16)
        bbar_ref[:, ns:] = (f_re * bim + f_im * bre).astype(BF16)
        pows = [(ab_re, ab_im)]
        for _ in range(SUBLANES - 1):
            pr, pi = pows[-1]
            pows.append((pr * ab_re - pi * ab_im, pr * ab_im + pi * ab_re))
        row = lax.broadcasted_iota(jnp.int32, shape, 0)
        zero = jnp.zeros(shape, F32)
        for idx, k in enumerate((1, 2, 4)):
            mask = (row + k <= SUBLANES - 1) if reverse else (row >= k)
            pw_ref[2 * idx] = jnp.where(mask, jnp.broadcast_to(pows[k - 1][0], shape), zero)
            pw_ref[2 * idx + 1] = jnp.where(mask, jnp.broadcast_to(pows[k - 1][1], shape), zero)
        pcr = zero
        pci = zero
        for tt in range(SUBLANES):
            e = (SUBLANES - 1 - tt) if reverse else tt
            pcr = jnp.where(row == tt, jnp.broadcast_to(pows[e][0], shape), pcr)
            pci = jnp.where(row == tt, jnp.broadcast_to(pows[e][1], shape), pci)
        pw_ref[6] = pcr
        pw_ref[7] = pci
        carry_ref[...] = jnp.zeros(carry_ref.shape, F32)

    u = u_ref[0]
    s_ref[...] = jnp.dot(u.astype(BF16), bbar_ref[...], preferred_element_type=F32)
    ngroups = TILE // SUBLANES

    def group(g, carry):
        cr, ci = carry
        gi = (ngroups - 1 - g) if reverse else g
        r0 = pl.multiple_of(gi * SUBLANES, SUBLANES)
        xr = s_ref[pl.ds(r0, SUBLANES), :ns]
        xi = s_ref[pl.ds(r0, SUBLANES), ns:]
        for idx, k in enumerate((1, 2, 4)):
            sh = (SUBLANES - k) if reverse else k
            sr = pltpu.roll(xr, sh, 0)
            si = pltpu.roll(xi, sh, 0)
            pr = pw_ref[2 * idx]
            pi = pw_ref[2 * idx + 1]
            xr, xi = xr + pr * sr - pi * si, xi + pr * si + pi * sr
        pr = pw_ref[6]
        pi = pw_ref[7]
        xr, xi = xr + pr * cr - pi * ci, xi + pr * ci + pi * cr
        s_ref[pl.ds(r0, SUBLANES), :ns] = xr
        s_ref[pl.ds(r0, SUBLANES), ns:] = xi
        last = 0 if reverse else SUBLANES - 1
        return (jnp.broadcast_to(xr[last:last + 1], shape),
                jnp.broadcast_to(xi[last:last + 1], shape))

    cr, ci = lax.fori_loop(0, ngroups, group, (carry_ref[0], carry_ref[1]))
    carry_ref[0] = cr
    carry_ref[1] = ci
    y = jnp.dot(s_ref[...].astype(BF16), c_ref[0, 0], preferred_element_type=F32)
    if final:
        y = y + yprev_ref[0] + d_ref[0] * u
        h = jax.nn.gelu(y)
        z = jnp.dot(h.astype(BF16), wglu_ref[0], preferred_element_type=F32) + bglu_ref[0]
        o_ref[0] = (h * jax.nn.sigmoid(z)).astype(BF16)
    else:
        o_ref[0] = y


def _s5_pass(layer, direction, u, a_re, a_im, ldt, b_re, b_im, c_stack, final_args=None):
    b, r, w = u.shape
    nt = r // TILE
    reverse = direction == 1
    final = final_args is not None
    if reverse:
        def chunk(i, s):
            return (i, nt - 1 - s, 0)
    else:
        def chunk(i, s):
            return (i, jnp.where(s == 0, nt - 1, s - 1), 0)

    def dir_spec(shape):
        nd = len(shape)
        return pl.BlockSpec((1, 1) + tuple(shape), lambda *_: (layer, direction) + (0,) * nd,
                            pipeline_mode=pl.Buffered(1))

    in_specs = [
        pl.BlockSpec((1, TILE, w), chunk),
        dir_spec((1, SSM_STATE)), dir_spec((1, SSM_STATE)), dir_spec((1, SSM_STATE)),
        dir_spec((w, SSM_STATE)), dir_spec((w, SSM_STATE)),
        dir_spec((2 * SSM_STATE, w)),
    ]
    args = [u, a_re, a_im, ldt, b_re, b_im, c_stack]
    if final:
        y_prev, d_skip, w_glu, b_glu = final_args
        in_specs += [pl.BlockSpec((1, TILE, w), chunk),
                     _const_spec((1, w), layer), _const_spec((w, w), layer),
                     _const_spec((1, w), layer)]
        args += [y_prev, d_skip, w_glu, b_glu]
    return pl.pallas_call(
        functools.partial(_s5_kernel, reverse=reverse, final=final),
        grid=(b, nt),
        in_specs=in_specs,
        out_specs=pl.BlockSpec((1, TILE, w), chunk),
        out_shape=jax.ShapeDtypeStruct((b, r, w), BF16 if final else F32),
        scratch_shapes=[pltpu.VMEM((w, 2 * SSM_STATE), BF16),
                        pltpu.VMEM((8, SUBLANES, SSM_STATE), F32),
                        pltpu.VMEM((2, SUBLANES, SSM_STATE), F32),
                        pltpu.VMEM((TILE, 2 * SSM_STATE), F32)],
        compiler_params=_params(("parallel", "arbitrary")),
        name="s5_reverse" if reverse else "s5_forward",
    )(*args)


def _fourier_outer_kernel(w_ref, x_ref, g_ref):
    g_ref[0] = jnp.dot(w_ref[...], x_ref[0].astype(BF16),
                       preferred_element_type=F32).astype(BF16)


def _fourier_outer(f, w_outer):
    b, r, w = f.shape
    nt = r // TILE
    l2n = nt - 1
    flat = TILE * w
    return pl.pallas_call(
        _fourier_outer_kernel,
        grid=(b, flat // FOURIER_CHUNK),
        in_specs=[_const_spec((2 * l2n, l2n)),
                  pl.BlockSpec((1, l2n, FOURIER_CHUNK), lambda i, j: (i, 0, j))],
        out_specs=pl.BlockSpec((1, 2 * l2n, FOURIER_CHUNK), lambda i, j: (i, 0, j)),
        out_shape=jax.ShapeDtypeStruct((b, 2 * l2n, flat), BF16),
        compiler_params=_params(("parallel", "parallel")),
        name="fourier_outer",
    )(w_outer, f.reshape(b, nt, flat))


def _fourier_tail(y_re, y_im, c64_ref, wf_ref):
    w = FNET_WIDTH
    z = (jnp.dot(y_re.astype(BF16), c64_ref[:w], preferred_element_type=F32)
         + jnp.dot(y_im.astype(BF16), c64_ref[w:], preferred_element_type=F32))
    return jnp.dot(z.astype(BF16), wf_ref[0], preferred_element_type=F32).astype(BF16)


def _fourier_inner_kernel(gr_ref, gi_ref, m_ref, c64_ref, wf_ref, o_ref):
    w = FNET_WIDTH
    for i in range(FOURIER_GROUP):
        y = (jnp.dot(m_ref[i, :, :TILE], gr_ref[0, i], preferred_element_type=F32)
             + jnp.dot(m_ref[i, :, TILE:], gi_ref[0, i], preferred_element_type=F32))
        o_ref[0, :, i * w:(i + 1) * w] = _fourier_tail(y[:TILE], y[TILE:], c64_ref, wf_ref)


def _fourier_inner(layer, g, m_tab, c64_tab, wf_blk, r):
    b, two_l2n, flat = g.shape
    l2n = two_l2n // 2
    w = flat // TILE
    steps = l2n // FOURIER_GROUP
    g4 = g.reshape(b, two_l2n, TILE, w)
    out = pl.pallas_call(
        _fourier_inner_kernel,
        grid=(b, steps),
        in_specs=[
            pl.BlockSpec((1, FOURIER_GROUP, TILE, w), lambda i, j: (i, j, 0, 0)),
            pl.BlockSpec((1, FOURIER_GROUP, TILE, w), lambda i, j: (i, steps + j, 0, 0)),
            pl.BlockSpec((FOURIER_GROUP, 2 * TILE, 2 * TILE), lambda i, j: (j, 0, 0)),
            _const_spec((2 * w, w)),
            _const_spec((w, w), layer),
        ],
        out_specs=pl.BlockSpec((1, TILE, FOURIER_GROUP * w), lambda i, j: (i, 0, j)),
        out_shape=jax.ShapeDtypeStruct((b, r // l2n, l2n * w), BF16),
        compiler_params=_params(("parallel", "parallel")),
        name="fourier_inner",
    )(g4, g4, m_tab, c64_tab, wf_blk)
    return out.reshape(b, r, w)


def _fourier_ctx_kernel(x_ref, m_ref, c64_ref, wf_ref, fn_ref, o_ref):
    del fn_ref
    y = jnp.dot(m_ref[...], x_ref[0].astype(BF16), preferred_element_type=F32)
    o_ref[0] = _fourier_tail(y[:TILE], y[TILE:], c64_ref, wf_ref)


def _fourier_context(layer, f, fn, m_ctx, c64_tab, wf_blk):
    b, r, w = f.shape
    nt = r // TILE
    return pl.pallas_call(
        _fourier_ctx_kernel,
        grid=(b,),
        in_specs=[
            pl.BlockSpec((1, TILE, w), lambda i: (i, nt - 1, 0)),
            _const_spec((2 * TILE, TILE)),
            _const_spec((2 * w, w)),
            _const_spec((w, w), layer),
            pl.BlockSpec(memory_space=pl.ANY),
        ],
        out_specs=pl.BlockSpec((1, TILE, w), lambda i: (i, nt - 1, 0)),
        out_shape=jax.ShapeDtypeStruct((b, r, w), BF16),
        input_output_aliases={4: 0},
        compiler_params=_params(("parallel",)),
        name="fourier_context",
    )(f, m_ctx, c64_tab, wf_blk, fn)


def _outffn_kernel(x_ref, a_ref, s_ref, fn_ref, mod_ref, wo_ref, g2_ref, w1_ref, w2_ref, o_ref):
    d = D_MODEL
    x = x_ref[0]
    mod = mod_ref[0, 0]
    o1 = ATTN_WIDTH
    o2 = ATTN_WIDTH + SSM_WIDTH
    mix = (jnp.dot(a_ref[0], wo_ref[0, :o1], preferred_element_type=F32)
           + jnp.dot(s_ref[0], wo_ref[0, o1:o2], preferred_element_type=F32)
           + jnp.dot(fn_ref[0], wo_ref[0, o2:], preferred_element_type=F32))
    x1 = x + mod[:, 2 * d:3 * d] * mix
    ms = jnp.mean(x1 * x1, axis=-1, keepdims=True)
    hn = x1 * lax.rsqrt(ms + EPS) * g2_ref[0]
    h = (hn * (1.0 + mod[:, 4 * d:5 * d]) + mod[:, 3 * d:4 * d]).astype(BF16)
    acc = jnp.zeros((TILE, d), F32)
    for cc in range(D_FF // FF_CHUNK):
        sl = slice(cc * FF_CHUNK, (cc + 1) * FF_CHUNK)
        t = jnp.dot(h, w1_ref[0, :, sl], preferred_element_type=F32)
        t = jnp.square(jnp.maximum(t, 0.0))
        acc = acc + jnp.dot(t.astype(BF16), w2_ref[0, sl, :], preferred_element_type=F32)
    o_ref[0] = x1 + mod[:, 5 * d:] * acc


def _out_ffn(layer, xs, attn, s5, fn, mods, w_out, g_norm2, w_ff1, w_ff2):
    b, r, d = xs.shape
    nt = r // TILE
    depth = w_out.shape[0]

    def tile(width):
        return pl.BlockSpec((1, TILE, width), lambda i, t: (i, t, 0))

    return pl.pallas_call(
        _outffn_kernel,
        grid=(b, nt),
        in_specs=[
            tile(d), tile(ATTN_WIDTH), tile(SSM_WIDTH), tile(FNET_WIDTH),
            pl.BlockSpec((1, 1, 1, 6 * d),
                         lambda i, t: (layer, jnp.where(t == nt - 1, b, i), 0, 0)),
            _const_spec((d, d), layer),
            _const_spec((1, d), layer),
            _const_spec((d, D_FF), layer),
            _const_spec((D_FF, d), layer),
        ],
        out_specs=tile(d),
        out_shape=jax.ShapeDtypeStruct((b, r, d), F32),
        input_output_aliases={0: 0},
        compiler_params=_params(("parallel", "parallel")),
        name="out_ffn",
    )(xs, attn, s5, fn, mods.reshape(depth, SUBLANES, 1, 6 * d), w_out,
      g_norm2.reshape(depth, 1, d), w_ff1, w_ff2)


def _head_lane_tables():
    j = np.arange(LANES)
    half = j // 64
    mp = (j // 32) % 2
    idx = j % 32
    src_in_head = mp * HEAD_DIM + half * 32 + idx
    gain_idx = half * 32 + idx
    return src_in_head, gain_idx, mp


def _in_proj_column_order():
    src_in_head, _, _ = _head_lane_tables()
    qk = np.concatenate([h * LANES + src_in_head for h in range(N_HEADS)])
    return np.concatenate([qk, QK_WIDTH + qk, np.arange(2 * QK_WIDTH, IN_WIDTH)])


def _same_map_matrix():
    i = np.arange(256)
    head = i // LANES
    mp = ((i % LANES) // 32) % 2
    same = (head[:, None] == head[None, :]) & (mp[:, None] == mp[None, :])
    return same.astype(np.float32)


def _rope_tables(seq, ctx_len):
    t = jnp.arange(seq)
    row = (t // GRID_W).astype(F32)
    col = (t % GRID_W).astype(F32)
    n_freq = HEAD_DIM // 4
    inv = jnp.power(ROPE_BASE, -jnp.arange(n_freq, dtype=F32) / n_freq)
    ang = jnp.concatenate([row[:, None] * inv, col[:, None] * inv], axis=-1)
    cos = jnp.tile(jnp.cos(ang), (1, 4))
    sign = np.where(np.arange(LANES) < LANES // 2, -1.0, 1.0).astype(np.float32)
    sin = jnp.tile(jnp.sin(ang), (1, 4)) * sign
    cos = jnp.concatenate([cos, jnp.ones((ctx_len, LANES), F32)], axis=0)
    sin = jnp.concatenate([sin, jnp.zeros((ctx_len, LANES), F32)], axis=0)
    return cos, sin


def _dft_tables(length):
    l2n = length // TILE
    k2 = np.arange(l2n)
    phi = 2.0 * np.pi * ((k2[:, None] * k2[None, :]) % l2n) / l2n
    w_outer = np.concatenate([np.cos(phi), -np.sin(phi)], axis=0).astype(np.float32)
    l1 = np.arange(TILE)
    alpha = 2.0 * np.pi * ((k2[:, None] * l1[None, :]) % length) / length
    beta = 2.0 * np.pi * ((l1[:, None] * l1[None, :]) % TILE) / TILE
    ca = jnp.asarray(np.cos(alpha).astype(np.float32))[:, None, :]
    sa = jnp.asarray(np.sin(alpha).astype(np.float32))[:, None, :]
    cb = jnp.asarray(np.cos(beta).astype(np.float32))[None]
    sb = jnp.asarray(np.sin(beta).astype(np.float32))[None]
    norm = 1.0 / math.sqrt(length)
    ct = (ca * cb - sa * sb) * norm
    st = (sa * cb + ca * sb) * norm
    m = jnp.concatenate([jnp.concatenate([ct, st], axis=2),
                         jnp.concatenate([-st, ct], axis=2)], axis=1).astype(BF16)
    return jnp.asarray(w_outer).astype(BF16), m


def _tile_dft_table():
    l1 = np.arange(TILE)
    beta = 2.0 * np.pi * ((l1[:, None] * l1[None, :]) % TILE) / TILE
    tab = np.concatenate([np.cos(beta), -np.sin(beta)], axis=0) / math.sqrt(TILE)
    return jnp.asarray(tab.astype(np.float32)).astype(BF16)


def _channel_dft_table():
    c = np.arange(FNET_C)
    th = 2.0 * np.pi * ((c[:, None] * c[None, :]) % FNET_C) / FNET_C
    eye = np.eye(FNET_G)
    norm = 1.0 / math.sqrt(FNET_C)
    cblk = np.kron(eye, np.cos(th)) * norm
    sblk = np.kron(eye, np.sin(th)) * norm
    return jnp.asarray(np.concatenate([cblk, sblk], axis=0).astype(np.float32)).astype(BF16)


def _block_diag(blocks):
    g = blocks.shape[-3]
    eye = jnp.eye(g, dtype=blocks.dtype)
    out = jnp.einsum('...gab,gh->...gahb', blocks, eye)
    return out.reshape(blocks.shape[:-3] + (g * blocks.shape[-2], g * blocks.shape[-1]))


def kernel(x, c, ctx, c_ctx, w_mod, b_mod, g_norm1, w_in, g_qnorm, g_knorm, lam_q1, lam_k1, lam_q2, lam_k2, g_subln, ssm_a_re, ssm_a_im, ssm_log_dt, ssm_b_re, ssm_b_im, ssm_c_re, ssm_c_im, ssm_d, w_glu, b_glu, w_fnet, w_out, g_norm2, w_ff1, w_ff2):
    bsz, seq, d = x.shape
    ctx_len = ctx.shape[1]
    depth = w_mod.shape[0]
    assert d == D_MODEL and ctx_len == TILE and seq % TILE == 0 and seq % GRID_W == 0
    assert bsz + 1 <= SUBLANES

    r_tot = seq + ctx_len
    assert (seq // TILE) % FOURIER_GROUP == 0
    xs = jnp.concatenate([x, ctx], axis=1)
    act = jnp.concatenate([c, c_ctx[None], jnp.zeros((SUBLANES - bsz - 1, d), F32)], axis=0)
    mods = _modulation(act, w_mod, b_mod)

    w_in_p = jnp.take(w_in, jnp.asarray(_in_proj_column_order()), axis=2).astype(BF16)
    _, gain_idx, _ = _head_lane_tables()
    gq = jnp.tile(g_qnorm[:, gain_idx], (1, N_HEADS)).reshape(depth, 1, QK_WIDTH)
    gk = jnp.tile(g_knorm[:, gain_idx], (1, N_HEADS)).reshape(depth, 1, QK_WIDTH)
    e_mat = jnp.asarray(_same_map_matrix()).astype(BF16)
    cos_t, sin_t = _rope_tables(seq, ctx_len)
    lam_init = np.array([0.8 - 0.6 * math.exp(-0.3 * i) for i in range(depth)], np.float32)
    pad = jnp.zeros((depth, LANES - HEAD_DIM), F32)
    lam_rows = [jnp.concatenate([v, pad], axis=1) for v in (lam_q1, lam_k1, lam_q2, lam_k2)]
    const_row = np.zeros((depth, LANES), np.float32)
    const_row[:, 0] = lam_init
    const_row[:, 1] = 1.0 - lam_init
    lam_p = jnp.stack(lam_rows + [jnp.asarray(const_row)]
                      + [jnp.zeros((depth, LANES), F32)] * (SUBLANES - 5), axis=1)
    g_sub = g_subln.reshape(depth, 1, V_DIM)

    a_re = ssm_a_re.reshape(depth, 2, 1, SSM_STATE)
    a_im = ssm_a_im.reshape(depth, 2, 1, SSM_STATE)
    ldt = jnp.repeat(ssm_log_dt, SSM_N, axis=-1).reshape(depth, 2, 1, SSM_STATE)
    b_re = _block_diag(jnp.swapaxes(ssm_b_re, -1, -2))
    b_im = _block_diag(jnp.swapaxes(ssm_b_im, -1, -2))
    c_stack = jnp.concatenate([_block_diag(jnp.swapaxes(ssm_c_re, -1, -2)),
                               -_block_diag(jnp.swapaxes(ssm_c_im, -1, -2))],
                              axis=2).astype(BF16)
    d_skip = ssm_d.reshape(depth, 1, SSM_WIDTH)
    w_glu_b = w_glu.astype(BF16)
    b_glu_r = b_glu.reshape(depth, 1, SSM_WIDTH)

    w_outer, m_lat = _dft_tables(seq)
    m_ctx = _tile_dft_table()
    c64_tab = _channel_dft_table()
    wf_blk = _block_diag(w_fnet).astype(BF16)

    w_out_b = w_out.astype(BF16)
    w_ff1_b = w_ff1.astype(BF16)
    w_ff2_b = w_ff2.astype(BF16)

    for layer in range(depth):
        q, kt, ve, u, f = _in_projection(layer, xs, mods, g_norm1, w_in_p, gq, gk, e_mat,
                                         cos_t, sin_t)
        attn = _attention(layer, q, kt, ve, lam_p, g_sub)
        y_fwd = _s5_pass(layer, 0, u, a_re, a_im, ldt, b_re, b_im, c_stack)
        s5 = _s5_pass(layer, 1, u, a_re, a_im, ldt, b_re, b_im, c_stack,
                      final_args=(y_fwd, d_skip, w_glu_b, b_glu_r))
        fn = _fourier_inner(layer, _fourier_outer(f, w_outer), m_lat, c64_tab, wf_blk, r_tot)
        fn = _fourier_context(layer, f, fn, m_ctx, c64_tab, wf_blk)
        xs = _out_ffn(layer, xs, attn, s5, fn, mods, w_out_b, g_norm2, w_ff1_b, w_ff2_b)
    return xs[:, :seq]
```

```python
import functools
import math

import numpy as np
import jax
import jax.numpy as jnp
from jax import lax
from jax.experimental import pallas as pl
from jax.experimental.pallas import tpu as pltpu

F32 = jnp.float32
BF16 = jnp.bfloat16

D_MODEL = 1024
GRID_W = 64
N_HEADS = 4
HEAD_DIM = 64
V_DIM = 2 * HEAD_DIM
QK_WIDTH = N_HEADS * 2 * HEAD_DIM
ATTN_WIDTH = N_HEADS * V_DIM
SSM_WIDTH = D_MODEL // 4
SSM_P = 16
SSM_G = SSM_WIDTH // SSM_P
SSM_N = 64
SSM_STATE = SSM_G * SSM_N
FNET_WIDTH = D_MODEL // 4
FNET_G = 4
FNET_C = FNET_WIDTH // FNET_G
IN_WIDTH = 2 * QK_WIDTH + ATTN_WIDTH + SSM_WIDTH + FNET_WIDTH
D_FF = 4 * D_MODEL
ROPE_BASE = 10000.0
EPS = 1e-6
SCALE = HEAD_DIM ** -0.5
LOG2E = math.log2(math.e)

TILE = 256
LANES = 128
SUBLANES = 8
FF_CHUNK = 1024
FOURIER_CHUNK = 8192
FOURIER_GROUP = 4
S5_BLOCK = 8
S5_CHUNK_ROWS = 256
VMEM_LIMIT = 48 * 1024 * 1024
S5_VMEM_LIMIT = 56 * 1024 * 1024


def _const_spec(shape, layer=None):
    nd = len(shape)
    if layer is None:
        return pl.BlockSpec(shape, lambda *_: (0,) * nd, pipeline_mode=pl.Buffered(1))
    return pl.BlockSpec((1,) + tuple(shape), lambda *_: (layer,) + (0,) * nd,
                        pipeline_mode=pl.Buffered(1))


def _params(sem, vmem=None):
    return pltpu.CompilerParams(dimension_semantics=sem, vmem_limit_bytes=vmem or VMEM_LIMIT)


def _mod_kernel(act_ref, w_ref, b_ref, o_ref):
    a = act_ref[...]
    a = a * jax.nn.sigmoid(a)
    o_ref[0] = jnp.dot(a.astype(BF16), w_ref[0].astype(BF16),
                       preferred_element_type=F32) + b_ref[0]


def _modulation(act, w_mod, b_mod):
    depth, d, n = w_mod.shape
    bn = 512
    return pl.pallas_call(
        _mod_kernel,
        grid=(depth, n // bn),
        in_specs=[pl.BlockSpec((SUBLANES, d), lambda l, j: (0, 0)),
                  pl.BlockSpec((1, d, bn), lambda l, j: (l, 0, j)),
                  pl.BlockSpec((1, 1, bn), lambda l, j: (l, 0, j))],
        out_specs=pl.BlockSpec((1, SUBLANES, bn), lambda l, j: (l, 0, j)),
        out_shape=jax.ShapeDtypeStruct((depth, SUBLANES, n), F32),
        compiler_params=_params(("parallel", "parallel")),
        name="modulation",
    )(act, w_mod, b_mod.reshape(depth, 1, n))


def _inproj_kernel(x_ref, mod_ref, g1_ref, w_ref, gq_ref, gk_ref, e_ref, cos_ref, sin_ref,
                   q_ref, kt_ref, ve_ref, u_ref, f_ref):
    d = D_MODEL
    x = x_ref[0]
    ms = jnp.mean(x * x, axis=-1, keepdims=True)
    xn = x * lax.rsqrt(ms + EPS) * g1_ref[0]
    mod = mod_ref[0, 0]
    h = xn * (1.0 + mod[:, d:2 * d]) + mod[:, :d]
    proj = jnp.dot(h.astype(BF16), w_ref[0], preferred_element_type=F32)
    e = e_ref[...]
    cos = cos_ref[...]
    sin = sin_ref[...]

    def qk_norm(z, g):
        sq = z * z
        hi = sq.astype(BF16)
        lo = (sq - hi.astype(F32)).astype(BF16)
        parts = []
        for j in range(QK_WIDTH // 256):
            sl = slice(j * 256, (j + 1) * 256)
            parts.append(jnp.dot(hi[:, sl], e, preferred_element_type=F32)
                         + jnp.dot(lo[:, sl], e, preferred_element_type=F32))
        ssum = jnp.concatenate(parts, axis=-1)
        return z * lax.rsqrt(ssum * (1.0 / HEAD_DIM) + EPS) * g

    def rope(zh):
        return zh * cos + pltpu.roll(zh, LANES // 2, 1) * sin

    qn = qk_norm(proj[:, :QK_WIDTH], gq_ref[0])
    kn = qk_norm(proj[:, QK_WIDTH:2 * QK_WIDTH], gk_ref[0])
    row_map = (lax.broadcasted_iota(jnp.int32, (LANES, TILE), 0) // 32) % 2
    ones = jnp.ones((TILE, LANES), BF16)
    for hh in range(N_HEADS):
        sl = slice(hh * LANES, (hh + 1) * LANES)
        q_ref[0, :, sl] = (rope(qn[:, sl]) * (SCALE * LOG2E)).astype(BF16)
        kt = rope(kn[:, sl]).T
        kt_ref[0, hh, 0] = jnp.where(row_map == 0, kt, 0.0).astype(BF16)
        kt_ref[0, hh, 1] = jnp.where(row_map == 1, kt, 0.0).astype(BF16)
        vo = 2 * QK_WIDTH + hh * V_DIM
        ve_ref[0, hh, :, :V_DIM] = proj[:, vo:vo + V_DIM].astype(BF16)
        ve_ref[0, hh, :, V_DIM:] = ones
    uo = 2 * QK_WIDTH + ATTN_WIDTH
    u_ref[0] = proj[:, uo:uo + SSM_WIDTH]
    f_ref[0] = proj[:, uo + SSM_WIDTH:]


def _in_projection(layer, xs, mods, g_norm1, w_in, gq, gk, e_mat, cos_t, sin_t):
    b, r, d = xs.shape
    nt = r // TILE
    depth = w_in.shape[0]
    return pl.pallas_call(
        _inproj_kernel,
        grid=(b, nt),
        in_specs=[
            pl.BlockSpec((1, TILE, d), lambda i, t: (i, t, 0)),
            pl.BlockSpec((1, 1, 1, 6 * d),
                         lambda i, t: (layer, jnp.where(t == nt - 1, b, i), 0, 0)),
            _const_spec((1, d), layer),
            _const_spec((d, IN_WIDTH), layer),
            _const_spec((1, QK_WIDTH), layer),
            _const_spec((1, QK_WIDTH), layer),
            _const_spec((256, 256)),
            pl.BlockSpec((TILE, LANES), lambda i, t: (t, 0)),
            pl.BlockSpec((TILE, LANES), lambda i, t: (t, 0)),
        ],
        out_specs=[
            pl.BlockSpec((1, TILE, QK_WIDTH), lambda i, t: (i, t, 0)),
            pl.BlockSpec((1, N_HEADS, 2, LANES, TILE), lambda i, t: (i, 0, 0, 0, t)),
            pl.BlockSpec((1, N_HEADS, TILE, 2 * V_DIM), lambda i, t: (i, 0, t, 0)),
            pl.BlockSpec((1, TILE, SSM_WIDTH), lambda i, t: (i, t, 0)),
            pl.BlockSpec((1, TILE, FNET_WIDTH), lambda i, t: (i, t, 0)),
        ],
        out_shape=[
            jax.ShapeDtypeStruct((b, r, QK_WIDTH), BF16),
            jax.ShapeDtypeStruct((b, N_HEADS, 2, LANES, r), BF16),
            jax.ShapeDtypeStruct((b, N_HEADS, r, 2 * V_DIM), BF16),
            jax.ShapeDtypeStruct((b, r, SSM_WIDTH), F32),
            jax.ShapeDtypeStruct((b, r, FNET_WIDTH), F32),
        ],
        compiler_params=_params(("parallel", "parallel")),
        name="in_projection",
    )(xs, mods.reshape(depth, SUBLANES, 1, 6 * d), g_norm1.reshape(depth, 1, d), w_in,
      gq, gk, e_mat, cos_t, sin_t)


def _attn_kernel(q_ref, kt_ref, ve_ref, lam_ref, gs_ref, o_ref, *, nt, ctx_len):
    t = pl.program_id(2)
    q = q_ref[0]
    r = ve_ref.shape[2]

    def attend(k0):
        outs = []
        ve = ve_ref[0, 0, k0:, :]
        scores = [jnp.dot(q, kt_ref[0, 0, mp, :, k0:], preferred_element_type=F32)
                  for mp in range(2)]
        for s in scores:
            p = jnp.exp2(s - jnp.max(s, axis=1, keepdims=True))
            acc = jnp.dot(p.astype(BF16), ve, preferred_element_type=F32)
            outs.append(acc[:, :V_DIM] / acc[:, V_DIM:])
        lp = lam_ref[0]
        s1 = jnp.sum(lp[0:1] * lp[1:2], axis=-1, keepdims=True)
        s2 = jnp.sum(lp[2:3] * lp[3:4], axis=-1, keepdims=True)
        lam = jnp.exp(s1) - jnp.exp(s2) + lp[4:5, 0:1]
        a = outs[0] - lam * outs[1]
        a = a * lax.rsqrt(jnp.mean(a * a, axis=-1, keepdims=True) + EPS)
        o_ref[0] = (a * gs_ref[0] * lp[4:5, 1:2]).astype(BF16)

    @pl.when(t < nt - 1)
    def _latent_queries():
        attend(0)

    @pl.when(t == nt - 1)
    def _context_queries():
        attend(r - ctx_len)


def _attention(layer, q, kt, ve, lam_p, g_subln):
    b, r, _ = q.shape
    nt = r // TILE
    return pl.pallas_call(
        functools.partial(_attn_kernel, nt=nt, ctx_len=TILE),
        grid=(b, N_HEADS, nt),
        in_specs=[
            pl.BlockSpec((1, TILE, LANES), lambda i, h, t: (i, t, h)),
            pl.BlockSpec((1, 1, 2, LANES, r), lambda i, h, t: (i, h, 0, 0, 0),
                         pipeline_mode=pl.Buffered(1)),
            pl.BlockSpec((1, 1, r, 2 * V_DIM), lambda i, h, t: (i, h, 0, 0),
                         pipeline_mode=pl.Buffered(1)),
            _const_spec((SUBLANES, LANES), layer),
            _const_spec((1, V_DIM), layer),
        ],
        out_specs=pl.BlockSpec((1, TILE, V_DIM), lambda i, h, t: (i, t, h)),
        out_shape=jax.ShapeDtypeStruct((b, r, ATTN_WIDTH), BF16),
        compiler_params=_params(("parallel", "parallel", "arbitrary")),
        name="diff_attention",
    )(q, kt, ve, lam_p, g_subln)


def _zoh(are_ref, aim_ref, ldt_ref):
    a_re = are_ref[0, 0]
    a_im = aim_ref[0, 0]
    dt = jnp.exp(ldt_ref[0, 0])
    mag = jnp.exp(dt * a_re)
    ang = dt * a_im
    ab_re = mag * jnp.cos(ang)
    ab_im = mag * jnp.sin(ang)
    den = a_re * a_re + a_im * a_im
    n_re = ab_re - 1.0
    f_re = (n_re * a_re + ab_im * a_im) / den
    f_im = (ab_im * a_re - n_re * a_im) / den
    return ab_re, ab_im, f_re, f_im


def _complex_powers(base_re, base_im, n):
    pows = [(jnp.ones_like(base_re), jnp.zeros_like(base_im))]
    for _ in range(n):
        pr, pi = pows[-1]
        pows.append((pr * base_re - pi * base_im, pr * base_im + pi * base_re))
    return pows


def _fill_scan_tables(pw_ref, base_re, base_im, reverse):
    shape = (SUBLANES, SSM_STATE)
    pows = _complex_powers(base_re, base_im, SUBLANES)
    row = lax.broadcasted_iota(jnp.int32, shape, 0)
    zero = jnp.zeros(shape, F32)
    for idx, k in enumerate((1, 2, 4)):
        mask = (row + k <= SUBLANES - 1) if reverse else (row >= k)
        pw_ref[2 * idx] = jnp.where(mask, jnp.broadcast_to(pows[k][0], shape), zero)
        pw_ref[2 * idx + 1] = jnp.where(mask, jnp.broadcast_to(pows[k][1], shape), zero)
    pcr = zero
    pci = zero
    for tt in range(SUBLANES):
        e = (SUBLANES - tt) if reverse else tt + 1
        pcr = jnp.where(row == tt, jnp.broadcast_to(pows[e][0], shape), pcr)
        pci = jnp.where(row == tt, jnp.broadcast_to(pows[e][1], shape), pci)
    pw_ref[6] = pcr
    pw_ref[7] = pci


def _scan_rows(s_ref, pw_ref, carry, nrows, reverse, exclusive):
    ns = SSM_STATE
    shape = (SUBLANES, ns)
    ngroups = nrows // SUBLANES
    edge = (SUBLANES - 1) if reverse else 0

    def group(g, carry):
        cr, ci = carry
        gi = (ngroups - 1 - g) if reverse else g
        r0 = pl.multiple_of(gi * SUBLANES, SUBLANES)
        xr = s_ref[pl.ds(r0, SUBLANES), :ns]
        xi = s_ref[pl.ds(r0, SUBLANES), ns:]
        for idx, k in enumerate((1, 2, 4)):
            sh = (SUBLANES - k) if reverse else k
            sr = pltpu.roll(xr, sh, 0)
            si = pltpu.roll(xi, sh, 0)
            pr = pw_ref[2 * idx]
            pi = pw_ref[2 * idx + 1]
            xr, xi = xr + pr * sr - pi * si, xi + pr * si + pi * sr
        pr = pw_ref[6]
        pi = pw_ref[7]
        xr, xi = xr + pr * cr - pi * ci, xi + pr * ci + pi * cr
        if exclusive:
            row = lax.broadcasted_iota(jnp.int32, shape, 0)
            sh = (SUBLANES - 1) if reverse else 1
            er = jnp.where(row == edge, cr, pltpu.roll(xr, sh, 0))
            ei = jnp.where(row == edge, ci, pltpu.roll(xi, sh, 0))
        else:
            er, ei = xr, xi
        s_ref[pl.ds(r0, SUBLANES), :ns] = er
        s_ref[pl.ds(r0, SUBLANES), ns:] = ei
        last = 0 if reverse else SUBLANES - 1
        return (jnp.broadcast_to(xr[last:last + 1], shape),
                jnp.broadcast_to(xi[last:last + 1], shape))

    return lax.fori_loop(0, ngroups, group, carry)


def _s5_ctx_kernel(u_ref, are_ref, aim_ref, ldt_ref, bre_ref, bim_ref, c_ref,
                   o_ref, st_ref, bbar_ref, pw_ref, s_ref, *, reverse):
    ns = SSM_STATE
    ab_re, ab_im, f_re, f_im = _zoh(are_ref, aim_ref, ldt_ref)
    bre = bre_ref[0, 0]
    bim = bim_ref[0, 0]
    bbar_ref[:, :ns] = (f_re * bre - f_im * bim).astype(BF16)
    bbar_ref[:, ns:] = (f_re * bim + f_im * bre).astype(BF16)
    _fill_scan_tables(pw_ref, ab_re, ab_im, reverse)
    s_ref[...] = jnp.dot(u_ref[0].astype(BF16), bbar_ref[...], preferred_element_type=F32)
    zero = jnp.zeros((SUBLANES, ns), F32)
    cr, ci = _scan_rows(s_ref, pw_ref, (zero, zero), TILE, reverse, exclusive=False)
    st_ref[0, 0] = cr
    st_ref[0, 1] = ci
    o_ref[0] = jnp.dot(s_ref[...].astype(BF16), c_ref[0, 0], preferred_element_type=F32)


def _s5_block_kernel(u8_ref, s0_ref, are_ref, aim_ref, ldt_ref, bre_ref, bim_ref, ctre_ref,
                     ctim_ref, y_ref, o_ref, wx_ref, wc_ref, wt_ref, pw_ref, carry_ref, s_ref,
                     *, reverse):
    del y_ref
    ns = SSM_STATE
    w = SSM_WIDTH
    nb = S5_BLOCK
    c = pl.program_id(0)
    b = pl.program_id(1)

    @pl.when((c == 0) & (b == 0))
    def _build_maps():
        ab_re, ab_im, f_re, f_im = _zoh(are_ref, aim_ref, ldt_ref)
        bre = bre_ref[0, 0]
        bim = bim_ref[0, 0]
        bb_re = f_re * bre - f_im * bim
        bb_im = f_re * bim + f_im * bre
        pows = _complex_powers(ab_re, ab_im, nb)
        ctre = ctre_ref[0, 0]
        ctim = ctim_ref[0, 0]
        ct_stack = jnp.concatenate([ctre, -ctim], axis=1).astype(BF16)
        taps = []
        for e in range(nb):
            pr, pi = pows[e]
            xr = pr * bb_re - pi * bb_im
            xi = pr * bb_im + pi * bb_re
            i = e if reverse else nb - 1 - e
            wx_ref[i * w:(i + 1) * w, :ns] = xr.astype(BF16)
            wx_ref[i * w:(i + 1) * w, ns:] = xi.astype(BF16)
            xk = jnp.concatenate([xr, xi], axis=1).astype(BF16)
            taps.append(lax.dot_general(xk, ct_stack, (((1,), (1,)), ((), ())),
                                        preferred_element_type=F32).astype(BF16))
        zero_blk = jnp.zeros((w, w), BF16)
        for i in range(nb):
            for j in range(nb):
                lag = (i - j) if reverse else (j - i)
                wt_ref[i * w:(i + 1) * w, j * w:(j + 1) * w] = taps[lag] if lag >= 0 else zero_blk
        for j in range(nb):
            pr, pi = pows[nb - j] if reverse else pows[j + 1]
            wc_ref[:ns, j * w:(j + 1) * w] = (ctre * pr - ctim * pi).T.astype(BF16)
            wc_ref[ns:, j * w:(j + 1) * w] = (-(ctre * pi + ctim * pr)).T.astype(BF16)
        _fill_scan_tables(pw_ref, pows[nb][0], pows[nb][1], reverse)

    @pl.when(c == 0)
    def _load_state():
        carry_ref[b] = s0_ref[0]

    u8 = u8_ref[0].astype(BF16)
    s_ref[...] = jnp.dot(u8, wx_ref[...], preferred_element_type=F32)
    cr, ci = _scan_rows(s_ref, pw_ref, (carry_ref[b, 0], carry_ref[b, 1]), S5_CHUNK_ROWS,
                        reverse, exclusive=True)
    carry_ref[b, 0] = cr
    carry_ref[b, 1] = ci
    o_ref[0] = (jnp.dot(s_ref[...].astype(BF16), wc_ref[...], preferred_element_type=F32)
                + jnp.dot(u8, wt_ref[...], preferred_element_type=F32))


def _s5_direction(layer, direction, u, a_re, a_im, ldt, b_re, b_im, ct_re, ct_im, c_stack):
    b, r, w = u.shape
    nt = r // TILE
    ns = SSM_STATE
    reverse = direction == 1

    def dir_spec(shape):
        nd = len(shape)
        return pl.BlockSpec((1, 1) + tuple(shape), lambda *_: (layer, direction) + (0,) * nd,
                            pipeline_mode=pl.Buffered(1))

    y_ctx, state = pl.pallas_call(
        functools.partial(_s5_ctx_kernel, reverse=reverse),
        grid=(b,),
        in_specs=[pl.BlockSpec((1, TILE, w), lambda i: (i, nt - 1, 0)),
                  dir_spec((1, ns)), dir_spec((1, ns)), dir_spec((1, ns)),
                  dir_spec((w, ns)), dir_spec((w, ns)), dir_spec((2 * ns, w))],
        out_specs=[pl.BlockSpec((1, TILE, w), lambda i: (i, nt - 1, 0)),
                   pl.BlockSpec((1, 2, SUBLANES, ns), lambda i: (i, 0, 0, 0))],
        out_shape=[jax.ShapeDtypeStruct((b, r, w), F32),
                   jax.ShapeDtypeStruct((b, 2, SUBLANES, ns), F32)],
        scratch_shapes=[pltpu.VMEM((w, 2 * ns), BF16),
                        pltpu.VMEM((8, SUBLANES, ns), F32),
                        pltpu.VMEM((TILE, 2 * ns), F32)],
        compiler_params=_params(("parallel",)),
        name="s5_context",
    )(u, a_re, a_im, ldt, b_re, b_im, c_stack)

    rows = r // S5_BLOCK
    nchunks = (r - TILE) // (S5_BLOCK * S5_CHUNK_ROWS)
    wide = S5_BLOCK * w

    def chunk(c, i):
        return (i, nchunks - 1 - c if reverse else c, 0)

    y = pl.pallas_call(
        functools.partial(_s5_block_kernel, reverse=reverse),
        grid=(nchunks, b),
        in_specs=[pl.BlockSpec((1, S5_CHUNK_ROWS, wide), chunk),
                  pl.BlockSpec((1, 2, SUBLANES, ns), lambda c, i: (i, 0, 0, 0)),
                  dir_spec((1, ns)), dir_spec((1, ns)), dir_spec((1, ns)),
                  dir_spec((w, ns)), dir_spec((w, ns)), dir_spec((w, ns)), dir_spec((w, ns)),
                  pl.BlockSpec(memory_space=pl.ANY)],
        out_specs=pl.BlockSpec((1, S5_CHUNK_ROWS, wide), chunk),
        out_shape=jax.ShapeDtypeStruct((b, rows, wide), F32),
        scratch_shapes=[pltpu.VMEM((wide, 2 * ns), BF16),
                        pltpu.VMEM((2 * ns, wide), BF16),
                        pltpu.VMEM((wide, wide), BF16),
                        pltpu.VMEM((8, SUBLANES, ns), F32),
                        pltpu.VMEM((b, 2, SUBLANES, ns), F32),
                        pltpu.VMEM((S5_CHUNK_ROWS, 2 * ns), F32)],
        input_output_aliases={9: 0},
        compiler_params=_params(("arbitrary", "arbitrary"), vmem=S5_VMEM_LIMIT),
        name="s5_blocks_reverse" if reverse else "s5_blocks_forward",
    )(u.reshape(b, rows, wide), state, a_re, a_im, ldt, b_re, b_im, ct_re, ct_im,
      y_ctx.reshape(b, rows, wide))
    return y.reshape(b, r, w)


def _fourier_outer_kernel(w_ref, x_ref, g_ref):
    g_ref[0] = jnp.dot(w_ref[...], x_ref[0].astype(BF16),
                       preferred_element_type=F32).astype(BF16)


def _fourier_outer(f, w_outer):
    b, r, w = f.shape
    nt = r // TILE
    l2n = nt - 1
    flat = TILE * w
    return pl.pallas_call(
        _fourier_outer_kernel,
        grid=(b, flat // FOURIER_CHUNK),
        in_specs=[_const_spec((2 * l2n, l2n)),
                  pl.BlockSpec((1, l2n, FOURIER_CHUNK), lambda i, j: (i, 0, j))],
        out_specs=pl.BlockSpec((1, 2 * l2n, FOURIER_CHUNK), lambda i, j: (i, 0, j)),
        out_shape=jax.ShapeDtypeStruct((b, 2 * l2n, flat), BF16),
        compiler_params=_params(("parallel", "parallel")),
        name="fourier_outer",
    )(w_outer, f.reshape(b, nt, flat))


def _fourier_tail(y_re, y_im, c64_ref, wf_ref):
    w = FNET_WIDTH
    z = (jnp.dot(y_re.astype(BF16), c64_ref[:w], preferred_element_type=F32)
         + jnp.dot(y_im.astype(BF16), c64_ref[w:], preferred_element_type=F32))
    return jnp.dot(z.astype(BF16), wf_ref[0], preferred_element_type=F32).astype(BF16)


def _fourier_inner_kernel(gr_ref, gi_ref, m_ref, c64_ref, wf_ref, o_ref):
    w = FNET_WIDTH
    for i in range(FOURIER_GROUP):
        y = (jnp.dot(m_ref[i, :, :TILE], gr_ref[0, i], preferred_element_type=F32)
             + jnp.dot(m_ref[i, :, TILE:], gi_ref[0, i], preferred_element_type=F32))
        o_ref[0, :, i * w:(i + 1) * w] = _fourier_tail(y[:TILE], y[TILE:], c64_ref, wf_ref)


def _fourier_inner(layer, g, m_tab, c64_tab, wf_blk, r):
    b, two_l2n, flat = g.shape
    l2n = two_l2n // 2
    w = flat // TILE
    steps = l2n // FOURIER_GROUP
    g4 = g.reshape(b, two_l2n, TILE, w)
    out = pl.pallas_call(
        _fourier_inner_kernel,
        grid=(b, steps),
        in_specs=[
            pl.BlockSpec((1, FOURIER_GROUP, TILE, w), lambda i, j: (i, j, 0, 0)),
            pl.BlockSpec((1, FOURIER_GROUP, TILE, w), lambda i, j: (i, steps + j, 0, 0)),
            pl.BlockSpec((FOURIER_GROUP, 2 * TILE, 2 * TILE), lambda i, j: (j, 0, 0)),
            _const_spec((2 * w, w)),
            _const_spec((w, w), layer),
        ],
        out_specs=pl.BlockSpec((1, TILE, FOURIER_GROUP * w), lambda i, j: (i, 0, j)),
        out_shape=jax.ShapeDtypeStruct((b, r // l2n, l2n * w), BF16),
        compiler_params=_params(("parallel", "parallel")),
        name="fourier_inner",
    )(g4, g4, m_tab, c64_tab, wf_blk)
    return out.reshape(b, r, w)


def _fourier_ctx_kernel(x_ref, m_ref, c64_ref, wf_ref, fn_ref, o_ref):
    del fn_ref
    y = jnp.dot(m_ref[...], x_ref[0].astype(BF16), preferred_element_type=F32)
    o_ref[0] = _fourier_tail(y[:TILE], y[TILE:], c64_ref, wf_ref)


def _fourier_context(layer, f, fn, m_ctx, c64_tab, wf_blk):
    b, r, w = f.shape
    nt = r // TILE
    return pl.pallas_call(
        _fourier_ctx_kernel,
        grid=(b,),
        in_specs=[
            pl.BlockSpec((1, TILE, w), lambda i: (i, nt - 1, 0)),
            _const_spec((2 * TILE, TILE)),
            _const_spec((2 * w, w)),
            _const_spec((w, w), layer),
            pl.BlockSpec(memory_space=pl.ANY),
        ],
        out_specs=pl.BlockSpec((1, TILE, w), lambda i: (i, nt - 1, 0)),
        out_shape=jax.ShapeDtypeStruct((b, r, w), BF16),
        input_output_aliases={4: 0},
        compiler_params=_params(("parallel",)),
        name="fourier_context",
    )(f, m_ctx, c64_tab, wf_blk, fn)


def _outffn_kernel(x_ref, a_ref, yf_ref, yb_ref, u_ref, fn_ref, mod_ref, dsk_ref, wglu_ref,
                   bglu_ref, wo_ref, g2_ref, w1_ref, w2_ref, o_ref):
    d = D_MODEL
    x = x_ref[0]
    mod = mod_ref[0, 0]
    o1 = ATTN_WIDTH
    o2 = ATTN_WIDTH + SSM_WIDTH
    hg = jax.nn.gelu(yf_ref[0] + yb_ref[0] + dsk_ref[0] * u_ref[0])
    zg = jnp.dot(hg.astype(BF16), wglu_ref[0], preferred_element_type=F32) + bglu_ref[0]
    s5 = (hg * jax.nn.sigmoid(zg)).astype(BF16)
    mix = (jnp.dot(a_ref[0], wo_ref[0, :o1], preferred_element_type=F32)
           + jnp.dot(s5, wo_ref[0, o1:o2], preferred_element_type=F32)
           + jnp.dot(fn_ref[0], wo_ref[0, o2:], preferred_element_type=F32))
    x1 = x + mod[:, 2 * d:3 * d] * mix
    ms = jnp.mean(x1 * x1, axis=-1, keepdims=True)
    hn = x1 * lax.rsqrt(ms + EPS) * g2_ref[0]
    h = (hn * (1.0 + mod[:, 4 * d:5 * d]) + mod[:, 3 * d:4 * d]).astype(BF16)
    acc = jnp.zeros((TILE, d), F32)
    for cc in range(D_FF // FF_CHUNK):
        sl = slice(cc * FF_CHUNK, (cc + 1) * FF_CHUNK)
        t = jnp.dot(h, w1_ref[0, :, sl], preferred_element_type=F32)
        t = jnp.square(jnp.maximum(t, 0.0))
        acc = acc + jnp.dot(t.astype(BF16), w2_ref[0, sl, :], preferred_element_type=F32)
    o_ref[0] = x1 + mod[:, 5 * d:] * acc


def _out_ffn(layer, xs, attn, y_fwd, y_bwd, u, fn, mods, d_skip, w_glu, b_glu, w_out, g_norm2,
             w_ff1, w_ff2):
    b, r, d = xs.shape
    nt = r // TILE
    depth = w_out.shape[0]

    def tile(width):
        return pl.BlockSpec((1, TILE, width), lambda i, t: (i, t, 0))

    return pl.pallas_call(
        _outffn_kernel,
        grid=(b, nt),
        in_specs=[
            tile(d), tile(ATTN_WIDTH), tile(SSM_WIDTH), tile(SSM_WIDTH), tile(SSM_WIDTH),
            tile(FNET_WIDTH),
            pl.BlockSpec((1, 1, 1, 6 * d),
                         lambda i, t: (layer, jnp.where(t == nt - 1, b, i), 0, 0)),
            _const_spec((1, SSM_WIDTH), layer),
            _const_spec((SSM_WIDTH, SSM_WIDTH), layer),
            _const_spec((1, SSM_WIDTH), layer),
            _const_spec((d, d), layer),
            _const_spec((1, d), layer),
            _const_spec((d, D_FF), layer),
            _const_spec((D_FF, d), layer),
        ],
        out_specs=tile(d),
        out_shape=jax.ShapeDtypeStruct((b, r, d), F32),
        input_output_aliases={0: 0},
        compiler_params=_params(("parallel", "parallel")),
        name="out_ffn",
    )(xs, attn, y_fwd, y_bwd, u, fn, mods.reshape(depth, SUBLANES, 1, 6 * d), d_skip, w_glu, b_glu,
      w_out, g_norm2.reshape(depth, 1, d), w_ff1, w_ff2)


def _head_lane_tables():
    j = np.arange(LANES)
    half = j // 64
    mp = (j // 32) % 2
    idx = j % 32
    src_in_head = mp * HEAD_DIM + half * 32 + idx
    gain_idx = half * 32 + idx
    return src_in_head, gain_idx, mp


def _in_proj_column_order():
    src_in_head, _, _ = _head_lane_tables()
    qk = np.concatenate([h * LANES + src_in_head for h in range(N_HEADS)])
    return np.concatenate([qk, QK_WIDTH + qk, np.arange(2 * QK_WIDTH, IN_WIDTH)])


def _same_map_matrix():
    i = np.arange(256)
    head = i // LANES
    mp = ((i % LANES) // 32) % 2
    same = (head[:, None] == head[None, :]) & (mp[:, None] == mp[None, :])
    return same.astype(np.float32)


def _rope_tables(seq, ctx_len):
    t = jnp.arange(seq)
    row = (t // GRID_W).astype(F32)
    col = (t % GRID_W).astype(F32)
    n_freq = HEAD_DIM // 4
    inv = jnp.power(ROPE_BASE, -jnp.arange(n_freq, dtype=F32) / n_freq)
    ang = jnp.concatenate([row[:, None] * inv, col[:, None] * inv], axis=-1)
    cos = jnp.tile(jnp.cos(ang), (1, 4))
    sign = np.where(np.arange(LANES) < LANES // 2, -1.0, 1.0).astype(np.float32)
    sin = jnp.tile(jnp.sin(ang), (1, 4)) * sign
    cos = jnp.concatenate([cos, jnp.ones((ctx_len, LANES), F32)], axis=0)
    sin = jnp.concatenate([sin, jnp.zeros((ctx_len, LANES), F32)], axis=0)
    return cos, sin


def _dft_tables(length):
    l2n = length // TILE
    k2 = np.arange(l2n)
    phi = 2.0 * np.pi * ((k2[:, None] * k2[None, :]) % l2n) / l2n
    w_outer = np.concatenate([np.cos(phi), -np.sin(phi)], axis=0).astype(np.float32)
    l1 = np.arange(TILE)
    alpha = 2.0 * np.pi * ((k2[:, None] * l1[None, :]) % length) / length
    beta = 2.0 * np.pi * ((l1[:, None] * l1[None, :]) % TILE) / TILE
    ca = jnp.asarray(np.cos(alpha).astype(np.float32))[:, None, :]
    sa = jnp.asarray(np.sin(alpha).astype(np.float32))[:, None, :]
    cb = jnp.asarray(np.cos(beta).astype(np.float32))[None]
    sb = jnp.asarray(np.sin(beta).astype(np.float32))[None]
    norm = 1.0 / math.sqrt(length)
    ct = (ca * cb - sa * sb) * norm
    st = (sa * cb + ca * sb) * norm
    m = jnp.concatenate([jnp.concatenate([ct, st], axis=2),
                         jnp.concatenate([-st, ct], axis=2)], axis=1).astype(BF16)
    return jnp.asarray(w_outer).astype(BF16), m


def _tile_dft_table():
    l1 = np.arange(TILE)
    beta = 2.0 * np.pi * ((l1[:, None] * l1[None, :]) % TILE) / TILE
    tab = np.concatenate([np.cos(beta), -np.sin(beta)], axis=0) / math.sqrt(TILE)
    return jnp.asarray(tab.astype(np.float32)).astype(BF16)


def _channel_dft_table():
    c = np.arange(FNET_C)
    th = 2.0 * np.pi * ((c[:, None] * c[None, :]) % FNET_C) / FNET_C
    eye = np.eye(FNET_G)
    norm = 1.0 / math.sqrt(FNET_C)
    cblk = np.kron(eye, np.cos(th)) * norm
    sblk = np.kron(eye, np.sin(th)) * norm
    return jnp.asarray(np.concatenate([cblk, sblk], axis=0).astype(np.float32)).astype(BF16)


def _block_diag(blocks):
    g = blocks.shape[-3]
    eye = jnp.eye(g, dtype=blocks.dtype)
    out = jnp.einsum('...gab,gh->...gahb', blocks, eye)
    return out.reshape(blocks.shape[:-3] + (g * blocks.shape[-2], g * blocks.shape[-1]))


def kernel(x, c, ctx, c_ctx, w_mod, b_mod, g_norm1, w_in, g_qnorm, g_knorm, lam_q1, lam_k1, lam_q2, lam_k2, g_subln, ssm_a_re, ssm_a_im, ssm_log_dt, ssm_b_re, ssm_b_im, ssm_c_re, ssm_c_im, ssm_d, w_glu, b_glu, w_fnet, w_out, g_norm2, w_ff1, w_ff2):
    bsz, seq, d = x.shape
    ctx_len = ctx.shape[1]
    depth = w_mod.shape[0]
    assert d == D_MODEL and ctx_len == TILE and seq % TILE == 0 and seq % GRID_W == 0
    assert bsz + 1 <= SUBLANES

    r_tot = seq + ctx_len
    assert (seq // TILE) % FOURIER_GROUP == 0 and seq % (S5_BLOCK * S5_CHUNK_ROWS) == 0
    xs = jnp.concatenate([x, ctx], axis=1)
    act = jnp.concatenate([c, c_ctx[None], jnp.zeros((SUBLANES - bsz - 1, d), F32)], axis=0)
    mods = _modulation(act, w_mod, b_mod)

    w_in_p = jnp.take(w_in, jnp.asarray(_in_proj_column_order()), axis=2).astype(BF16)
    _, gain_idx, _ = _head_lane_tables()
    gq = jnp.tile(g_qnorm[:, gain_idx], (1, N_HEADS)).reshape(depth, 1, QK_WIDTH)
    gk = jnp.tile(g_knorm[:, gain_idx], (1, N_HEADS)).reshape(depth, 1, QK_WIDTH)
    e_mat = jnp.asarray(_same_map_matrix()).astype(BF16)
    cos_t, sin_t = _rope_tables(seq, ctx_len)
    lam_init = np.array([0.8 - 0.6 * math.exp(-0.3 * i) for i in range(depth)], np.float32)
    pad = jnp.zeros((depth, LANES - HEAD_DIM), F32)
    lam_rows = [jnp.concatenate([v, pad], axis=1) for v in (lam_q1, lam_k1, lam_q2, lam_k2)]
    const_row = np.zeros((depth, LANES), np.float32)
    const_row[:, 0] = lam_init
    const_row[:, 1] = 1.0 - lam_init
    lam_p = jnp.stack(lam_rows + [jnp.asarray(const_row)]
                      + [jnp.zeros((depth, LANES), F32)] * (SUBLANES - 5), axis=1)
    g_sub = g_subln.reshape(depth, 1, V_DIM)

    a_re = ssm_a_re.reshape(depth, 2, 1, SSM_STATE)
    a_im = ssm_a_im.reshape(depth, 2, 1, SSM_STATE)
    ldt = jnp.repeat(ssm_log_dt, SSM_N, axis=-1).reshape(depth, 2, 1, SSM_STATE)
    b_re = _block_diag(jnp.swapaxes(ssm_b_re, -1, -2))
    b_im = _block_diag(jnp.swapaxes(ssm_b_im, -1, -2))
    c_stack = jnp.concatenate([_block_diag(jnp.swapaxes(ssm_c_re, -1, -2)),
                               -_block_diag(jnp.swapaxes(ssm_c_im, -1, -2))],
                              axis=2).astype(BF16)
    ct_re = _block_diag(ssm_c_re)
    ct_im = _block_diag(ssm_c_im)
    d_skip = ssm_d.reshape(depth, 1, SSM_WIDTH)
    w_glu_b = w_glu.astype(BF16)
    b_glu_r = b_glu.reshape(depth, 1, SSM_WIDTH)

    w_outer, m_lat = _dft_tables(seq)
    m_ctx = _tile_dft_table()
    c64_tab = _channel_dft_table()
    wf_blk = _block_diag(w_fnet).astype(BF16)

    w_out_b = w_out.astype(BF16)
    w_ff1_b = w_ff1.astype(BF16)
    w_ff2_b = w_ff2.astype(BF16)

    for layer in range(depth):
        q, kt, ve, u, f = _in_projection(layer, xs, mods, g_norm1, w_in_p, gq, gk, e_mat,
                                         cos_t, sin_t)
        attn = _attention(layer, q, kt, ve, lam_p, g_sub)
        y_fwd, y_bwd = [_s5_direction(layer, dr, u, a_re, a_im, ldt, b_re, b_im, ct_re, ct_im,
                                      c_stack) for dr in range(2)]
        fn = _fourier_inner(layer, _fourier_outer(f, w_outer), m_lat, c64_tab, wf_blk, r_tot)
        fn = _fourier_context(layer, f, fn, m_ctx, c64_tab, wf_blk)
        xs = _out_ffn(layer, xs, attn, y_fwd, y_bwd, u, fn, mods, d_skip, w_glu_b, b_glu_r,
                      w_out_b, g_norm2, w_ff1_b, w_ff2_b)
    return xs[:, :seq]
```

```python
import functools
import math

import numpy as np
import jax
import jax.numpy as jnp
from jax import lax
from jax.experimental import pallas as pl
from jax.experimental.pallas import tpu as pltpu

F32 = jnp.float32
BF16 = jnp.bfloat16

D_MODEL = 1024
GRID_W = 64
N_HEADS = 4
HEAD_DIM = 64
V_DIM = 2 * HEAD_DIM
QK_WIDTH = N_HEADS * 2 * HEAD_DIM
ATTN_WIDTH = N_HEADS * V_DIM
SSM_WIDTH = D_MODEL // 4
SSM_P = 16
SSM_G = SSM_WIDTH // SSM_P
SSM_N = 64
SSM_STATE = SSM_G * SSM_N
FNET_WIDTH = D_MODEL // 4
FNET_G = 4
FNET_C = FNET_WIDTH // FNET_G
IN_WIDTH = 2 * QK_WIDTH + ATTN_WIDTH + SSM_WIDTH + FNET_WIDTH
D_FF = 4 * D_MODEL
ROPE_BASE = 10000.0
EPS = 1e-6
SCALE = HEAD_DIM ** -0.5
LOG2E = math.log2(math.e)

TILE = 256
LANES = 128
SUBLANES = 8
FF_CHUNK = 1024
FOURIER_CHUNK = 8192
FOURIER_GROUP = 4
S5_BLOCK = 8
S5_CHUNK_ROWS = 256
VMEM_LIMIT = 48 * 1024 * 1024
S5_VMEM_LIMIT = 56 * 1024 * 1024


def _const_spec(shape, layer=None):
    nd = len(shape)
    if layer is None:
        return pl.BlockSpec(shape, lambda *_: (0,) * nd, pipeline_mode=pl.Buffered(1))
    return pl.BlockSpec((1,) + tuple(shape), lambda *_: (layer,) + (0,) * nd,
                        pipeline_mode=pl.Buffered(1))


def _params(sem, vmem=None):
    return pltpu.CompilerParams(dimension_semantics=sem, vmem_limit_bytes=vmem or VMEM_LIMIT)


def _rows_to_blocked(tok, scr_ref, blk_ref, period):
    n, w = tok.shape
    for h in range(w // LANES):
        scr_ref[h] = tok[:, h * LANES:(h + 1) * LANES]
    for j in range(period):
        for h in range(w // LANES):
            lo = j * w + h * LANES
            blk_ref[:, lo:lo + LANES] = scr_ref[h, pl.ds(j, n // period, stride=period), :]


def _blocked_to_rows(blk, scr_ref, period):
    nh, n, _ = scr_ref.shape
    w = nh * LANES
    for j in range(period):
        for h in range(nh):
            lo = j * w + h * LANES
            scr_ref[h, pl.ds(j, n // period, stride=period), :] = blk[:, lo:lo + LANES]
    return jnp.concatenate([scr_ref[h] for h in range(nh)], axis=1)


def _mod_kernel(act_ref, w_ref, b_ref, o_ref):
    a = act_ref[...]
    a = a * jax.nn.sigmoid(a)
    o_ref[0] = jnp.dot(a.astype(BF16), w_ref[0].astype(BF16),
                       preferred_element_type=F32) + b_ref[0]


def _modulation(act, w_mod, b_mod):
    depth, d, n = w_mod.shape
    bn = 512
    return pl.pallas_call(
        _mod_kernel,
        grid=(depth, n // bn),
        in_specs=[pl.BlockSpec((SUBLANES, d), lambda l, j: (0, 0)),
                  pl.BlockSpec((1, d, bn), lambda l, j: (l, 0, j)),
                  pl.BlockSpec((1, 1, bn), lambda l, j: (l, 0, j))],
        out_specs=pl.BlockSpec((1, SUBLANES, bn), lambda l, j: (l, 0, j)),
        out_shape=jax.ShapeDtypeStruct((depth, SUBLANES, n), F32),
        compiler_params=_params(("parallel", "parallel")),
        name="modulation",
    )(act, w_mod, b_mod.reshape(depth, 1, n))


def _inproj_kernel(x_ref, mod_ref, g1_ref, w_ref, gq_ref, gk_ref, e_ref, cos_ref, sin_ref,
                   q_ref, kt_ref, ve_ref, u_ref, u8_ref, f_ref, stage_ref):
    d = D_MODEL
    x = x_ref[0]
    ms = jnp.mean(x * x, axis=-1, keepdims=True)
    xn = x * lax.rsqrt(ms + EPS) * g1_ref[0]
    mod = mod_ref[0, 0]
    h = xn * (1.0 + mod[:, d:2 * d]) + mod[:, :d]
    proj = jnp.dot(h.astype(BF16), w_ref[0], preferred_element_type=F32)
    e = e_ref[...]
    cos = cos_ref[...]
    sin = sin_ref[...]

    def qk_norm(z, g):
        sq = z * z
        hi = sq.astype(BF16)
        lo = (sq - hi.astype(F32)).astype(BF16)
        parts = []
        for j in range(QK_WIDTH // 256):
            sl = slice(j * 256, (j + 1) * 256)
            parts.append(jnp.dot(hi[:, sl], e, preferred_element_type=F32)
                         + jnp.dot(lo[:, sl], e, preferred_element_type=F32))
        ssum = jnp.concatenate(parts, axis=-1)
        return z * lax.rsqrt(ssum * (1.0 / HEAD_DIM) + EPS) * g

    def rope(zh):
        return zh * cos + pltpu.roll(zh, LANES // 2, 1) * sin

    qn = qk_norm(proj[:, :QK_WIDTH], gq_ref[0])
    kn = qk_norm(proj[:, QK_WIDTH:2 * QK_WIDTH], gk_ref[0])
    row_map = (lax.broadcasted_iota(jnp.int32, (LANES, TILE), 0) // 32) % 2
    ones = jnp.ones((TILE, LANES), BF16)
    for hh in range(N_HEADS):
        sl = slice(hh * LANES, (hh + 1) * LANES)
        q_ref[0, :, sl] = (rope(qn[:, sl]) * (SCALE * LOG2E)).astype(BF16)
        kt = rope(kn[:, sl]).T
        kt_ref[0, hh, 0] = jnp.where(row_map == 0, kt, 0.0).astype(BF16)
        kt_ref[0, hh, 1] = jnp.where(row_map == 1, kt, 0.0).astype(BF16)
        vo = 2 * QK_WIDTH + hh * V_DIM
        ve_ref[0, hh, :, :V_DIM] = proj[:, vo:vo + V_DIM].astype(BF16)
        ve_ref[0, hh, :, V_DIM:] = ones
    uo = 2 * QK_WIDTH + ATTN_WIDTH
    u_ref[0] = proj[:, uo:uo + SSM_WIDTH]
    _rows_to_blocked(proj[:, uo:uo + SSM_WIDTH], stage_ref, u8_ref.at[0], S5_BLOCK)
    f_ref[0] = proj[:, uo + SSM_WIDTH:]


def _in_projection(layer, xs, mods, g_norm1, w_in, gq, gk, e_mat, cos_t, sin_t):
    b, r, d = xs.shape
    nt = r // TILE
    depth = w_in.shape[0]
    return pl.pallas_call(
        _inproj_kernel,
        grid=(b, nt),
        in_specs=[
            pl.BlockSpec((1, TILE, d), lambda i, t: (i, t, 0)),
            pl.BlockSpec((1, 1, 1, 6 * d),
                         lambda i, t: (layer, jnp.where(t == nt - 1, b, i), 0, 0)),
            _const_spec((1, d), layer),
            _const_spec((d, IN_WIDTH), layer),
            _const_spec((1, QK_WIDTH), layer),
            _const_spec((1, QK_WIDTH), layer),
            _const_spec((256, 256)),
            pl.BlockSpec((TILE, LANES), lambda i, t: (t, 0)),
            pl.BlockSpec((TILE, LANES), lambda i, t: (t, 0)),
        ],
        out_specs=[
            pl.BlockSpec((1, TILE, QK_WIDTH), lambda i, t: (i, t, 0)),
            pl.BlockSpec((1, N_HEADS, 2, LANES, TILE), lambda i, t: (i, 0, 0, 0, t)),
            pl.BlockSpec((1, N_HEADS, TILE, 2 * V_DIM), lambda i, t: (i, 0, t, 0)),
            pl.BlockSpec((1, TILE, SSM_WIDTH), lambda i, t: (i, t, 0)),
            pl.BlockSpec((1, TILE // S5_BLOCK, S5_BLOCK * SSM_WIDTH), lambda i, t: (i, t, 0)),
            pl.BlockSpec((1, TILE, FNET_WIDTH), lambda i, t: (i, t, 0)),
        ],
        out_shape=[
            jax.ShapeDtypeStruct((b, r, QK_WIDTH), BF16),
            jax.ShapeDtypeStruct((b, N_HEADS, 2, LANES, r), BF16),
            jax.ShapeDtypeStruct((b, N_HEADS, r, 2 * V_DIM), BF16),
            jax.ShapeDtypeStruct((b, r, SSM_WIDTH), F32),
            jax.ShapeDtypeStruct((b, r // S5_BLOCK, S5_BLOCK * SSM_WIDTH), F32),
            jax.ShapeDtypeStruct((b, r, FNET_WIDTH), F32),
        ],
        scratch_shapes=[pltpu.VMEM((SSM_WIDTH // LANES, TILE, LANES), F32)],
        compiler_params=_params(("parallel", "parallel")),
        name="in_projection",
    )(xs, mods.reshape(depth, SUBLANES, 1, 6 * d), g_norm1.reshape(depth, 1, d), w_in,
      gq, gk, e_mat, cos_t, sin_t)


def _attn_kernel(q_ref, kt_ref, ve_ref, lam_ref, gs_ref, o_ref, *, nt, ctx_len):
    t = pl.program_id(2)
    q = q_ref[0]
    r = ve_ref.shape[2]

    def attend(k0):
        outs = []
        ve = ve_ref[0, 0, k0:, :]
        scores = [jnp.dot(q, kt_ref[0, 0, mp, :, k0:], preferred_element_type=F32)
                  for mp in range(2)]
        for s in scores:
            p = jnp.exp2(s - jnp.max(s, axis=1, keepdims=True))
            acc = jnp.dot(p.astype(BF16), ve, preferred_element_type=F32)
            outs.append(acc[:, :V_DIM] / acc[:, V_DIM:])
        lp = lam_ref[0]
        s1 = jnp.sum(lp[0:1] * lp[1:2], axis=-1, keepdims=True)
        s2 = jnp.sum(lp[2:3] * lp[3:4], axis=-1, keepdims=True)
        lam = jnp.exp(s1) - jnp.exp(s2) + lp[4:5, 0:1]
        a = outs[0] - lam * outs[1]
        a = a * lax.rsqrt(jnp.mean(a * a, axis=-1, keepdims=True) + EPS)
        o_ref[0] = (a * gs_ref[0] * lp[4:5, 1:2]).astype(BF16)

    @pl.when(t < nt - 1)
    def _latent_queries():
        attend(0)

    @pl.when(t == nt - 1)
    def _context_queries():
        attend(r - ctx_len)


def _attention(layer, q, kt, ve, lam_p, g_subln):
    b, r, _ = q.shape
    nt = r // TILE
    return pl.pallas_call(
        functools.partial(_attn_kernel, nt=nt, ctx_len=TILE),
        grid=(b, N_HEADS, nt),
        in_specs=[
            pl.BlockSpec((1, TILE, LANES), lambda i, h, t: (i, t, h)),
            pl.BlockSpec((1, 1, 2, LANES, r), lambda i, h, t: (i, h, 0, 0, 0),
                         pipeline_mode=pl.Buffered(1)),
            pl.BlockSpec((1, 1, r, 2 * V_DIM), lambda i, h, t: (i, h, 0, 0),
                         pipeline_mode=pl.Buffered(1)),
            _const_spec((SUBLANES, LANES), layer),
            _const_spec((1, V_DIM), layer),
        ],
        out_specs=pl.BlockSpec((1, TILE, V_DIM), lambda i, h, t: (i, t, h)),
        out_shape=jax.ShapeDtypeStruct((b, r, ATTN_WIDTH), BF16),
        compiler_params=_params(("parallel", "parallel", "arbitrary")),
        name="diff_attention",
    )(q, kt, ve, lam_p, g_subln)


def _zoh(are_ref, aim_ref, ldt_ref):
    a_re = are_ref[0, 0]
    a_im = aim_ref[0, 0]
    dt = jnp.exp(ldt_ref[0, 0])
    mag = jnp.exp(dt * a_re)
    ang = dt * a_im
    ab_re = mag * jnp.cos(ang)
    ab_im = mag * jnp.sin(ang)
    den = a_re * a_re + a_im * a_im
    n_re = ab_re - 1.0
    f_re = (n_re * a_re + ab_im * a_im) / den
    f_im = (ab_im * a_re - n_re * a_im) / den
    return ab_re, ab_im, f_re, f_im


def _complex_powers(base_re, base_im, n):
    pows = [(jnp.ones_like(base_re), jnp.zeros_like(base_im))]
    for _ in range(n):
        pr, pi = pows[-1]
        pows.append((pr * base_re - pi * base_im, pr * base_im + pi * base_re))
    return pows


def _fill_scan_tables(pw_ref, base_re, base_im, reverse):
    shape = (SUBLANES, SSM_STATE)
    pows = _complex_powers(base_re, base_im, SUBLANES)
    row = lax.broadcasted_iota(jnp.int32, shape, 0)
    zero = jnp.zeros(shape, F32)
    for idx, k in enumerate((1, 2, 4)):
        mask = (row + k <= SUBLANES - 1) if reverse else (row >= k)
        pw_ref[2 * idx] = jnp.where(mask, jnp.broadcast_to(pows[k][0], shape), zero)
        pw_ref[2 * idx + 1] = jnp.where(mask, jnp.broadcast_to(pows[k][1], shape), zero)
    pcr = zero
    pci = zero
    for tt in range(SUBLANES):
        e = (SUBLANES - tt) if reverse else tt + 1
        pcr = jnp.where(row == tt, jnp.broadcast_to(pows[e][0], shape), pcr)
        pci = jnp.where(row == tt, jnp.broadcast_to(pows[e][1], shape), pci)
    pw_ref[6] = pcr
    pw_ref[7] = pci


def _scan_rows(s_ref, pw_ref, carry, nrows, reverse, exclusive):
    ns = SSM_STATE
    shape = (SUBLANES, ns)
    ngroups = nrows // SUBLANES
    edge = (SUBLANES - 1) if reverse else 0

    def group(g, carry):
        cr, ci = carry
        gi = (ngroups - 1 - g) if reverse else g
        r0 = pl.multiple_of(gi * SUBLANES, SUBLANES)
        xr = s_ref[pl.ds(r0, SUBLANES), :ns]
        xi = s_ref[pl.ds(r0, SUBLANES), ns:]
        for idx, k in enumerate((1, 2, 4)):
            sh = (SUBLANES - k) if reverse else k
            sr = pltpu.roll(xr, sh, 0)
            si = pltpu.roll(xi, sh, 0)
            pr = pw_ref[2 * idx]
            pi = pw_ref[2 * idx + 1]
            xr, xi = xr + pr * sr - pi * si, xi + pr * si + pi * sr
        pr = pw_ref[6]
        pi = pw_ref[7]
        xr, xi = xr + pr * cr - pi * ci, xi + pr * ci + pi * cr
        if exclusive:
            row = lax.broadcasted_iota(jnp.int32, shape, 0)
            sh = (SUBLANES - 1) if reverse else 1
            er = jnp.where(row == edge, cr, pltpu.roll(xr, sh, 0))
            ei = jnp.where(row == edge, ci, pltpu.roll(xi, sh, 0))
        else:
            er, ei = xr, xi
        s_ref[pl.ds(r0, SUBLANES), :ns] = er
        s_ref[pl.ds(r0, SUBLANES), ns:] = ei
        last = 0 if reverse else SUBLANES - 1
        return (jnp.broadcast_to(xr[last:last + 1], shape),
                jnp.broadcast_to(xi[last:last + 1], shape))

    return lax.fori_loop(0, ngroups, group, carry)


def _s5_ctx_kernel(u_ref, are_ref, aim_ref, ldt_ref, bre_ref, bim_ref, c_ref,
                   o_ref, st_ref, bbar_ref, pw_ref, s_ref, y_ref, *, reverse):
    ns = SSM_STATE
    ab_re, ab_im, f_re, f_im = _zoh(are_ref, aim_ref, ldt_ref)
    bre = bre_ref[0, 0]
    bim = bim_ref[0, 0]
    bbar_ref[:, :ns] = (f_re * bre - f_im * bim).astype(BF16)
    bbar_ref[:, ns:] = (f_re * bim + f_im * bre).astype(BF16)
    _fill_scan_tables(pw_ref, ab_re, ab_im, reverse)
    s_ref[...] = jnp.dot(u_ref[0].astype(BF16), bbar_ref[...], preferred_element_type=F32)
    zero = jnp.zeros((SUBLANES, ns), F32)
    cr, ci = _scan_rows(s_ref, pw_ref, (zero, zero), TILE, reverse, exclusive=False)
    st_ref[0, 0] = cr
    st_ref[0, 1] = ci
    y = jnp.dot(s_ref[...].astype(BF16), c_ref[0, 0], preferred_element_type=F32)
    _rows_to_blocked(y, y_ref, o_ref.at[0], S5_BLOCK)


def _s5_block_kernel(u8_ref, s0_ref, are_ref, aim_ref, ldt_ref, bre_ref, bim_ref, ctre_ref,
                     ctim_ref, y_ref, o_ref, wx_ref, wc_ref, wt_ref, pw_ref, carry_ref, s_ref,
                     *, reverse):
    del y_ref
    ns = SSM_STATE
    w = SSM_WIDTH
    nb = S5_BLOCK
    c = pl.program_id(0)
    b = pl.program_id(1)

    @pl.when((c == 0) & (b == 0))
    def _build_maps():
        ab_re, ab_im, f_re, f_im = _zoh(are_ref, aim_ref, ldt_ref)
        bre = bre_ref[0, 0]
        bim = bim_ref[0, 0]
        bb_re = f_re * bre - f_im * bim
        bb_im = f_re * bim + f_im * bre
        pows = _complex_powers(ab_re, ab_im, nb)
        ctre = ctre_ref[0, 0]
        ctim = ctim_ref[0, 0]
        ct_stack = jnp.concatenate([ctre, -ctim], axis=1).astype(BF16)
        taps = []
        for e in range(nb):
            pr, pi = pows[e]
            xr = pr * bb_re - pi * bb_im
            xi = pr * bb_im + pi * bb_re
            i = e if reverse else nb - 1 - e
            wx_ref[i * w:(i + 1) * w, :ns] = xr.astype(BF16)
            wx_ref[i * w:(i + 1) * w, ns:] = xi.astype(BF16)
            xk = jnp.concatenate([xr, xi], axis=1).astype(BF16)
            taps.append(lax.dot_general(xk, ct_stack, (((1,), (1,)), ((), ())),
                                        preferred_element_type=F32).astype(BF16))
        zero_blk = jnp.zeros((w, w), BF16)
        for i in range(nb):
            for j in range(nb):
                lag = (i - j) if reverse else (j - i)
                wt_ref[i * w:(i + 1) * w, j * w:(j + 1) * w] = taps[lag] if lag >= 0 else zero_blk
        for j in range(nb):
            pr, pi = pows[nb - j] if reverse else pows[j + 1]
            wc_ref[:ns, j * w:(j + 1) * w] = (ctre * pr - ctim * pi).T.astype(BF16)
            wc_ref[ns:, j * w:(j + 1) * w] = (-(ctre * pi + ctim * pr)).T.astype(BF16)
        _fill_scan_tables(pw_ref, pows[nb][0], pows[nb][1], reverse)

    @pl.when(c == 0)
    def _load_state():
        carry_ref[b] = s0_ref[0]

    u8 = u8_ref[0].astype(BF16)
    s_ref[...] = jnp.dot(u8, wx_ref[...], preferred_element_type=F32)
    cr, ci = _scan_rows(s_ref, pw_ref, (carry_ref[b, 0], carry_ref[b, 1]), S5_CHUNK_ROWS,
                        reverse, exclusive=True)
    carry_ref[b, 0] = cr
    carry_ref[b, 1] = ci
    o_ref[0] = (jnp.dot(s_ref[...].astype(BF16), wc_ref[...], preferred_element_type=F32)
                + jnp.dot(u8, wt_ref[...], preferred_element_type=F32))


def _s5_direction(layer, direction, u, u8, a_re, a_im, ldt, b_re, b_im, ct_re, ct_im, c_stack):
    b, r, w = u.shape
    nt = r // TILE
    rows = r // S5_BLOCK
    wide = S5_BLOCK * w
    tile_rows = TILE // S5_BLOCK
    ns = SSM_STATE
    reverse = direction == 1

    def dir_spec(shape):
        nd = len(shape)
        return pl.BlockSpec((1, 1) + tuple(shape), lambda *_: (layer, direction) + (0,) * nd,
                            pipeline_mode=pl.Buffered(1))

    y_ctx, state = pl.pallas_call(
        functools.partial(_s5_ctx_kernel, reverse=reverse),
        grid=(b,),
        in_specs=[pl.BlockSpec((1, TILE, w), lambda i: (i, nt - 1, 0)),
                  dir_spec((1, ns)), dir_spec((1, ns)), dir_spec((1, ns)),
                  dir_spec((w, ns)), dir_spec((w, ns)), dir_spec((2 * ns, w))],
        out_specs=[pl.BlockSpec((1, tile_rows, wide), lambda i: (i, nt - 1, 0)),
                   pl.BlockSpec((1, 2, SUBLANES, ns), lambda i: (i, 0, 0, 0))],
        out_shape=[jax.ShapeDtypeStruct((b, rows, wide), F32),
                   jax.ShapeDtypeStruct((b, 2, SUBLANES, ns), F32)],
        scratch_shapes=[pltpu.VMEM((w, 2 * ns), BF16),
                        pltpu.VMEM((8, SUBLANES, ns), F32),
                        pltpu.VMEM((TILE, 2 * ns), F32),
                        pltpu.VMEM((w // LANES, TILE, LANES), F32)],
        compiler_params=_params(("parallel",)),
        name="s5_context",
    )(u, a_re, a_im, ldt, b_re, b_im, c_stack)

    nchunks = (r - TILE) // (S5_BLOCK * S5_CHUNK_ROWS)

    def chunk(c, i):
        return (i, nchunks - 1 - c if reverse else c, 0)

    return pl.pallas_call(
        functools.partial(_s5_block_kernel, reverse=reverse),
        grid=(nchunks, b),
        in_specs=[pl.BlockSpec((1, S5_CHUNK_ROWS, wide), chunk),
                  pl.BlockSpec((1, 2, SUBLANES, ns), lambda c, i: (i, 0, 0, 0)),
                  dir_spec((1, ns)), dir_spec((1, ns)), dir_spec((1, ns)),
                  dir_spec((w, ns)), dir_spec((w, ns)), dir_spec((w, ns)), dir_spec((w, ns)),
                  pl.BlockSpec(memory_space=pl.ANY)],
        out_specs=pl.BlockSpec((1, S5_CHUNK_ROWS, wide), chunk),
        out_shape=jax.ShapeDtypeStruct((b, rows, wide), F32),
        scratch_shapes=[pltpu.VMEM((wide, 2 * ns), BF16),
                        pltpu.VMEM((2 * ns, wide), BF16),
                        pltpu.VMEM((wide, wide), BF16),
                        pltpu.VMEM((8, SUBLANES, ns), F32),
                        pltpu.VMEM((b, 2, SUBLANES, ns), F32),
                        pltpu.VMEM((S5_CHUNK_ROWS, 2 * ns), F32)],
        input_output_aliases={9: 0},
        compiler_params=_params(("arbitrary", "arbitrary"), vmem=S5_VMEM_LIMIT),
        name="s5_blocks_reverse" if reverse else "s5_blocks_forward",
    )(u8, state, a_re, a_im, ldt, b_re, b_im, ct_re, ct_im, y_ctx)


def _fourier_outer_kernel(w_ref, x_ref, g_ref):
    g_ref[0] = jnp.dot(w_ref[...], x_ref[0].astype(BF16),
                       preferred_element_type=F32).astype(BF16)


def _fourier_outer(f, w_outer):
    b, r, w = f.shape
    nt = r // TILE
    l2n = nt - 1
    flat = TILE * w
    return pl.pallas_call(
        _fourier_outer_kernel,
        grid=(b, flat // FOURIER_CHUNK),
        in_specs=[_const_spec((2 * l2n, l2n)),
                  pl.BlockSpec((1, l2n, FOURIER_CHUNK), lambda i, j: (i, 0, j))],
        out_specs=pl.BlockSpec((1, 2 * l2n, FOURIER_CHUNK), lambda i, j: (i, 0, j)),
        out_shape=jax.ShapeDtypeStruct((b, 2 * l2n, flat), BF16),
        compiler_params=_params(("parallel", "parallel")),
        name="fourier_outer",
    )(w_outer, f.reshape(b, nt, flat))


def _fourier_tail(y_re, y_im, c64_ref, wf_ref):
    w = FNET_WIDTH
    z = (jnp.dot(y_re.astype(BF16), c64_ref[:w], preferred_element_type=F32)
         + jnp.dot(y_im.astype(BF16), c64_ref[w:], preferred_element_type=F32))
    return jnp.dot(z.astype(BF16), wf_ref[0], preferred_element_type=F32)


def _fourier_inner_kernel(gr_ref, gi_ref, m_ref, c64_ref, wf_ref, o_ref):
    w = FNET_WIDTH
    for i in range(FOURIER_GROUP):
        y = (jnp.dot(m_ref[i, :, :TILE], gr_ref[0, i], preferred_element_type=F32)
             + jnp.dot(m_ref[i, :, TILE:], gi_ref[0, i], preferred_element_type=F32))
        o_ref[0, :, i * w:(i + 1) * w] = _fourier_tail(y[:TILE], y[TILE:], c64_ref, wf_ref)


def _fourier_inner(layer, g, m_tab, c64_tab, wf_blk, r):
    b, two_l2n, flat = g.shape
    l2n = two_l2n // 2
    w = flat // TILE
    steps = l2n // FOURIER_GROUP
    g4 = g.reshape(b, two_l2n, TILE, w)
    return pl.pallas_call(
        _fourier_inner_kernel,
        grid=(b, steps),
        in_specs=[
            pl.BlockSpec((1, FOURIER_GROUP, TILE, w), lambda i, j: (i, j, 0, 0)),
            pl.BlockSpec((1, FOURIER_GROUP, TILE, w), lambda i, j: (i, steps + j, 0, 0)),
            pl.BlockSpec((FOURIER_GROUP, 2 * TILE, 2 * TILE), lambda i, j: (j, 0, 0)),
            _const_spec((2 * w, w)),
            _const_spec((w, w), layer),
        ],
        out_specs=pl.BlockSpec((1, TILE, FOURIER_GROUP * w), lambda i, j: (i, 0, j)),
        out_shape=jax.ShapeDtypeStruct((b, r // l2n, l2n * w), F32),
        compiler_params=_params(("parallel", "parallel")),
        name="fourier_inner",
    )(g4, g4, m_tab, c64_tab, wf_blk)


def _fourier_ctx_kernel(x_ref, m_ref, c64_ref, wf_ref, fn_ref, o_ref, z_ref, *, l2n):
    del fn_ref
    y = jnp.dot(m_ref[...], x_ref[0].astype(BF16), preferred_element_type=F32)
    z = _fourier_tail(y[:TILE], y[TILE:], c64_ref, wf_ref)
    _rows_to_blocked(z, z_ref, o_ref.at[0], l2n)


def _fourier_context(layer, f, fn, m_ctx, c64_tab, wf_blk):
    b, r, w = f.shape
    nt = r // TILE
    l2n = nt - 1
    return pl.pallas_call(
        functools.partial(_fourier_ctx_kernel, l2n=l2n),
        grid=(b,),
        in_specs=[
            pl.BlockSpec((1, TILE, w), lambda i: (i, nt - 1, 0)),
            _const_spec((2 * TILE, TILE)),
            _const_spec((2 * w, w)),
            _const_spec((w, w), layer),
            pl.BlockSpec(memory_space=pl.ANY),
        ],
        out_specs=pl.BlockSpec((1, TILE // l2n, l2n * w), lambda i: (i, nt - 1, 0)),
        out_shape=jax.ShapeDtypeStruct((b, r // l2n, l2n * w), F32),
        scratch_shapes=[pltpu.VMEM((w // LANES, TILE, LANES), F32)],
        input_output_aliases={4: 0},
        compiler_params=_params(("parallel",)),
        name="fourier_context",
    )(f, m_ctx, c64_tab, wf_blk, fn)


def _outffn_kernel(x_ref, a_ref, yf_ref, yb_ref, u_ref, fn_ref, mod_ref, dsk_ref, wglu_ref,
                   bglu_ref, wo_ref, g2_ref, w1_ref, w2_ref, o_ref, ytok_ref, ftok_ref, *, l2n):
    d = D_MODEL
    x = x_ref[0]
    mod = mod_ref[0, 0]
    o1 = ATTN_WIDTH
    o2 = ATTN_WIDTH + SSM_WIDTH
    y_scan = _blocked_to_rows(yf_ref[0] + yb_ref[0], ytok_ref, S5_BLOCK)
    fnet = _blocked_to_rows(fn_ref[0], ftok_ref, l2n)
    hg = jax.nn.gelu(y_scan + dsk_ref[0] * u_ref[0])
    zg = jnp.dot(hg.astype(BF16), wglu_ref[0], preferred_element_type=F32) + bglu_ref[0]
    s5 = (hg * jax.nn.sigmoid(zg)).astype(BF16)
    mix = (jnp.dot(a_ref[0], wo_ref[0, :o1], preferred_element_type=F32)
           + jnp.dot(s5, wo_ref[0, o1:o2], preferred_element_type=F32)
           + jnp.dot(fnet.astype(BF16), wo_ref[0, o2:], preferred_element_type=F32))
    x1 = x + mod[:, 2 * d:3 * d] * mix
    ms = jnp.mean(x1 * x1, axis=-1, keepdims=True)
    hn = x1 * lax.rsqrt(ms + EPS) * g2_ref[0]
    h = (hn * (1.0 + mod[:, 4 * d:5 * d]) + mod[:, 3 * d:4 * d]).astype(BF16)
    acc = jnp.zeros((TILE, d), F32)
    for cc in range(D_FF // FF_CHUNK):
        sl = slice(cc * FF_CHUNK, (cc + 1) * FF_CHUNK)
        t = jnp.dot(h, w1_ref[0, :, sl], preferred_element_type=F32)
        t = jnp.square(jnp.maximum(t, 0.0))
        acc = acc + jnp.dot(t.astype(BF16), w2_ref[0, sl, :], preferred_element_type=F32)
    o_ref[0] = x1 + mod[:, 5 * d:] * acc


def _out_ffn(layer, xs, attn, y_fwd, y_bwd, u, fn, mods, d_skip, w_glu, b_glu, w_out, g_norm2,
             w_ff1, w_ff2, latent_only):
    b, r, d = xs.shape
    nt = r // TILE
    l2n = nt - 1
    depth = w_out.shape[0]
    s5_tile = pl.BlockSpec((1, TILE // S5_BLOCK, S5_BLOCK * SSM_WIDTH), lambda i, t: (i, t, 0))
    fn_tile = pl.BlockSpec((1, TILE // l2n, l2n * FNET_WIDTH), lambda i, t: (i, t, 0))

    def tile(width):
        return pl.BlockSpec((1, TILE, width), lambda i, t: (i, t, 0))

    return pl.pallas_call(
        functools.partial(_outffn_kernel, l2n=l2n),
        grid=(b, nt - 1 if latent_only else nt),
        in_specs=[
            tile(d), tile(ATTN_WIDTH), s5_tile, s5_tile, tile(SSM_WIDTH), fn_tile,
            pl.BlockSpec((1, 1, 1, 6 * d),
                         lambda i, t: (layer, jnp.where(t == nt - 1, b, i), 0, 0)),
            _const_spec((1, SSM_WIDTH), layer),
            _const_spec((SSM_WIDTH, SSM_WIDTH), layer),
            _const_spec((1, SSM_WIDTH), layer),
            _const_spec((d, d), layer),
            _const_spec((1, d), layer),
            _const_spec((d, D_FF), layer),
            _const_spec((D_FF, d), layer),
        ],
        out_specs=tile(d),
        out_shape=jax.ShapeDtypeStruct((b, r - TILE if latent_only else r, d), F32),
        scratch_shapes=[pltpu.VMEM((SSM_WIDTH // LANES, TILE, LANES), F32),
                        pltpu.VMEM((FNET_WIDTH // LANES, TILE, LANES), F32)],
        input_output_aliases={} if latent_only else {0: 0},
        compiler_params=_params(("parallel", "parallel")),
        name="out_ffn",
    )(xs, attn, y_fwd, y_bwd, u, fn, mods.reshape(depth, SUBLANES, 1, 6 * d), d_skip, w_glu, b_glu,
      w_out, g_norm2.reshape(depth, 1, d), w_ff1, w_ff2)


def _head_lane_tables():
    j = np.arange(LANES)
    half = j // 64
    mp = (j // 32) % 2
    idx = j % 32
    src_in_head = mp * HEAD_DIM + half * 32 + idx
    gain_idx = half * 32 + idx
    return src_in_head, gain_idx, mp


def _in_proj_column_order():
    src_in_head, _, _ = _head_lane_tables()
    qk = np.concatenate([h * LANES + src_in_head for h in range(N_HEADS)])
    return np.concatenate([qk, QK_WIDTH + qk, np.arange(2 * QK_WIDTH, IN_WIDTH)])


def _same_map_matrix():
    i = np.arange(256)
    head = i // LANES
    mp = ((i % LANES) // 32) % 2
    same = (head[:, None] == head[None, :]) & (mp[:, None] == mp[None, :])
    return same.astype(np.float32)


def _rope_tables(seq, ctx_len):
    t = jnp.arange(seq)
    row = (t // GRID_W).astype(F32)
    col = (t % GRID_W).astype(F32)
    n_freq = HEAD_DIM // 4
    inv = jnp.power(ROPE_BASE, -jnp.arange(n_freq, dtype=F32) / n_freq)
    ang = jnp.concatenate([row[:, None] * inv, col[:, None] * inv], axis=-1)
    cos = jnp.tile(jnp.cos(ang), (1, 4))
    sign = np.where(np.arange(LANES) < LANES // 2, -1.0, 1.0).astype(np.float32)
    sin = jnp.tile(jnp.sin(ang), (1, 4)) * sign
    cos = jnp.concatenate([cos, jnp.ones((ctx_len, LANES), F32)], axis=0)
    sin = jnp.concatenate([sin, jnp.zeros((ctx_len, LANES), F32)], axis=0)
    return cos, sin


def _dft_tables(length):
    l2n = length // TILE
    k2 = np.arange(l2n)
    phi = 2.0 * np.pi * ((k2[:, None] * k2[None, :]) % l2n) / l2n
    w_outer = np.concatenate([np.cos(phi), -np.sin(phi)], axis=0).astype(np.float32)
    l1 = np.arange(TILE)
    alpha = 2.0 * np.pi * ((k2[:, None] * l1[None, :]) % length) / length
    beta = 2.0 * np.pi * ((l1[:, None] * l1[None, :]) % TILE) / TILE
    ca = jnp.asarray(np.cos(alpha).astype(np.float32))[:, None, :]
    sa = jnp.asarray(np.sin(alpha).astype(np.float32))[:, None, :]
    cb = jnp.asarray(np.cos(beta).astype(np.float32))[None]
    sb = jnp.asarray(np.sin(beta).astype(np.float32))[None]
    norm = 1.0 / math.sqrt(length)
    ct = (ca * cb - sa * sb) * norm
    st = (sa * cb + ca * sb) * norm
    m = jnp.concatenate([jnp.concatenate([ct, st], axis=2),
                         jnp.concatenate([-st, ct], axis=2)], axis=1).astype(BF16)
    return jnp.asarray(w_outer).astype(BF16), m


def _tile_dft_table():
    l1 = np.arange(TILE)
    beta = 2.0 * np.pi * ((l1[:, None] * l1[None, :]) % TILE) / TILE
    tab = np.concatenate([np.cos(beta), -np.sin(beta)], axis=0) / math.sqrt(TILE)
    return jnp.asarray(tab.astype(np.float32)).astype(BF16)


def _channel_dft_table():
    c = np.arange(FNET_C)
    th = 2.0 * np.pi * ((c[:, None] * c[None, :]) % FNET_C) / FNET_C
    eye = np.eye(FNET_G)
    norm = 1.0 / math.sqrt(FNET_C)
    cblk = np.kron(eye, np.cos(th)) * norm
    sblk = np.kron(eye, np.sin(th)) * norm
    return jnp.asarray(np.concatenate([cblk, sblk], axis=0).astype(np.float32)).astype(BF16)


def _block_diag(blocks):
    g = blocks.shape[-3]
    eye = jnp.eye(g, dtype=blocks.dtype)
    out = jnp.einsum('...gab,gh->...gahb', blocks, eye)
    return out.reshape(blocks.shape[:-3] + (g * blocks.shape[-2], g * blocks.shape[-1]))


def kernel(x, c, ctx, c_ctx, w_mod, b_mod, g_norm1, w_in, g_qnorm, g_knorm, lam_q1, lam_k1, lam_q2, lam_k2, g_subln, ssm_a_re, ssm_a_im, ssm_log_dt, ssm_b_re, ssm_b_im, ssm_c_re, ssm_c_im, ssm_d, w_glu, b_glu, w_fnet, w_out, g_norm2, w_ff1, w_ff2):
    bsz, seq, d = x.shape
    ctx_len = ctx.shape[1]
    depth = w_mod.shape[0]
    assert d == D_MODEL and ctx_len == TILE and seq % TILE == 0 and seq % GRID_W == 0
    assert bsz + 1 <= SUBLANES

    r_tot = seq + ctx_len
    assert (seq // TILE) % FOURIER_GROUP == 0 and seq % (S5_BLOCK * S5_CHUNK_ROWS) == 0
    xs = jnp.concatenate([x, ctx], axis=1)
    act = jnp.concatenate([c, c_ctx[None], jnp.zeros((SUBLANES - bsz - 1, d), F32)], axis=0)
    mods = _modulation(act, w_mod, b_mod)

    w_in_p = jnp.take(w_in, jnp.asarray(_in_proj_column_order()), axis=2).astype(BF16)
    _, gain_idx, _ = _head_lane_tables()
    gq = jnp.tile(g_qnorm[:, gain_idx], (1, N_HEADS)).reshape(depth, 1, QK_WIDTH)
    gk = jnp.tile(g_knorm[:, gain_idx], (1, N_HEADS)).reshape(depth, 1, QK_WIDTH)
    e_mat = jnp.asarray(_same_map_matrix()).astype(BF16)
    cos_t, sin_t = _rope_tables(seq, ctx_len)
    lam_init = np.array([0.8 - 0.6 * math.exp(-0.3 * i) for i in range(depth)], np.float32)
    pad = jnp.zeros((depth, LANES - HEAD_DIM), F32)
    lam_rows = [jnp.concatenate([v, pad], axis=1) for v in (lam_q1, lam_k1, lam_q2, lam_k2)]
    const_row = np.zeros((depth, LANES), np.float32)
    const_row[:, 0] = lam_init
    const_row[:, 1] = 1.0 - lam_init
    lam_p = jnp.stack(lam_rows + [jnp.asarray(const_row)]
                      + [jnp.zeros((depth, LANES), F32)] * (SUBLANES - 5), axis=1)
    g_sub = g_subln.reshape(depth, 1, V_DIM)

    a_re = ssm_a_re.reshape(depth, 2, 1, SSM_STATE)
    a_im = ssm_a_im.reshape(depth, 2, 1, SSM_STATE)
    ldt = jnp.repeat(ssm_log_dt, SSM_N, axis=-1).reshape(depth, 2, 1, SSM_STATE)
    b_re = _block_diag(jnp.swapaxes(ssm_b_re, -1, -2))
    b_im = _block_diag(jnp.swapaxes(ssm_b_im, -1, -2))
    c_stack = jnp.concatenate([_block_diag(jnp.swapaxes(ssm_c_re, -1, -2)),
                               -_block_diag(jnp.swapaxes(ssm_c_im, -1, -2))],
                              axis=2).astype(BF16)
    ct_re = _block_diag(ssm_c_re)
    ct_im = _block_diag(ssm_c_im)
    d_skip = ssm_d.reshape(depth, 1, SSM_WIDTH)
    w_glu_b = w_glu.astype(BF16)
    b_glu_r = b_glu.reshape(depth, 1, SSM_WIDTH)

    w_outer, m_lat = _dft_tables(seq)
    m_ctx = _tile_dft_table()
    c64_tab = _channel_dft_table()
    wf_blk = _block_diag(w_fnet).astype(BF16)

    w_out_b = w_out.astype(BF16)
    w_ff1_b = w_ff1.astype(BF16)
    w_ff2_b = w_ff2.astype(BF16)

    for layer in range(depth):
        q, kt, ve, u, u8, f = _in_projection(layer, xs, mods, g_norm1, w_in_p, gq, gk, e_mat,
                                         cos_t, sin_t)
        attn = _attention(layer, q, kt, ve, lam_p, g_sub)
        y_fwd, y_bwd = [_s5_direction(layer, dr, u, u8, a_re, a_im, ldt, b_re, b_im, ct_re,
                                      ct_im, c_stack) for dr in range(2)]
        fn = _fourier_inner(layer, _fourier_outer(f, w_outer), m_lat, c64_tab, wf_blk, r_tot)
        last = layer == depth - 1
        if not last:
            fn = _fourier_context(layer, f, fn, m_ctx, c64_tab, wf_blk)
        xs = _out_ffn(layer, xs, attn, y_fwd, y_bwd, u, fn, mods, d_skip, w_glu_b, b_glu_r,
                      w_out_b, g_norm2, w_ff1_b, w_ff2_b, latent_only=last)
    return xs
```

```python
import functools
import math

import numpy as np
import jax
import jax.numpy as jnp
from jax import lax
from jax.experimental import pallas as pl
from jax.experimental.pallas import tpu as pltpu

F32 = jnp.float32
BF16 = jnp.bfloat16

D_MODEL = 1024
GRID_W = 64
N_HEADS = 4
HEAD_DIM = 64
V_DIM = 2 * HEAD_DIM
QK_WIDTH = N_HEADS * 2 * HEAD_DIM
ATTN_WIDTH = N_HEADS * V_DIM
SSM_WIDTH = D_MODEL // 4
SSM_P = 16
SSM_G = SSM_WIDTH // SSM_P
SSM_N = 64
SSM_STATE = SSM_G * SSM_N
FNET_WIDTH = D_MODEL // 4
FNET_G = 4
FNET_C = FNET_WIDTH // FNET_G
IN_WIDTH = 2 * QK_WIDTH + ATTN_WIDTH + SSM_WIDTH + FNET_WIDTH
D_FF = 4 * D_MODEL
ROPE_BASE = 10000.0
EPS = 1e-6
SCALE = HEAD_DIM ** -0.5
LOG2E = math.log2(math.e)

TILE = 256
LANES = 128
SUBLANES = 8
FF_CHUNK = 1024
MOD_BLOCK = 1536
FOURIER_CHUNK = 8192
FOURIER_GROUP = 4
S5_BLOCK = 8
S5_CHUNK_ROWS = 256
VMEM_LIMIT = 48 * 1024 * 1024
S5_VMEM_LIMIT = 56 * 1024 * 1024

def _const_spec(shape, layer=None):
    nd = len(shape)
    if layer is None:
        return pl.BlockSpec(shape, lambda *_: (0,) * nd, pipeline_mode=pl.Buffered(1))
    return pl.BlockSpec((1,) + tuple(shape), lambda *_: (layer,) + (0,) * nd,
                        pipeline_mode=pl.Buffered(1))


def _params(sem, vmem=None):
    return pltpu.CompilerParams(dimension_semantics=sem, vmem_limit_bytes=vmem or VMEM_LIMIT)


def _rows_to_blocked(tok, scr_ref, blk_ref, period):
    n, w = tok.shape
    for h in range(w // LANES):
        scr_ref[h] = tok[:, h * LANES:(h + 1) * LANES]
    for j in range(period):
        for h in range(w // LANES):
            lo = j * w + h * LANES
            blk_ref[:, lo:lo + LANES] = scr_ref[h, pl.ds(j, n // period, stride=period), :]


def _blocked_to_rows(blk, scr_ref, period):
    nh, n, _ = scr_ref.shape
    w = nh * LANES
    for j in range(period):
        for h in range(nh):
            lo = j * w + h * LANES
            scr_ref[h, pl.ds(j, n // period, stride=period), :] = blk[:, lo:lo + LANES]
    return jnp.concatenate([scr_ref[h] for h in range(nh)], axis=1)


def _mod_kernel(act_ref, w_ref, b_ref, o_ref):
    a = act_ref[...]
    a = a * jax.nn.sigmoid(a)
    o_ref[0] = jnp.dot(a.astype(BF16), w_ref[0].astype(BF16),
                       preferred_element_type=F32) + b_ref[0]


def _modulation(act, w_mod, b_mod):
    depth, d, n = w_mod.shape
    bn = MOD_BLOCK
    return pl.pallas_call(
        _mod_kernel,
        grid=(depth, n // bn),
        in_specs=[pl.BlockSpec((SUBLANES, d), lambda l, j: (0, 0)),
                  pl.BlockSpec((1, d, bn), lambda l, j: (l, 0, j)),
                  pl.BlockSpec((1, 1, bn), lambda l, j: (l, 0, j))],
        out_specs=pl.BlockSpec((1, SUBLANES, bn), lambda l, j: (l, 0, j)),
        out_shape=jax.ShapeDtypeStruct((depth, SUBLANES, n), F32),
        compiler_params=_params(("parallel", "parallel")),
        name="modulation",
    )(act, w_mod, b_mod.reshape(depth, 1, n))


def _inproj_kernel(x_ref, mod_ref, g1_ref, w_ref, gq_ref, gk_ref, e_ref, cos_ref, sin_ref,
                   q_ref, kt_ref, ve_ref, u_ref, u8_ref, f_ref, stage_ref):
    d = D_MODEL
    x = x_ref[0]
    ms = jnp.mean(x * x, axis=-1, keepdims=True)
    xn = x * lax.rsqrt(ms + EPS) * g1_ref[0]
    mod = mod_ref[0, 0]
    h = xn * (1.0 + mod[:, d:2 * d]) + mod[:, :d]
    proj = jnp.dot(h.astype(BF16), w_ref[0], preferred_element_type=F32)
    e = e_ref[...]
    cos = cos_ref[...]
    sin = sin_ref[...]

    def qk_norm(z, g):
        sq = z * z
        hi = sq.astype(BF16)
        lo = (sq - hi.astype(F32)).astype(BF16)
        parts = []
        for j in range(QK_WIDTH // 256):
            sl = slice(j * 256, (j + 1) * 256)
            parts.append(jnp.dot(hi[:, sl], e, preferred_element_type=F32)
                         + jnp.dot(lo[:, sl], e, preferred_element_type=F32))
        ssum = jnp.concatenate(parts, axis=-1)
        return z * lax.rsqrt(ssum * (1.0 / HEAD_DIM) + EPS) * g

    def rope(zh):
        return zh * cos + pltpu.roll(zh, LANES // 2, 1) * sin

    qn = qk_norm(proj[:, :QK_WIDTH], gq_ref[0])
    kn = qk_norm(proj[:, QK_WIDTH:2 * QK_WIDTH], gk_ref[0])
    row_map = (lax.broadcasted_iota(jnp.int32, (LANES, TILE), 0) // 32) % 2
    ones = jnp.ones((TILE, LANES), BF16)
    for hh in range(N_HEADS):
        sl = slice(hh * LANES, (hh + 1) * LANES)
        q_ref[0, :, sl] = (rope(qn[:, sl]) * (SCALE * LOG2E)).astype(BF16)
        kt = rope(kn[:, sl]).T
        kt_ref[0, hh, 0] = jnp.where(row_map == 0, kt, 0.0).astype(BF16)
        kt_ref[0, hh, 1] = jnp.where(row_map == 1, kt, 0.0).astype(BF16)
        vo = 2 * QK_WIDTH + hh * V_DIM
        ve_ref[0, hh, :, :V_DIM] = proj[:, vo:vo + V_DIM].astype(BF16)
        ve_ref[0, hh, :, V_DIM:] = ones
    uo = 2 * QK_WIDTH + ATTN_WIDTH
    u_ref[0] = proj[:, uo:uo + SSM_WIDTH]
    _rows_to_blocked(proj[:, uo:uo + SSM_WIDTH], stage_ref, u8_ref.at[0], S5_BLOCK)
    f_ref[0] = proj[:, uo + SSM_WIDTH:]


def _in_projection(layer, xs, mods, g_norm1, w_in, gq, gk, e_mat, cos_t, sin_t):
    b, r, d = xs.shape
    nt = r // TILE
    depth = w_in.shape[0]
    return pl.pallas_call(
        _inproj_kernel,
        grid=(b, nt),
        in_specs=[
            pl.BlockSpec((1, TILE, d), lambda i, t: (i, t, 0)),
            pl.BlockSpec((1, 1, 1, 6 * d),
                         lambda i, t: (layer, jnp.where(t == nt - 1, b, i), 0, 0)),
            _const_spec((1, d), layer),
            _const_spec((d, IN_WIDTH), layer),
            _const_spec((1, QK_WIDTH), layer),
            _const_spec((1, QK_WIDTH), layer),
            _const_spec((256, 256)),
            pl.BlockSpec((TILE, LANES), lambda i, t: (t, 0)),
            pl.BlockSpec((TILE, LANES), lambda i, t: (t, 0)),
        ],
        out_specs=[
            pl.BlockSpec((1, TILE, QK_WIDTH), lambda i, t: (i, t, 0)),
            pl.BlockSpec((1, N_HEADS, 2, LANES, TILE), lambda i, t: (i, 0, 0, 0, t)),
            pl.BlockSpec((1, N_HEADS, TILE, 2 * V_DIM), lambda i, t: (i, 0, t, 0)),
            pl.BlockSpec((1, TILE, SSM_WIDTH), lambda i, t: (i, t, 0)),
            pl.BlockSpec((1, TILE // S5_BLOCK, S5_BLOCK * SSM_WIDTH), lambda i, t: (i, t, 0)),
            pl.BlockSpec((1, TILE, FNET_WIDTH), lambda i, t: (i, t, 0)),
        ],
        out_shape=[
            jax.ShapeDtypeStruct((b, r, QK_WIDTH), BF16),
            jax.ShapeDtypeStruct((b, N_HEADS, 2, LANES, r), BF16),
            jax.ShapeDtypeStruct((b, N_HEADS, r, 2 * V_DIM), BF16),
            jax.ShapeDtypeStruct((b, r, SSM_WIDTH), F32),
            jax.ShapeDtypeStruct((b, r // S5_BLOCK, S5_BLOCK * SSM_WIDTH), F32),
            jax.ShapeDtypeStruct((b, r, FNET_WIDTH), F32),
        ],
        scratch_shapes=[pltpu.VMEM((SSM_WIDTH // LANES, TILE, LANES), F32)],
        compiler_params=_params(("parallel", "parallel")),
        name="in_projection",
    )(xs, mods.reshape(depth, SUBLANES, 1, 6 * d), g_norm1.reshape(depth, 1, d), w_in,
      gq, gk, e_mat, cos_t, sin_t)


def _attn_kernel(q_ref, kt_ref, ve_ref, lam_ref, gs_ref, o_ref, *, nt, ctx_len):
    t = pl.program_id(2)
    q = q_ref[0]
    r = ve_ref.shape[2]

    def attend(k0):
        outs = []
        ve = ve_ref[0, 0, k0:, :]
        scores = [jnp.dot(q, kt_ref[0, 0, mp, :, k0:], preferred_element_type=F32)
                  for mp in range(2)]
        for s in scores:
            p = jnp.exp2(s - jnp.max(s, axis=1, keepdims=True))
            acc = jnp.dot(p.astype(BF16), ve, preferred_element_type=F32)
            outs.append(acc[:, :V_DIM] / acc[:, V_DIM:])
        lp = lam_ref[0]
        s1 = jnp.sum(lp[0:1] * lp[1:2], axis=-1, keepdims=True)
        s2 = jnp.sum(lp[2:3] * lp[3:4], axis=-1, keepdims=True)
        lam = jnp.exp(s1) - jnp.exp(s2) + lp[4:5, 0:1]
        a = outs[0] - lam * outs[1]
        a = a * lax.rsqrt(jnp.mean(a * a, axis=-1, keepdims=True) + EPS)
        o_ref[0] = (a * gs_ref[0] * lp[4:5, 1:2]).astype(BF16)

    @pl.when(t < nt - 1)
    def _latent_queries():
        attend(0)

    @pl.when(t == nt - 1)
    def _context_queries():
        attend(r - ctx_len)


def _attention(layer, q, kt, ve, lam_p, g_subln):
    b, r, _ = q.shape
    nt = r // TILE
    return pl.pallas_call(
        functools.partial(_attn_kernel, nt=nt, ctx_len=TILE),
        grid=(b, N_HEADS, nt),
        in_specs=[
            pl.BlockSpec((1, TILE, LANES), lambda i, h, t: (i, t, h)),
            pl.BlockSpec((1, 1, 2, LANES, r), lambda i, h, t: (i, h, 0, 0, 0),
                         pipeline_mode=pl.Buffered(1)),
            pl.BlockSpec((1, 1, r, 2 * V_DIM), lambda i, h, t: (i, h, 0, 0),
                         pipeline_mode=pl.Buffered(1)),
            _const_spec((SUBLANES, LANES), layer),
            _const_spec((1, V_DIM), layer),
        ],
        out_specs=pl.BlockSpec((1, TILE, V_DIM), lambda i, h, t: (i, t, h)),
        out_shape=jax.ShapeDtypeStruct((b, r, ATTN_WIDTH), BF16),
        compiler_params=_params(("parallel", "parallel", "arbitrary")),
        name="diff_attention",
    )(q, kt, ve, lam_p, g_subln)


def _zoh(are_ref, aim_ref, ldt_ref):
    a_re = are_ref[0, 0]
    a_im = aim_ref[0, 0]
    dt = jnp.exp(ldt_ref[0, 0])
    mag = jnp.exp(dt * a_re)
    ang = dt * a_im
    ab_re = mag * jnp.cos(ang)
    ab_im = mag * jnp.sin(ang)
    den = a_re * a_re + a_im * a_im
    n_re = ab_re - 1.0
    f_re = (n_re * a_re + ab_im * a_im) / den
    f_im = (ab_im * a_re - n_re * a_im) / den
    return ab_re, ab_im, f_re, f_im


def _complex_powers(base_re, base_im, n):
    pows = [(jnp.ones_like(base_re), jnp.zeros_like(base_im))]
    for _ in range(n):
        pr, pi = pows[-1]
        pows.append((pr * base_re - pi * base_im, pr * base_im + pi * base_re))
    return pows


def _fill_scan_tables(pw_ref, base_re, base_im, reverse):
    shape = (SUBLANES, SSM_STATE)
    pows = _complex_powers(base_re, base_im, SUBLANES)
    row = lax.broadcasted_iota(jnp.int32, shape, 0)
    zero = jnp.zeros(shape, F32)
    for idx, k in enumerate((1, 2, 4)):
        mask = (row + k <= SUBLANES - 1) if reverse else (row >= k)
        pw_ref[2 * idx] = jnp.where(mask, jnp.broadcast_to(pows[k][0], shape), zero)
        pw_ref[2 * idx + 1] = jnp.where(mask, jnp.broadcast_to(pows[k][1], shape), zero)
    pcr = zero
    pci = zero
    for tt in range(SUBLANES):
        e = (SUBLANES - tt) if reverse else tt + 1
        pcr = jnp.where(row == tt, jnp.broadcast_to(pows[e][0], shape), pcr)
        pci = jnp.where(row == tt, jnp.broadcast_to(pows[e][1], shape), pci)
    pw_ref[6] = pcr
    pw_ref[7] = pci


def _scan_rows(s_ref, pw_ref, carry, nrows, reverse, exclusive, groups=None):
    ns = SSM_STATE
    shape = (SUBLANES, ns)
    ngroups = nrows // SUBLANES
    edge = (SUBLANES - 1) if reverse else 0

    def group(g, carry):
        cr, ci = carry
        gi = (ngroups - 1 - g) if reverse else g
        r0 = pl.multiple_of(gi * SUBLANES, SUBLANES)
        xr = s_ref[pl.ds(r0, SUBLANES), :ns]
        xi = s_ref[pl.ds(r0, SUBLANES), ns:]
        for idx, k in enumerate((1, 2, 4)):
            sh = (SUBLANES - k) if reverse else k
            sr = pltpu.roll(xr, sh, 0)
            si = pltpu.roll(xi, sh, 0)
            pr = pw_ref[2 * idx]
            pi = pw_ref[2 * idx + 1]
            xr, xi = xr + pr * sr - pi * si, xi + pr * si + pi * sr
        pr = pw_ref[6]
        pi = pw_ref[7]
        xr, xi = xr + pr * cr - pi * ci, xi + pr * ci + pi * cr
        if exclusive:
            row = lax.broadcasted_iota(jnp.int32, shape, 0)
            sh = (SUBLANES - 1) if reverse else 1
            er = jnp.where(row == edge, cr, pltpu.roll(xr, sh, 0))
            ei = jnp.where(row == edge, ci, pltpu.roll(xi, sh, 0))
        else:
            er, ei = xr, xi
        s_ref[pl.ds(r0, SUBLANES), :ns] = er
        s_ref[pl.ds(r0, SUBLANES), ns:] = ei
        last = 0 if reverse else SUBLANES - 1
        return (jnp.broadcast_to(xr[last:last + 1], shape),
                jnp.broadcast_to(xi[last:last + 1], shape))

    if groups is None:
        return lax.fori_loop(0, ngroups, group, carry)
    for g in groups:
        carry = group(g, carry)
    return carry


def _s5_ctx_kernel(u_ref, are_ref, aim_ref, ldt_ref, bre_ref, bim_ref, c_ref,
                   o_ref, st_ref, bbar_ref, pw_ref, s_ref, y_ref, *, reverse):
    ns = SSM_STATE
    ab_re, ab_im, f_re, f_im = _zoh(are_ref, aim_ref, ldt_ref)
    bre = bre_ref[0, 0]
    bim = bim_ref[0, 0]
    bbar_ref[:, :ns] = (f_re * bre - f_im * bim).astype(BF16)
    bbar_ref[:, ns:] = (f_re * bim + f_im * bre).astype(BF16)
    _fill_scan_tables(pw_ref, ab_re, ab_im, reverse)
    s_ref[...] = jnp.dot(u_ref[0].astype(BF16), bbar_ref[...], preferred_element_type=F32)
    zero = jnp.zeros((SUBLANES, ns), F32)
    cr, ci = _scan_rows(s_ref, pw_ref, (zero, zero), TILE, reverse, exclusive=False)
    st_ref[0, 0] = cr
    st_ref[0, 1] = ci
    y = jnp.dot(s_ref[...].astype(BF16), c_ref[0, 0], preferred_element_type=F32)
    _rows_to_blocked(y, y_ref, o_ref.at[0], S5_BLOCK)


def _s5_block_kernel(u8_ref, s0_ref, are_ref, aim_ref, ldt_ref, bre_ref, bim_ref, ctre_ref,
                     ctim_ref, y_ref, o_ref, wx_ref, wc_ref, wt_ref, pw_ref, carry_ref, s_ref,
                     *, reverse):
    del y_ref
    ns = SSM_STATE
    w = SSM_WIDTH
    nb = S5_BLOCK
    c = pl.program_id(0)
    b = pl.program_id(1)

    @pl.when((c == 0) & (b == 0))
    def _build_maps():
        ab_re, ab_im, f_re, f_im = _zoh(are_ref, aim_ref, ldt_ref)
        bre = bre_ref[0, 0]
        bim = bim_ref[0, 0]
        bb_re = f_re * bre - f_im * bim
        bb_im = f_re * bim + f_im * bre
        pows = _complex_powers(ab_re, ab_im, nb)
        ctre = ctre_ref[0, 0]
        ctim = ctim_ref[0, 0]
        ct_stack = jnp.concatenate([ctre, -ctim], axis=1).astype(BF16)
        taps = []
        for e in range(nb):
            pr, pi = pows[e]
            xr = pr * bb_re - pi * bb_im
            xi = pr * bb_im + pi * bb_re
            i = e if reverse else nb - 1 - e
            wx_ref[i * w:(i + 1) * w, :ns] = xr.astype(BF16)
            wx_ref[i * w:(i + 1) * w, ns:] = xi.astype(BF16)
            xk = jnp.concatenate([xr, xi], axis=1).astype(BF16)
            taps.append(lax.dot_general(xk, ct_stack, (((1,), (1,)), ((), ())),
                                        preferred_element_type=F32).astype(BF16))
        zero_blk = jnp.zeros((w, w), BF16)
        for i in range(nb):
            for j in range(nb):
                lag = (i - j) if reverse else (j - i)
                wt_ref[i * w:(i + 1) * w, j * w:(j + 1) * w] = taps[lag] if lag >= 0 else zero_blk
        for j in range(nb):
            pr, pi = pows[nb - j] if reverse else pows[j + 1]
            wc_ref[:ns, j * w:(j + 1) * w] = (ctre * pr - ctim * pi).T.astype(BF16)
            wc_ref[ns:, j * w:(j + 1) * w] = (-(ctre * pi + ctim * pr)).T.astype(BF16)
        _fill_scan_tables(pw_ref, pows[nb][0], pows[nb][1], reverse)

    @pl.when(c == 0)
    def _load_state():
        carry_ref[b] = s0_ref[0]

    u8 = u8_ref[0].astype(BF16)
    s_ref[...] = jnp.dot(u8, wx_ref[...], preferred_element_type=F32)
    carry = (carry_ref[b, 0], carry_ref[b, 1])
    ngroups = S5_CHUNK_ROWS // SUBLANES
    per = ngroups // nb
    for j in range(nb):
        o_ref[0, :, j * w:(j + 1) * w] = jnp.dot(u8, wt_ref[:, j * w:(j + 1) * w],
                                                 preferred_element_type=F32)
        carry = _scan_rows(s_ref, pw_ref, carry, S5_CHUNK_ROWS, reverse, exclusive=True,
                           groups=range(j * per, (j + 1) * per))
    carry_ref[b, 0] = carry[0]
    carry_ref[b, 1] = carry[1]
    o_ref[0] = o_ref[0] + jnp.dot(s_ref[...].astype(BF16), wc_ref[...],
                                  preferred_element_type=F32)


def _s5_direction(layer, direction, u, u8, a_re, a_im, ldt, b_re, b_im, ct_re, ct_im, c_stack):
    b, r, w = u.shape
    nt = r // TILE
    rows = r // S5_BLOCK
    wide = S5_BLOCK * w
    tile_rows = TILE // S5_BLOCK
    ns = SSM_STATE
    reverse = direction == 1

    def dir_spec(shape):
        nd = len(shape)
        return pl.BlockSpec((1, 1) + tuple(shape), lambda *_: (layer, direction) + (0,) * nd,
                            pipeline_mode=pl.Buffered(1))

    y_ctx, state = pl.pallas_call(
        functools.partial(_s5_ctx_kernel, reverse=reverse),
        grid=(b,),
        in_specs=[pl.BlockSpec((1, TILE, w), lambda i: (i, nt - 1, 0)),
                  dir_spec((1, ns)), dir_spec((1, ns)), dir_spec((1, ns)),
                  dir_spec((w, ns)), dir_spec((w, ns)), dir_spec((2 * ns, w))],
        out_specs=[pl.BlockSpec((1, tile_rows, wide), lambda i: (i, nt - 1, 0)),
                   pl.BlockSpec((1, 2, SUBLANES, ns), lambda i: (i, 0, 0, 0))],
        out_shape=[jax.ShapeDtypeStruct((b, rows, wide), F32),
                   jax.ShapeDtypeStruct((b, 2, SUBLANES, ns), F32)],
        scratch_shapes=[pltpu.VMEM((w, 2 * ns), BF16),
                        pltpu.VMEM((8, SUBLANES, ns), F32),
                        pltpu.VMEM((TILE, 2 * ns), F32),
                        pltpu.VMEM((w // LANES, TILE, LANES), F32)],
        compiler_params=_params(("parallel",)),
        name="s5_context",
    )(u, a_re, a_im, ldt, b_re, b_im, c_stack)

    nchunks = (r - TILE) // (S5_BLOCK * S5_CHUNK_ROWS)

    def chunk(c, i):
        return (i, nchunks - 1 - c if reverse else c, 0)

    return pl.pallas_call(
        functools.partial(_s5_block_kernel, reverse=reverse),
        grid=(nchunks, b),
        in_specs=[pl.BlockSpec((1, S5_CHUNK_ROWS, wide), chunk),
                  pl.BlockSpec((1, 2, SUBLANES, ns), lambda c, i: (i, 0, 0, 0)),
                  dir_spec((1, ns)), dir_spec((1, ns)), dir_spec((1, ns)),
                  dir_spec((w, ns)), dir_spec((w, ns)), dir_spec((w, ns)), dir_spec((w, ns)),
                  pl.BlockSpec(memory_space=pl.ANY)],
        out_specs=pl.BlockSpec((1, S5_CHUNK_ROWS, wide), chunk),
        out_shape=jax.ShapeDtypeStruct((b, rows, wide), F32),
        scratch_shapes=[pltpu.VMEM((wide, 2 * ns), BF16),
                        pltpu.VMEM((2 * ns, wide), BF16),
                        pltpu.VMEM((wide, wide), BF16),
                        pltpu.VMEM((8, SUBLANES, ns), F32),
                        pltpu.VMEM((b, 2, SUBLANES, ns), F32),
                        pltpu.VMEM((S5_CHUNK_ROWS, 2 * ns), F32)],
        input_output_aliases={9: 0},
        compiler_params=_params(("arbitrary", "arbitrary"), vmem=S5_VMEM_LIMIT),
        name="s5_blocks_reverse" if reverse else "s5_blocks_forward",
    )(u8, state, a_re, a_im, ldt, b_re, b_im, ct_re, ct_im, y_ctx)


def _fourier_outer_kernel(w_ref, x_ref, g_ref):
    g_ref[0] = jnp.dot(w_ref[...], x_ref[0].astype(BF16),
                       preferred_element_type=F32).astype(BF16)


def _fourier_outer(f, w_outer):
    b, r, w = f.shape
    nt = r // TILE
    l2n = nt - 1
    flat = TILE * w
    return pl.pallas_call(
        _fourier_outer_kernel,
        grid=(b, flat // FOURIER_CHUNK),
        in_specs=[_const_spec((2 * l2n, l2n)),
                  pl.BlockSpec((1, l2n, FOURIER_CHUNK), lambda i, j: (i, 0, j))],
        out_specs=pl.BlockSpec((1, 2 * l2n, FOURIER_CHUNK), lambda i, j: (i, 0, j)),
        out_shape=jax.ShapeDtypeStruct((b, 2 * l2n, flat), BF16),
        compiler_params=_params(("parallel", "parallel")),
        name="fourier_outer",
    )(w_outer, f.reshape(b, nt, flat))


def _fourier_tail(y_re, y_im, c64_ref, wf_ref):
    w = FNET_WIDTH
    z = (jnp.dot(y_re.astype(BF16), c64_ref[:w], preferred_element_type=F32)
         + jnp.dot(y_im.astype(BF16), c64_ref[w:], preferred_element_type=F32))
    return jnp.dot(z.astype(BF16), wf_ref[0], preferred_element_type=F32)


def _fourier_inner_kernel(gr_ref, gi_ref, m_ref, c64_ref, wf_ref, o_ref):
    w = FNET_WIDTH
    ys = [jnp.dot(m_ref[i, :, :TILE], gr_ref[0, i], preferred_element_type=F32)
          + jnp.dot(m_ref[i, :, TILE:], gi_ref[0, i], preferred_element_type=F32)
          for i in range(FOURIER_GROUP)]
    for i, y in enumerate(ys):
        o_ref[0, :, i * w:(i + 1) * w] = _fourier_tail(y[:TILE], y[TILE:], c64_ref, wf_ref)


def _fourier_inner(layer, g, m_tab, c64_tab, wf_blk, r):
    b, two_l2n, flat = g.shape
    l2n = two_l2n // 2
    w = flat // TILE
    steps = l2n // FOURIER_GROUP
    g4 = g.reshape(b, two_l2n, TILE, w)
    return pl.pallas_call(
        _fourier_inner_kernel,
        grid=(b, steps),
        in_specs=[
            pl.BlockSpec((1, FOURIER_GROUP, TILE, w), lambda i, j: (i, j, 0, 0)),
            pl.BlockSpec((1, FOURIER_GROUP, TILE, w), lambda i, j: (i, steps + j, 0, 0)),
            pl.BlockSpec((FOURIER_GROUP, 2 * TILE, 2 * TILE), lambda i, j: (j, 0, 0)),
            _const_spec((2 * w, w)),
            _const_spec((w, w), layer),
        ],
        out_specs=pl.BlockSpec((1, TILE, FOURIER_GROUP * w), lambda i, j: (i, 0, j)),
        out_shape=jax.ShapeDtypeStruct((b, r // l2n, l2n * w), F32),
        compiler_params=_params(("parallel", "parallel")),
        name="fourier_inner",
    )(g4, g4, m_tab, c64_tab, wf_blk)


def _fourier_ctx_kernel(x_ref, m_ref, c64_ref, wf_ref, fn_ref, o_ref, z_ref, *, l2n):
    del fn_ref
    y = jnp.dot(m_ref[...], x_ref[0].astype(BF16), preferred_element_type=F32)
    z = _fourier_tail(y[:TILE], y[TILE:], c64_ref, wf_ref)
    _rows_to_blocked(z, z_ref, o_ref.at[0], l2n)


def _fourier_context(layer, f, fn, m_ctx, c64_tab, wf_blk):
    b, r, w = f.shape
    nt = r // TILE
    l2n = nt - 1
    return pl.pallas_call(
        functools.partial(_fourier_ctx_kernel, l2n=l2n),
        grid=(b,),
        in_specs=[
            pl.BlockSpec((1, TILE, w), lambda i: (i, nt - 1, 0)),
            _const_spec((2 * TILE, TILE)),
            _const_spec((2 * w, w)),
            _const_spec((w, w), layer),
            pl.BlockSpec(memory_space=pl.ANY),
        ],
        out_specs=pl.BlockSpec((1, TILE // l2n, l2n * w), lambda i: (i, nt - 1, 0)),
        out_shape=jax.ShapeDtypeStruct((b, r // l2n, l2n * w), F32),
        scratch_shapes=[pltpu.VMEM((w // LANES, TILE, LANES), F32)],
        input_output_aliases={4: 0},
        compiler_params=_params(("parallel",)),
        name="fourier_context",
    )(f, m_ctx, c64_tab, wf_blk, fn)


def _outffn_kernel(x_ref, a_ref, yf_ref, yb_ref, u_ref, fn_ref, mod_ref, dsk_ref, wglu_ref,
                   bglu_ref, wo_ref, g2_ref, w1_ref, w2_ref, o_ref, ytok_ref, ftok_ref, *, l2n):
    d = D_MODEL
    x = x_ref[0]
    mod = mod_ref[0, 0]
    o1 = ATTN_WIDTH
    o2 = ATTN_WIDTH + SSM_WIDTH
    y_scan = _blocked_to_rows(yf_ref[0] + yb_ref[0], ytok_ref, S5_BLOCK)
    fnet = _blocked_to_rows(fn_ref[0], ftok_ref, l2n)
    hg = jax.nn.gelu(y_scan + dsk_ref[0] * u_ref[0])
    zg = jnp.dot(hg.astype(BF16), wglu_ref[0], preferred_element_type=F32) + bglu_ref[0]
    s5 = (hg * jax.nn.sigmoid(zg)).astype(BF16)
    mix = (jnp.dot(a_ref[0], wo_ref[0, :o1], preferred_element_type=F32)
           + jnp.dot(s5, wo_ref[0, o1:o2], preferred_element_type=F32)
           + jnp.dot(fnet.astype(BF16), wo_ref[0, o2:], preferred_element_type=F32))
    x1 = x + mod[:, 2 * d:3 * d] * mix
    ms = jnp.mean(x1 * x1, axis=-1, keepdims=True)
    hn = x1 * lax.rsqrt(ms + EPS) * g2_ref[0]
    h = (hn * (1.0 + mod[:, 4 * d:5 * d]) + mod[:, 3 * d:4 * d]).astype(BF16)
    acc = jnp.zeros((TILE, d), F32)
    for cc in range(D_FF // FF_CHUNK):
        sl = slice(cc * FF_CHUNK, (cc + 1) * FF_CHUNK)
        t = jnp.dot(h, w1_ref[0, :, sl], preferred_element_type=F32)
        t = jnp.square(jnp.maximum(t, 0.0))
        acc = acc + jnp.dot(t.astype(BF16), w2_ref[0, sl, :], preferred_element_type=F32)
    o_ref[0] = x1 + mod[:, 5 * d:] * acc


def _out_ffn(layer, xs, attn, y_fwd, y_bwd, u, fn, mods, d_skip, w_glu, b_glu, w_out, g_norm2,
             w_ff1, w_ff2, latent_only):
    b, r, d = xs.shape
    nt = r // TILE
    l2n = nt - 1
    depth = w_out.shape[0]
    s5_tile = pl.BlockSpec((1, TILE // S5_BLOCK, S5_BLOCK * SSM_WIDTH), lambda i, t: (i, t, 0))
    fn_tile = pl.BlockSpec((1, TILE // l2n, l2n * FNET_WIDTH), lambda i, t: (i, t, 0))

    def tile(width):
        return pl.BlockSpec((1, TILE, width), lambda i, t: (i, t, 0))

    return pl.pallas_call(
        functools.partial(_outffn_kernel, l2n=l2n),
        grid=(b, nt - 1 if latent_only else nt),
        in_specs=[
            tile(d), tile(ATTN_WIDTH), s5_tile, s5_tile, tile(SSM_WIDTH), fn_tile,
            pl.BlockSpec((1, 1, 1, 6 * d),
                         lambda i, t: (layer, jnp.where(t == nt - 1, b, i), 0, 0)),
            _const_spec((1, SSM_WIDTH), layer),
            _const_spec((SSM_WIDTH, SSM_WIDTH), layer),
            _const_spec((1, SSM_WIDTH), layer),
            _const_spec((d, d), layer),
            _const_spec((1, d), layer),
            _const_spec((d, D_FF), layer),
            _const_spec((D_FF, d), layer),
        ],
        out_specs=tile(d),
        out_shape=jax.ShapeDtypeStruct((b, r - TILE if latent_only else r, d), F32),
        scratch_shapes=[pltpu.VMEM((SSM_WIDTH // LANES, TILE, LANES), F32),
                        pltpu.VMEM((FNET_WIDTH // LANES, TILE, LANES), F32)],
        input_output_aliases={} if latent_only else {0: 0},
        compiler_params=_params(("parallel", "parallel")),
        name="out_ffn",
    )(xs, attn, y_fwd, y_bwd, u, fn, mods.reshape(depth, SUBLANES, 1, 6 * d), d_skip, w_glu, b_glu,
      w_out, g_norm2.reshape(depth, 1, d), w_ff1, w_ff2)


def _head_lane_tables():
    j = np.arange(LANES)
    half = j // 64
    mp = (j // 32) % 2
    idx = j % 32
    src_in_head = mp * HEAD_DIM + half * 32 + idx
    gain_idx = half * 32 + idx
    return src_in_head, gain_idx, mp


def _in_proj_column_order():
    src_in_head, _, _ = _head_lane_tables()
    qk = np.concatenate([h * LANES + src_in_head for h in range(N_HEADS)])
    return np.concatenate([qk, QK_WIDTH + qk, np.arange(2 * QK_WIDTH, IN_WIDTH)])


def _same_map_matrix():
    i = np.arange(256)
    head = i // LANES
    mp = ((i % LANES) // 32) % 2
    same = (head[:, None] == head[None, :]) & (mp[:, None] == mp[None, :])
    return same.astype(np.float32)


def _rope_tables(seq, ctx_len):
    t = jnp.arange(seq)
    row = (t // GRID_W).astype(F32)
    col = (t % GRID_W).astype(F32)
    n_freq = HEAD_DIM // 4
    inv = jnp.power(ROPE_BASE, -jnp.arange(n_freq, dtype=F32) / n_freq)
    ang = jnp.concatenate([row[:, None] * inv, col[:, None] * inv], axis=-1)
    cos = jnp.tile(jnp.cos(ang), (1, 4))
    sign = np.where(np.arange(LANES) < LANES // 2, -1.0, 1.0).astype(np.float32)
    sin = jnp.tile(jnp.sin(ang), (1, 4)) * sign
    cos = jnp.concatenate([cos, jnp.ones((ctx_len, LANES), F32)], axis=0)
    sin = jnp.concatenate([sin, jnp.zeros((ctx_len, LANES), F32)], axis=0)
    return cos, sin


def _dft_tables(length):
    l2n = length // TILE
    k2 = np.arange(l2n)
    phi = 2.0 * np.pi * ((k2[:, None] * k2[None, :]) % l2n) / l2n
    w_outer = np.concatenate([np.cos(phi), -np.sin(phi)], axis=0).astype(np.float32)
    l1 = np.arange(TILE)
    alpha = 2.0 * np.pi * ((k2[:, None] * l1[None, :]) % length) / length
    beta = 2.0 * np.pi * ((l1[:, None] * l1[None, :]) % TILE) / TILE
    ca = jnp.asarray(np.cos(alpha).astype(np.float32))[:, None, :]
    sa = jnp.asarray(np.sin(alpha).astype(np.float32))[:, None, :]
    cb = jnp.asarray(np.cos(beta).astype(np.float32))[None]
    sb = jnp.asarray(np.sin(beta).astype(np.float32))[None]
    norm = 1.0 / math.sqrt(length)
    ct = (ca * cb - sa * sb) * norm
    st = (sa * cb + ca * sb) * norm
    m = jnp.concatenate([jnp.concatenate([ct, st], axis=2),
                         jnp.concatenate([-st, ct], axis=2)], axis=1).astype(BF16)
    return jnp.asarray(w_outer).astype(BF16), m


def _tile_dft_table():
    l1 = np.arange(TILE)
    beta = 2.0 * np.pi * ((l1[:, None] * l1[None, :]) % TILE) / TILE
    tab = np.concatenate([np.cos(beta), -np.sin(beta)], axis=0) / math.sqrt(TILE)
    return jnp.asarray(tab.astype(np.float32)).astype(BF16)


def _channel_dft_table():
    c = np.arange(FNET_C)
    th = 2.0 * np.pi * ((c[:, None] * c[None, :]) % FNET_C) / FNET_C
    eye = np.eye(FNET_G)
    norm = 1.0 / math.sqrt(FNET_C)
    cblk = np.kron(eye, np.cos(th)) * norm
    sblk = np.kron(eye, np.sin(th)) * norm
    return jnp.asarray(np.concatenate([cblk, sblk], axis=0).astype(np.float32)).astype(BF16)


def _block_diag(blocks):
    g = blocks.shape[-3]
    eye = jnp.eye(g, dtype=blocks.dtype)
    out = jnp.einsum('...gab,gh->...gahb', blocks, eye)
    return out.reshape(blocks.shape[:-3] + (g * blocks.shape[-2], g * blocks.shape[-1]))


def kernel(x, c, ctx, c_ctx, w_mod, b_mod, g_norm1, w_in, g_qnorm, g_knorm, lam_q1, lam_k1, lam_q2, lam_k2, g_subln, ssm_a_re, ssm_a_im, ssm_log_dt, ssm_b_re, ssm_b_im, ssm_c_re, ssm_c_im, ssm_d, w_glu, b_glu, w_fnet, w_out, g_norm2, w_ff1, w_ff2):
    bsz, seq, d = x.shape
    ctx_len = ctx.shape[1]
    depth = w_mod.shape[0]
    assert d == D_MODEL and ctx_len == TILE and seq % TILE == 0 and seq % GRID_W == 0
    assert bsz + 1 <= SUBLANES

    r_tot = seq + ctx_len
    assert (seq // TILE) % FOURIER_GROUP == 0 and seq % (S5_BLOCK * S5_CHUNK_ROWS) == 0
    xs = jnp.concatenate([x, ctx], axis=1)
    act = jnp.concatenate([c, c_ctx[None], jnp.zeros((SUBLANES - bsz - 1, d), F32)], axis=0)
    mods = _modulation(act, w_mod, b_mod)

    w_in_p = jnp.take(w_in, jnp.asarray(_in_proj_column_order()), axis=2).astype(BF16)
    _, gain_idx, _ = _head_lane_tables()
    gq = jnp.tile(g_qnorm[:, gain_idx], (1, N_HEADS)).reshape(depth, 1, QK_WIDTH)
    gk = jnp.tile(g_knorm[:, gain_idx], (1, N_HEADS)).reshape(depth, 1, QK_WIDTH)
    e_mat = jnp.asarray(_same_map_matrix()).astype(BF16)
    cos_t, sin_t = _rope_tables(seq, ctx_len)
    lam_init = np.array([0.8 - 0.6 * math.exp(-0.3 * i) for i in range(depth)], np.float32)
    pad = jnp.zeros((depth, LANES - HEAD_DIM), F32)
    lam_rows = [jnp.concatenate([v, pad], axis=1) for v in (lam_q1, lam_k1, lam_q2, lam_k2)]
    const_row = np.zeros((depth, LANES), np.float32)
    const_row[:, 0] = lam_init
    const_row[:, 1] = 1.0 - lam_init
    lam_p = jnp.stack(lam_rows + [jnp.asarray(const_row)]
                      + [jnp.zeros((depth, LANES), F32)] * (SUBLANES - 5), axis=1)
    g_sub = g_subln.reshape(depth, 1, V_DIM)

    a_re = ssm_a_re.reshape(depth, 2, 1, SSM_STATE)
    a_im = ssm_a_im.reshape(depth, 2, 1, SSM_STATE)
    ldt = jnp.repeat(ssm_log_dt, SSM_N, axis=-1).reshape(depth, 2, 1, SSM_STATE)
    b_re = _block_diag(jnp.swapaxes(ssm_b_re, -1, -2))
    b_im = _block_diag(jnp.swapaxes(ssm_b_im, -1, -2))
    c_stack = jnp.concatenate([_block_diag(jnp.swapaxes(ssm_c_re, -1, -2)),
                               -_block_diag(jnp.swapaxes(ssm_c_im, -1, -2))],
                              axis=2).astype(BF16)
    ct_re = _block_diag(ssm_c_re)
    ct_im = _block_diag(ssm_c_im)
    d_skip = ssm_d.reshape(depth, 1, SSM_WIDTH)
    w_glu_b = w_glu.astype(BF16)
    b_glu_r = b_glu.reshape(depth, 1, SSM_WIDTH)

    w_outer, m_lat = _dft_tables(seq)
    m_ctx = _tile_dft_table()
    c64_tab = _channel_dft_table()
    wf_blk = _block_diag(w_fnet).astype(BF16)

    w_out_b = w_out.astype(BF16)
    w_ff1_b = w_ff1.astype(BF16)
    w_ff2_b = w_ff2.astype(BF16)

    for layer in range(depth):
        q, kt, ve, u, u8, f = _in_projection(layer, xs, mods, g_norm1, w_in_p, gq, gk, e_mat,
                                         cos_t, sin_t)
        attn = _attention(layer, q, kt, ve, lam_p, g_sub)
        y_fwd, y_bwd = [_s5_direction(layer, dr, u, u8, a_re, a_im, ldt, b_re, b_im, ct_re,
                                      ct_im, c_stack) for dr in range(2)]
        fn = _fourier_inner(layer, _fourier_outer(f, w_outer), m_lat, c64_tab, wf_blk, r_tot)
        last = layer == depth - 1
        if not last:
            fn = _fourier_context(layer, f, fn, m_ctx, c64_tab, wf_blk)
        xs = _out_ffn(layer, xs, attn, y_fwd, y_bwd, u, fn, mods, d_skip, w_glu_b, b_glu_r,
                      w_out_b, g_norm2, w_ff1_b, w_ff2_b, latent_only=last)
    return xs
```

```python
import functools
import math

import numpy as np
import jax
import jax.numpy as jnp
from jax import lax
from jax.experimental import pallas as pl
from jax.experimental.pallas import tpu as pltpu

F32 = jnp.float32
BF16 = jnp.bfloat16

D_MODEL = 1024
GRID_W = 64
N_HEADS = 4
HEAD_DIM = 64
V_DIM = 2 * HEAD_DIM
QK_WIDTH = N_HEADS * 2 * HEAD_DIM
ATTN_WIDTH = N_HEADS * V_DIM
SSM_WIDTH = D_MODEL // 4
SSM_P = 16
SSM_G = SSM_WIDTH // SSM_P
SSM_N = 64
SSM_STATE = SSM_G * SSM_N
FNET_WIDTH = D_MODEL // 4
FNET_G = 4
FNET_C = FNET_WIDTH // FNET_G
IN_WIDTH = 2 * QK_WIDTH + ATTN_WIDTH + SSM_WIDTH + FNET_WIDTH
D_FF = 4 * D_MODEL
ROPE_BASE = 10000.0
EPS = 1e-6
SCALE = HEAD_DIM ** -0.5
LOG2E = math.log2(math.e)

TILE = 256
LANES = 128
SUBLANES = 8
FF_CHUNK = 1024
MOD_BLOCK = 1536
FOURIER_CHUNK = 8192
FOURIER_GROUP = 4
S5_BLOCK = 8
S5_CHUNK_ROWS = 256
VMEM_LIMIT = 48 * 1024 * 1024
S5_VMEM_LIMIT = 56 * 1024 * 1024

def _const_spec(shape, layer=None):
    nd = len(shape)
    if layer is None:
        return pl.BlockSpec(shape, lambda *_: (0,) * nd, pipeline_mode=pl.Buffered(1))
    return pl.BlockSpec((1,) + tuple(shape), lambda *_: (layer,) + (0,) * nd,
                        pipeline_mode=pl.Buffered(1))


def _params(sem, vmem=None):
    return pltpu.CompilerParams(dimension_semantics=sem, vmem_limit_bytes=vmem or VMEM_LIMIT)


def _rows_to_blocked(tok, scr_ref, blk_ref, period):
    n, w = tok.shape
    for h in range(w // LANES):
        scr_ref[h] = tok[:, h * LANES:(h + 1) * LANES]
    for j in range(period):
        for h in range(w // LANES):
            lo = j * w + h * LANES
            blk_ref[:, lo:lo + LANES] = scr_ref[h, pl.ds(j, n // period, stride=period), :]


def _blocked_to_rows(blk, scr_ref, period):
    nh, n, _ = scr_ref.shape
    w = nh * LANES
    for j in range(period):
        for h in range(nh):
            lo = j * w + h * LANES
            scr_ref[h, pl.ds(j, n // period, stride=period), :] = blk[:, lo:lo + LANES]
    return jnp.concatenate([scr_ref[h] for h in range(nh)], axis=1)


def _mod_kernel(act_ref, w_ref, b_ref, o_ref):
    a = act_ref[...]
    a = a * jax.nn.sigmoid(a)
    o_ref[0] = jnp.dot(a.astype(BF16), w_ref[0].astype(BF16),
                       preferred_element_type=F32) + b_ref[0]


def _modulation(act, w_mod, b_mod):
    depth, d, n = w_mod.shape
    bn = MOD_BLOCK
    return pl.pallas_call(
        _mod_kernel,
        grid=(depth, n // bn),
        in_specs=[pl.BlockSpec((SUBLANES, d), lambda l, j: (0, 0)),
                  pl.BlockSpec((1, d, bn), lambda l, j: (l, 0, j)),
                  pl.BlockSpec((1, 1, bn), lambda l, j: (l, 0, j))],
        out_specs=pl.BlockSpec((1, SUBLANES, bn), lambda l, j: (l, 0, j)),
        out_shape=jax.ShapeDtypeStruct((depth, SUBLANES, n), F32),
        compiler_params=_params(("parallel", "parallel")),
        name="modulation",
    )(act, w_mod, b_mod.reshape(depth, 1, n))


def _inproj_kernel(x_ref, mod_ref, g1_ref, w_ref, gq_ref, gk_ref, e_ref, cos_ref, sin_ref,
                   q_ref, kt_ref, ve_ref, u_ref, u8_ref, f_ref, stage_ref):
    d = D_MODEL
    x = x_ref[0]
    ms = jnp.mean(x * x, axis=-1, keepdims=True)
    xn = x * lax.rsqrt(ms + EPS) * g1_ref[0]
    mod = mod_ref[0, 0]
    h = xn * (1.0 + mod[:, d:2 * d]) + mod[:, :d]
    proj = jnp.dot(h.astype(BF16), w_ref[0], preferred_element_type=F32)
    e = e_ref[...]
    cos = cos_ref[...]
    sin = sin_ref[...]

    def qk_norm(z, g):
        sq = z * z
        hi = sq.astype(BF16)
        lo = (sq - hi.astype(F32)).astype(BF16)
        parts = []
        for j in range(QK_WIDTH // 256):
            sl = slice(j * 256, (j + 1) * 256)
            parts.append(jnp.dot(hi[:, sl], e, preferred_element_type=F32)
                         + jnp.dot(lo[:, sl], e, preferred_element_type=F32))
        ssum = jnp.concatenate(parts, axis=-1)
        return z * lax.rsqrt(ssum * (1.0 / HEAD_DIM) + EPS) * g

    def rope(zh):
        return zh * cos + pltpu.roll(zh, LANES // 2, 1) * sin

    qn = qk_norm(proj[:, :QK_WIDTH], gq_ref[0])
    kn = qk_norm(proj[:, QK_WIDTH:2 * QK_WIDTH], gk_ref[0])
    row_map = (lax.broadcasted_iota(jnp.int32, (LANES, TILE), 0) // 32) % 2
    ones = jnp.ones((TILE, LANES), BF16)
    for hh in range(N_HEADS):
        sl = slice(hh * LANES, (hh + 1) * LANES)
        q_ref[0, :, sl] = (rope(qn[:, sl]) * (SCALE * LOG2E)).astype(BF16)
        kt = rope(kn[:, sl]).T
        kt_ref[0, hh, 0] = jnp.where(row_map == 0, kt, 0.0).astype(BF16)
        kt_ref[0, hh, 1] = jnp.where(row_map == 1, kt, 0.0).astype(BF16)
        vo = 2 * QK_WIDTH + hh * V_DIM
        ve_ref[0, hh, :, :V_DIM] = proj[:, vo:vo + V_DIM].astype(BF16)
        ve_ref[0, hh, :, V_DIM:] = ones
    uo = 2 * QK_WIDTH + ATTN_WIDTH
    u_ref[0] = proj[:, uo:uo + SSM_WIDTH]
    _rows_to_blocked(proj[:, uo:uo + SSM_WIDTH], stage_ref, u8_ref.at[0], S5_BLOCK)
    f_ref[0] = proj[:, uo + SSM_WIDTH:]


def _in_projection(layer, xs, mods, g_norm1, w_in, gq, gk, e_mat, cos_t, sin_t):
    b, r, d = xs.shape
    nt = r // TILE
    depth = w_in.shape[0]
    return pl.pallas_call(
        _inproj_kernel,
        grid=(b, nt),
        in_specs=[
            pl.BlockSpec((1, TILE, d), lambda i, t: (i, t, 0)),
            pl.BlockSpec((1, 1, 1, 6 * d),
                         lambda i, t: (layer, jnp.where(t == nt - 1, b, i), 0, 0)),
            _const_spec((1, d), layer),
            _const_spec((d, IN_WIDTH), layer),
            _const_spec((1, QK_WIDTH), layer),
            _const_spec((1, QK_WIDTH), layer),
            _const_spec((256, 256)),
            pl.BlockSpec((TILE, LANES), lambda i, t: (t, 0)),
            pl.BlockSpec((TILE, LANES), lambda i, t: (t, 0)),
        ],
        out_specs=[
            pl.BlockSpec((1, TILE, QK_WIDTH), lambda i, t: (i, t, 0)),
            pl.BlockSpec((1, N_HEADS, 2, LANES, TILE), lambda i, t: (i, 0, 0, 0, t)),
            pl.BlockSpec((1, N_HEADS, TILE, 2 * V_DIM), lambda i, t: (i, 0, t, 0)),
            pl.BlockSpec((1, TILE, SSM_WIDTH), lambda i, t: (i, t, 0)),
            pl.BlockSpec((1, TILE // S5_BLOCK, S5_BLOCK * SSM_WIDTH), lambda i, t: (i, t, 0)),
            pl.BlockSpec((1, TILE, FNET_WIDTH), lambda i, t: (i, t, 0)),
        ],
        out_shape=[
            jax.ShapeDtypeStruct((b, r, QK_WIDTH), BF16),
            jax.ShapeDtypeStruct((b, N_HEADS, 2, LANES, r), BF16),
            jax.ShapeDtypeStruct((b, N_HEADS, r, 2 * V_DIM), BF16),
            jax.ShapeDtypeStruct((b, r, SSM_WIDTH), F32),
            jax.ShapeDtypeStruct((b, r // S5_BLOCK, S5_BLOCK * SSM_WIDTH), F32),
            jax.ShapeDtypeStruct((b, r, FNET_WIDTH), F32),
        ],
        scratch_shapes=[pltpu.VMEM((SSM_WIDTH // LANES, TILE, LANES), F32)],
        compiler_params=_params(("parallel", "parallel")),
        name="in_projection",
    )(xs, mods.reshape(depth, SUBLANES, 1, 6 * d), g_norm1.reshape(depth, 1, d), w_in,
      gq, gk, e_mat, cos_t, sin_t)


def _attn_kernel(q_ref, kt_ref, ve_ref, lam_ref, gs_ref, o_ref, *, nt, ctx_len):
    t = pl.program_id(2)
    q = q_ref[0]
    r = ve_ref.shape[2]

    def attend(k0):
        outs = []
        ve = ve_ref[0, 0, k0:, :]
        scores = [jnp.dot(q, kt_ref[0, 0, mp, :, k0:], preferred_element_type=F32)
                  for mp in range(2)]
        for s in scores:
            p = jnp.exp2(s - jnp.max(s, axis=1, keepdims=True))
            acc = jnp.dot(p.astype(BF16), ve, preferred_element_type=F32)
            outs.append(acc[:, :V_DIM] / acc[:, V_DIM:])
        lp = lam_ref[0]
        s1 = jnp.sum(lp[0:1] * lp[1:2], axis=-1, keepdims=True)
        s2 = jnp.sum(lp[2:3] * lp[3:4], axis=-1, keepdims=True)
        lam = jnp.exp(s1) - jnp.exp(s2) + lp[4:5, 0:1]
        a = outs[0] - lam * outs[1]
        a = a * lax.rsqrt(jnp.mean(a * a, axis=-1, keepdims=True) + EPS)
        o_ref[0] = (a * gs_ref[0] * lp[4:5, 1:2]).astype(BF16)

    @pl.when(t < nt - 1)
    def _latent_queries():
        attend(0)

    @pl.when(t == nt - 1)
    def _context_queries():
        attend(r - ctx_len)


def _attention(layer, q, kt, ve, lam_p, g_subln):
    b, r, _ = q.shape
    nt = r // TILE
    return pl.pallas_call(
        functools.partial(_attn_kernel, nt=nt, ctx_len=TILE),
        grid=(b, N_HEADS, nt),
        in_specs=[
            pl.BlockSpec((1, TILE, LANES), lambda i, h, t: (i, t, h)),
            pl.BlockSpec((1, 1, 2, LANES, r), lambda i, h, t: (i, h, 0, 0, 0),
                         pipeline_mode=pl.Buffered(1)),
            pl.BlockSpec((1, 1, r, 2 * V_DIM), lambda i, h, t: (i, h, 0, 0),
                         pipeline_mode=pl.Buffered(1)),
            _const_spec((SUBLANES, LANES), layer),
            _const_spec((1, V_DIM), layer),
        ],
        out_specs=pl.BlockSpec((1, TILE, V_DIM), lambda i, h, t: (i, t, h)),
        out_shape=jax.ShapeDtypeStruct((b, r, ATTN_WIDTH), BF16),
        compiler_params=_params(("parallel", "parallel", "arbitrary")),
        name="diff_attention",
    )(q, kt, ve, lam_p, g_subln)


def _zoh(are_ref, aim_ref, ldt_ref):
    a_re = are_ref[0, 0]
    a_im = aim_ref[0, 0]
    dt = jnp.exp(ldt_ref[0, 0])
    mag = jnp.exp(dt * a_re)
    ang = dt * a_im
    ab_re = mag * jnp.cos(ang)
    ab_im = mag * jnp.sin(ang)
    den = a_re * a_re + a_im * a_im
    n_re = ab_re - 1.0
    f_re = (n_re * a_re + ab_im * a_im) / den
    f_im = (ab_im * a_re - n_re * a_im) / den
    return ab_re, ab_im, f_re, f_im


def _complex_powers(base_re, base_im, n):
    pows = [(jnp.ones_like(base_re), jnp.zeros_like(base_im))]
    for _ in range(n):
        pr, pi = pows[-1]
        pows.append((pr * base_re - pi * base_im, pr * base_im + pi * base_re))
    return pows


def _fill_scan_tables(pw_ref, base_re, base_im, reverse):
    shape = (SUBLANES, SSM_STATE)
    pows = _complex_powers(base_re, base_im, SUBLANES)
    row = lax.broadcasted_iota(jnp.int32, shape, 0)
    zero = jnp.zeros(shape, F32)
    for idx, k in enumerate((1, 2, 4)):
        mask = (row + k <= SUBLANES - 1) if reverse else (row >= k)
        pw_ref[2 * idx] = jnp.where(mask, jnp.broadcast_to(pows[k][0], shape), zero)
        pw_ref[2 * idx + 1] = jnp.where(mask, jnp.broadcast_to(pows[k][1], shape), zero)
    pcr = zero
    pci = zero
    for tt in range(SUBLANES):
        e = (SUBLANES - tt) if reverse else tt + 1
        pcr = jnp.where(row == tt, jnp.broadcast_to(pows[e][0], shape), pcr)
        pci = jnp.where(row == tt, jnp.broadcast_to(pows[e][1], shape), pci)
    pw_ref[6] = pcr
    pw_ref[7] = pci


def _scan_rows(s_ref, pw_ref, carry, nrows, reverse, exclusive, groups=None):
    ns = SSM_STATE
    shape = (SUBLANES, ns)
    ngroups = nrows // SUBLANES
    edge = (SUBLANES - 1) if reverse else 0

    def group(g, carry):
        cr, ci = carry
        gi = (ngroups - 1 - g) if reverse else g
        r0 = pl.multiple_of(gi * SUBLANES, SUBLANES)
        xr = s_ref[pl.ds(r0, SUBLANES), :ns]
        xi = s_ref[pl.ds(r0, SUBLANES), ns:]
        for idx, k in enumerate((1, 2, 4)):
            sh = (SUBLANES - k) if reverse else k
            sr = pltpu.roll(xr, sh, 0)
            si = pltpu.roll(xi, sh, 0)
            pr = pw_ref[2 * idx]
            pi = pw_ref[2 * idx + 1]
            xr, xi = xr + pr * sr - pi * si, xi + pr * si + pi * sr
        pr = pw_ref[6]
        pi = pw_ref[7]
        xr, xi = xr + pr * cr - pi * ci, xi + pr * ci + pi * cr
        if exclusive:
            row = lax.broadcasted_iota(jnp.int32, shape, 0)
            sh = (SUBLANES - 1) if reverse else 1
            er = jnp.where(row == edge, cr, pltpu.roll(xr, sh, 0))
            ei = jnp.where(row == edge, ci, pltpu.roll(xi, sh, 0))
        else:
            er, ei = xr, xi
        s_ref[pl.ds(r0, SUBLANES), :ns] = er
        s_ref[pl.ds(r0, SUBLANES), ns:] = ei
        last = 0 if reverse else SUBLANES - 1
        return (jnp.broadcast_to(xr[last:last + 1], shape),
                jnp.broadcast_to(xi[last:last + 1], shape))

    if groups is None:
        return lax.fori_loop(0, ngroups, group, carry)
    for g in groups:
        carry = group(g, carry)
    return carry


def _s5_ctx_kernel(u_ref, are_ref, aim_ref, ldt_ref, bre_ref, bim_ref, c_ref,
                   o_ref, st_ref, bbar_ref, pw_ref, s_ref, y_ref, *, reverse):
    ns = SSM_STATE
    ab_re, ab_im, f_re, f_im = _zoh(are_ref, aim_ref, ldt_ref)
    bre = bre_ref[0, 0]
    bim = bim_ref[0, 0]
    bbar_ref[:, :ns] = (f_re * bre - f_im * bim).astype(BF16)
    bbar_ref[:, ns:] = (f_re * bim + f_im * bre).astype(BF16)
    _fill_scan_tables(pw_ref, ab_re, ab_im, reverse)
    s_ref[...] = jnp.dot(u_ref[0].astype(BF16), bbar_ref[...], preferred_element_type=F32)
    zero = jnp.zeros((SUBLANES, ns), F32)
    cr, ci = _scan_rows(s_ref, pw_ref, (zero, zero), TILE, reverse, exclusive=False)
    st_ref[0, 0] = cr
    st_ref[0, 1] = ci
    y = jnp.dot(s_ref[...].astype(BF16), c_ref[0, 0], preferred_element_type=F32)
    _rows_to_blocked(y, y_ref, o_ref.at[0], S5_BLOCK)


def _s5_block_kernel(u8_ref, s0_ref, are_ref, aim_ref, ldt_ref, bre_ref, bim_ref, ctre_ref,
                     ctim_ref, o_ref, wx_ref, wc_ref, wt_ref, pw_ref, carry_ref, s_ref,
                     *, reverse):
    ns = SSM_STATE
    w = SSM_WIDTH
    nb = S5_BLOCK
    c = pl.program_id(0)
    b = pl.program_id(1)

    @pl.when((c == 0) & (b == 0))
    def _build_maps():
        ab_re, ab_im, f_re, f_im = _zoh(are_ref, aim_ref, ldt_ref)
        bre = bre_ref[0, 0]
        bim = bim_ref[0, 0]
        bb_re = f_re * bre - f_im * bim
        bb_im = f_re * bim + f_im * bre
        pows = _complex_powers(ab_re, ab_im, nb)
        ctre = ctre_ref[0, 0]
        ctim = ctim_ref[0, 0]
        ct_stack = jnp.concatenate([ctre, -ctim], axis=1).astype(BF16)
        taps = []
        for e in range(nb):
            pr, pi = pows[e]
            xr = pr * bb_re - pi * bb_im
            xi = pr * bb_im + pi * bb_re
            i = e if reverse else nb - 1 - e
            wx_ref[i * w:(i + 1) * w, :ns] = xr.astype(BF16)
            wx_ref[i * w:(i + 1) * w, ns:] = xi.astype(BF16)
            xk = jnp.concatenate([xr, xi], axis=1).astype(BF16)
            taps.append(lax.dot_general(xk, ct_stack, (((1,), (1,)), ((), ())),
                                        preferred_element_type=F32).astype(BF16))
        zero_blk = jnp.zeros((w, w), BF16)
        for i in range(nb):
            for j in range(nb):
                lag = (i - j) if reverse else (j - i)
                wt_ref[i * w:(i + 1) * w, j * w:(j + 1) * w] = taps[lag] if lag >= 0 else zero_blk
        for j in range(nb):
            pr, pi = pows[nb - j] if reverse else pows[j + 1]
            wc_ref[:ns, j * w:(j + 1) * w] = (ctre * pr - ctim * pi).T.astype(BF16)
            wc_ref[ns:, j * w:(j + 1) * w] = (-(ctre * pi + ctim * pr)).T.astype(BF16)
        _fill_scan_tables(pw_ref, pows[nb][0], pows[nb][1], reverse)

    @pl.when(c == 0)
    def _load_state():
        carry_ref[b] = s0_ref[0]

    u8 = u8_ref[0].astype(BF16)
    s_ref[...] = jnp.dot(u8, wx_ref[...], preferred_element_type=F32)
    carry = (carry_ref[b, 0], carry_ref[b, 1])
    ngroups = S5_CHUNK_ROWS // SUBLANES
    per = ngroups // nb
    for j in range(nb):
        o_ref[0, :, j * w:(j + 1) * w] = jnp.dot(u8, wt_ref[:, j * w:(j + 1) * w],
                                                 preferred_element_type=F32)
        carry = _scan_rows(s_ref, pw_ref, carry, S5_CHUNK_ROWS, reverse, exclusive=True,
                           groups=range(j * per, (j + 1) * per))
    carry_ref[b, 0] = carry[0]
    carry_ref[b, 1] = carry[1]
    o_ref[0] = o_ref[0] + jnp.dot(s_ref[...].astype(BF16), wc_ref[...],
                                  preferred_element_type=F32)


def _s5_direction(layer, direction, u, u8, a_re, a_im, ldt, b_re, b_im, ct_re, ct_im, c_stack):
    b, r, w = u.shape
    nt = r // TILE
    rows = r // S5_BLOCK
    wide = S5_BLOCK * w
    tile_rows = TILE // S5_BLOCK
    ns = SSM_STATE
    reverse = direction == 1

    def dir_spec(shape):
        nd = len(shape)
        return pl.BlockSpec((1, 1) + tuple(shape), lambda *_: (layer, direction) + (0,) * nd,
                            pipeline_mode=pl.Buffered(1))

    y_ctx, state = pl.pallas_call(
        functools.partial(_s5_ctx_kernel, reverse=reverse),
        grid=(b,),
        in_specs=[pl.BlockSpec((1, TILE, w), lambda i: (i, nt - 1, 0)),
                  dir_spec((1, ns)), dir_spec((1, ns)), dir_spec((1, ns)),
                  dir_spec((w, ns)), dir_spec((w, ns)), dir_spec((2 * ns, w))],
        out_specs=[pl.BlockSpec((1, tile_rows, wide), lambda i: (i, 0, 0)),
                   pl.BlockSpec((1, 2, SUBLANES, ns), lambda i: (i, 0, 0, 0))],
        out_shape=[jax.ShapeDtypeStruct((b, tile_rows, wide), F32),
                   jax.ShapeDtypeStruct((b, 2, SUBLANES, ns), F32)],
        scratch_shapes=[pltpu.VMEM((w, 2 * ns), BF16),
                        pltpu.VMEM((8, SUBLANES, ns), F32),
                        pltpu.VMEM((TILE, 2 * ns), F32),
                        pltpu.VMEM((w // LANES, TILE, LANES), F32)],
        compiler_params=_params(("parallel",)),
        name="s5_context",
    )(u, a_re, a_im, ldt, b_re, b_im, c_stack)

    nchunks = (r - TILE) // (S5_BLOCK * S5_CHUNK_ROWS)

    def chunk(c, i):
        return (i, nchunks - 1 - c if reverse else c, 0)

    y_lat = pl.pallas_call(
        functools.partial(_s5_block_kernel, reverse=reverse),
        grid=(nchunks, b),
        in_specs=[pl.BlockSpec((1, S5_CHUNK_ROWS, wide), chunk),
                  pl.BlockSpec((1, 2, SUBLANES, ns), lambda c, i: (i, 0, 0, 0)),
                  dir_spec((1, ns)), dir_spec((1, ns)), dir_spec((1, ns)),
                  dir_spec((w, ns)), dir_spec((w, ns)), dir_spec((w, ns)), dir_spec((w, ns))],
        out_specs=pl.BlockSpec((1, S5_CHUNK_ROWS, wide), chunk),
        out_shape=jax.ShapeDtypeStruct((b, rows - tile_rows, wide), F32),
        scratch_shapes=[pltpu.VMEM((wide, 2 * ns), BF16),
                        pltpu.VMEM((2 * ns, wide), BF16),
                        pltpu.VMEM((wide, wide), BF16),
                        pltpu.VMEM((8, SUBLANES, ns), F32),
                        pltpu.VMEM((b, 2, SUBLANES, ns), F32),
                        pltpu.VMEM((S5_CHUNK_ROWS, 2 * ns), F32)],
        compiler_params=_params(("arbitrary", "arbitrary"), vmem=S5_VMEM_LIMIT),
        name="s5_blocks_reverse" if reverse else "s5_blocks_forward",
    )(u8, state, a_re, a_im, ldt, b_re, b_im, ct_re, ct_im)
    return y_lat, y_ctx


def _fourier_outer_kernel(w_ref, x_ref, g_ref):
    g_ref[0] = jnp.dot(w_ref[...], x_ref[0].astype(BF16),
                       preferred_element_type=F32).astype(BF16)


def _fourier_outer(f, w_outer):
    b, r, w = f.shape
    nt = r // TILE
    l2n = nt - 1
    flat = TILE * w
    return pl.pallas_call(
        _fourier_outer_kernel,
        grid=(b, flat // FOURIER_CHUNK),
        in_specs=[_const_spec((2 * l2n, l2n)),
                  pl.BlockSpec((1, l2n, FOURIER_CHUNK), lambda i, j: (i, 0, j))],
        out_specs=pl.BlockSpec((1, 2 * l2n, FOURIER_CHUNK), lambda i, j: (i, 0, j)),
        out_shape=jax.ShapeDtypeStruct((b, 2 * l2n, flat), BF16),
        compiler_params=_params(("parallel", "parallel")),
        name="fourier_outer",
    )(w_outer, f.reshape(b, nt, flat))


def _fourier_tail(y_re, y_im, c64_ref, wf_ref):
    w = FNET_WIDTH
    z = (jnp.dot(y_re.astype(BF16), c64_ref[:w], preferred_element_type=F32)
         + jnp.dot(y_im.astype(BF16), c64_ref[w:], preferred_element_type=F32))
    return jnp.dot(z.astype(BF16), wf_ref[0], preferred_element_type=F32)


def _fourier_inner_kernel(gr_ref, gi_ref, m_ref, c64_ref, wf_ref, o_ref):
    w = FNET_WIDTH
    ys = [jnp.dot(m_ref[i, :, :TILE], gr_ref[0, i], preferred_element_type=F32)
          + jnp.dot(m_ref[i, :, TILE:], gi_ref[0, i], preferred_element_type=F32)
          for i in range(FOURIER_GROUP)]
    for i, y in enumerate(ys):
        o_ref[0, :, i * w:(i + 1) * w] = _fourier_tail(y[:TILE], y[TILE:], c64_ref, wf_ref)


def _fourier_inner(layer, g, m_tab, c64_tab, wf_blk):
    b, two_l2n, flat = g.shape
    l2n = two_l2n // 2
    w = flat // TILE
    steps = l2n // FOURIER_GROUP
    g4 = g.reshape(b, two_l2n, TILE, w)
    return pl.pallas_call(
        _fourier_inner_kernel,
        grid=(b, steps),
        in_specs=[
            pl.BlockSpec((1, FOURIER_GROUP, TILE, w), lambda i, j: (i, j, 0, 0)),
            pl.BlockSpec((1, FOURIER_GROUP, TILE, w), lambda i, j: (i, steps + j, 0, 0)),
            pl.BlockSpec((FOURIER_GROUP, 2 * TILE, 2 * TILE), lambda i, j: (j, 0, 0)),
            _const_spec((2 * w, w)),
            _const_spec((w, w), layer),
        ],
        out_specs=pl.BlockSpec((1, TILE, FOURIER_GROUP * w), lambda i, j: (i, 0, j)),
        out_shape=jax.ShapeDtypeStruct((b, TILE, l2n * w), F32),
        compiler_params=_params(("parallel", "parallel")),
        name="fourier_inner",
    )(g4, g4, m_tab, c64_tab, wf_blk)


def _fourier_ctx_kernel(x_ref, m_ref, c64_ref, wf_ref, o_ref, z_ref, *, l2n):
    y = jnp.dot(m_ref[...], x_ref[0].astype(BF16), preferred_element_type=F32)
    z = _fourier_tail(y[:TILE], y[TILE:], c64_ref, wf_ref)
    _rows_to_blocked(z, z_ref, o_ref.at[0], l2n)


def _fourier_context(layer, f, m_ctx, c64_tab, wf_blk):
    b, r, w = f.shape
    nt = r // TILE
    l2n = nt - 1
    return pl.pallas_call(
        functools.partial(_fourier_ctx_kernel, l2n=l2n),
        grid=(b,),
        in_specs=[
            pl.BlockSpec((1, TILE, w), lambda i: (i, nt - 1, 0)),
            _const_spec((2 * TILE, TILE)),
            _const_spec((2 * w, w)),
            _const_spec((w, w), layer),
        ],
        out_specs=pl.BlockSpec((1, TILE // l2n, l2n * w), lambda i: (i, 0, 0)),
        out_shape=jax.ShapeDtypeStruct((b, TILE // l2n, l2n * w), F32),
        scratch_shapes=[pltpu.VMEM((w // LANES, TILE, LANES), F32)],
        compiler_params=_params(("parallel",)),
        name="fourier_context",
    )(f, m_ctx, c64_tab, wf_blk)


def _outffn_kernel(x_ref, a_ref, yf_ref, yb_ref, yfc_ref, ybc_ref, u_ref, fn_ref, fnc_ref, mod_ref,
                   dsk_ref, wglu_ref, bglu_ref, wo_ref, g2_ref, w1_ref, w2_ref, o_ref, ytok_ref,
                   ftok_ref, *, l2n, nt):
    d = D_MODEL
    x = x_ref[0]
    mod = mod_ref[0, 0]
    o1 = ATTN_WIDTH
    o2 = ATTN_WIDTH + SSM_WIDTH
    is_ctx = pl.program_id(1) == nt - 1
    y_blk = jnp.where(is_ctx, yfc_ref[0] + ybc_ref[0], yf_ref[0] + yb_ref[0])
    y_scan = _blocked_to_rows(y_blk, ytok_ref, S5_BLOCK)
    fnet = _blocked_to_rows(jnp.where(is_ctx, fnc_ref[0], fn_ref[0]), ftok_ref, l2n)
    hg = jax.nn.gelu(y_scan + dsk_ref[0] * u_ref[0])
    zg = jnp.dot(hg.astype(BF16), wglu_ref[0], preferred_element_type=F32) + bglu_ref[0]
    s5 = (hg * jax.nn.sigmoid(zg)).astype(BF16)
    mix = (jnp.dot(a_ref[0], wo_ref[0, :o1], preferred_element_type=F32)
           + jnp.dot(s5, wo_ref[0, o1:o2], preferred_element_type=F32)
           + jnp.dot(fnet.astype(BF16), wo_ref[0, o2:], preferred_element_type=F32))
    x1 = x + mod[:, 2 * d:3 * d] * mix
    ms = jnp.mean(x1 * x1, axis=-1, keepdims=True)
    hn = x1 * lax.rsqrt(ms + EPS) * g2_ref[0]
    h = (hn * (1.0 + mod[:, 4 * d:5 * d]) + mod[:, 3 * d:4 * d]).astype(BF16)
    acc = jnp.zeros((TILE, d), F32)
    for cc in range(D_FF // FF_CHUNK):
        sl = slice(cc * FF_CHUNK, (cc + 1) * FF_CHUNK)
        t = jnp.dot(h, w1_ref[0, :, sl], preferred_element_type=F32)
        t = jnp.square(jnp.maximum(t, 0.0))
        acc = acc + jnp.dot(t.astype(BF16), w2_ref[0, sl, :], preferred_element_type=F32)
    o_ref[0] = x1 + mod[:, 5 * d:] * acc


def _out_ffn(layer, xs, attn, y_fwd, y_bwd, u, fn, fn_ctx, mods, d_skip, w_glu, b_glu, w_out,
             g_norm2, w_ff1, w_ff2, latent_only):
    b, r, d = xs.shape
    nt = r // TILE
    l2n = nt - 1
    depth = w_out.shape[0]
    s5_shape = (1, TILE // S5_BLOCK, S5_BLOCK * SSM_WIDTH)
    fn_shape = (1, TILE // l2n, l2n * FNET_WIDTH)

    def latent(shape):
        return pl.BlockSpec(shape, lambda i, t: (i, jnp.minimum(t, nt - 2), 0))

    def context(shape):
        return pl.BlockSpec(shape, lambda i, t: (i, 0, 0))

    def tile(width):
        return pl.BlockSpec((1, TILE, width), lambda i, t: (i, t, 0))

    return pl.pallas_call(
        functools.partial(_outffn_kernel, l2n=l2n, nt=nt),
        grid=(b, nt - 1 if latent_only else nt),
        in_specs=[
            tile(d), tile(ATTN_WIDTH), latent(s5_shape), latent(s5_shape), context(s5_shape),
            context(s5_shape), tile(SSM_WIDTH), latent(fn_shape), context(fn_shape),
            pl.BlockSpec((1, 1, 1, 6 * d),
                         lambda i, t: (layer, jnp.where(t == nt - 1, b, i), 0, 0)),
            _const_spec((1, SSM_WIDTH), layer),
            _const_spec((SSM_WIDTH, SSM_WIDTH), layer),
            _const_spec((1, SSM_WIDTH), layer),
            _const_spec((d, d), layer),
            _const_spec((1, d), layer),
            _const_spec((d, D_FF), layer),
            _const_spec((D_FF, d), layer),
        ],
        out_specs=tile(d),
        out_shape=jax.ShapeDtypeStruct((b, r - TILE if latent_only else r, d), F32),
        scratch_shapes=[pltpu.VMEM((SSM_WIDTH // LANES, TILE, LANES), F32),
                        pltpu.VMEM((FNET_WIDTH // LANES, TILE, LANES), F32)],
        input_output_aliases={} if latent_only else {0: 0},
        compiler_params=_params(("parallel", "parallel")),
        name="out_ffn",
    )(xs, attn, y_fwd[0], y_bwd[0], y_fwd[1], y_bwd[1], u, fn, fn_ctx,
      mods.reshape(depth, SUBLANES, 1, 6 * d), d_skip, w_glu, b_glu,
      w_out, g_norm2.reshape(depth, 1, d), w_ff1, w_ff2)


def _head_lane_tables():
    j = np.arange(LANES)
    half = j // 64
    mp = (j // 32) % 2
    idx = j % 32
    src_in_head = mp * HEAD_DIM + half * 32 + idx
    gain_idx = half * 32 + idx
    return src_in_head, gain_idx, mp


def _in_proj_column_order():
    src_in_head, _, _ = _head_lane_tables()
    qk = np.concatenate([h * LANES + src_in_head for h in range(N_HEADS)])
    return np.concatenate([qk, QK_WIDTH + qk, np.arange(2 * QK_WIDTH, IN_WIDTH)])


def _same_map_matrix():
    i = np.arange(256)
    head = i // LANES
    mp = ((i % LANES) // 32) % 2
    same = (head[:, None] == head[None, :]) & (mp[:, None] == mp[None, :])
    return same.astype(np.float32)


def _rope_tables(seq, ctx_len):
    t = jnp.arange(seq)
    row = (t // GRID_W).astype(F32)
    col = (t % GRID_W).astype(F32)
    n_freq = HEAD_DIM // 4
    inv = jnp.power(ROPE_BASE, -jnp.arange(n_freq, dtype=F32) / n_freq)
    ang = jnp.concatenate([row[:, None] * inv, col[:, None] * inv], axis=-1)
    cos = jnp.tile(jnp.cos(ang), (1, 4))
    sign = np.where(np.arange(LANES) < LANES // 2, -1.0, 1.0).astype(np.float32)
    sin = jnp.tile(jnp.sin(ang), (1, 4)) * sign
    cos = jnp.concatenate([cos, jnp.ones((ctx_len, LANES), F32)], axis=0)
    sin = jnp.concatenate([sin, jnp.zeros((ctx_len, LANES), F32)], axis=0)
    return cos, sin


def _dft_tables(length):
    l2n = length // TILE
    k2 = np.arange(l2n)
    phi = 2.0 * np.pi * ((k2[:, None] * k2[None, :]) % l2n) / l2n
    w_outer = np.concatenate([np.cos(phi), -np.sin(phi)], axis=0).astype(np.float32)
    l1 = np.arange(TILE)
    alpha = 2.0 * np.pi * ((k2[:, None] * l1[None, :]) % length) / length
    beta = 2.0 * np.pi * ((l1[:, None] * l1[None, :]) % TILE) / TILE
    ca = jnp.asarray(np.cos(alpha).astype(np.float32))[:, None, :]
    sa = jnp.asarray(np.sin(alpha).astype(np.float32))[:, None, :]
    cb = jnp.asarray(np.cos(beta).astype(np.float32))[None]
    sb = jnp.asarray(np.sin(beta).astype(np.float32))[None]
    norm = 1.0 / math.sqrt(length)
    ct = (ca * cb - sa * sb) * norm
    st = (sa * cb + ca * sb) * norm
    m = jnp.concatenate([jnp.concatenate([ct, st], axis=2),
                         jnp.concatenate([-st, ct], axis=2)], axis=1).astype(BF16)
    return jnp.asarray(w_outer).astype(BF16), m


def _tile_dft_table():
    l1 = np.arange(TILE)
    beta = 2.0 * np.pi * ((l1[:, None] * l1[None, :]) % TILE) / TILE
    tab = np.concatenate([np.cos(beta), -np.sin(beta)], axis=0) / math.sqrt(TILE)
    return jnp.asarray(tab.astype(np.float32)).astype(BF16)


def _channel_dft_table():
    c = np.arange(FNET_C)
    th = 2.0 * np.pi * ((c[:, None] * c[None, :]) % FNET_C) / FNET_C
    eye = np.eye(FNET_G)
    norm = 1.0 / math.sqrt(FNET_C)
    cblk = np.kron(eye, np.cos(th)) * norm
    sblk = np.kron(eye, np.sin(th)) * norm
    return jnp.asarray(np.concatenate([cblk, sblk], axis=0).astype(np.float32)).astype(BF16)


def _block_diag(blocks):
    g = blocks.shape[-3]
    eye = jnp.eye(g, dtype=blocks.dtype)
    out = jnp.einsum('...gab,gh->...gahb', blocks, eye)
    return out.reshape(blocks.shape[:-3] + (g * blocks.shape[-2], g * blocks.shape[-1]))


def kernel(x, c, ctx, c_ctx, w_mod, b_mod, g_norm1, w_in, g_qnorm, g_knorm, lam_q1, lam_k1, lam_q2, lam_k2, g_subln, ssm_a_re, ssm_a_im, ssm_log_dt, ssm_b_re, ssm_b_im, ssm_c_re, ssm_c_im, ssm_d, w_glu, b_glu, w_fnet, w_out, g_norm2, w_ff1, w_ff2):
    bsz, seq, d = x.shape
    ctx_len = ctx.shape[1]
    depth = w_mod.shape[0]
    assert d == D_MODEL and ctx_len == TILE and seq % TILE == 0 and seq % GRID_W == 0
    assert bsz + 1 <= SUBLANES

    assert (seq // TILE) % FOURIER_GROUP == 0 and seq % (S5_BLOCK * S5_CHUNK_ROWS) == 0
    xs = jnp.concatenate([x, ctx], axis=1)
    act = jnp.concatenate([c, c_ctx[None], jnp.zeros((SUBLANES - bsz - 1, d), F32)], axis=0)
    mods = _modulation(act, w_mod, b_mod)

    w_in_p = jnp.take(w_in, jnp.asarray(_in_proj_column_order()), axis=2).astype(BF16)
    _, gain_idx, _ = _head_lane_tables()
    gq = jnp.tile(g_qnorm[:, gain_idx], (1, N_HEADS)).reshape(depth, 1, QK_WIDTH)
    gk = jnp.tile(g_knorm[:, gain_idx], (1, N_HEADS)).reshape(depth, 1, QK_WIDTH)
    e_mat = jnp.asarray(_same_map_matrix()).astype(BF16)
    cos_t, sin_t = _rope_tables(seq, ctx_len)
    lam_init = np.array([0.8 - 0.6 * math.exp(-0.3 * i) for i in range(depth)], np.float32)
    pad = jnp.zeros((depth, LANES - HEAD_DIM), F32)
    lam_rows = [jnp.concatenate([v, pad], axis=1) for v in (lam_q1, lam_k1, lam_q2, lam_k2)]
    const_row = np.zeros((depth, LANES), np.float32)
    const_row[:, 0] = lam_init
    const_row[:, 1] = 1.0 - lam_init
    lam_p = jnp.stack(lam_rows + [jnp.asarray(const_row)]
                      + [jnp.zeros((depth, LANES), F32)] * (SUBLANES - 5), axis=1)
    g_sub = g_subln.reshape(depth, 1, V_DIM)

    a_re = ssm_a_re.reshape(depth, 2, 1, SSM_STATE)
    a_im = ssm_a_im.reshape(depth, 2, 1, SSM_STATE)
    ldt = jnp.repeat(ssm_log_dt, SSM_N, axis=-1).reshape(depth, 2, 1, SSM_STATE)
    b_re = _block_diag(jnp.swapaxes(ssm_b_re, -1, -2))
    b_im = _block_diag(jnp.swapaxes(ssm_b_im, -1, -2))
    c_stack = jnp.concatenate([_block_diag(jnp.swapaxes(ssm_c_re, -1, -2)),
                               -_block_diag(jnp.swapaxes(ssm_c_im, -1, -2))],
                              axis=2).astype(BF16)
    ct_re = _block_diag(ssm_c_re)
    ct_im = _block_diag(ssm_c_im)
    d_skip = ssm_d.reshape(depth, 1, SSM_WIDTH)
    w_glu_b = w_glu.astype(BF16)
    b_glu_r = b_glu.reshape(depth, 1, SSM_WIDTH)

    w_outer, m_lat = _dft_tables(seq)
    m_ctx = _tile_dft_table()
    c64_tab = _channel_dft_table()
    wf_blk = _block_diag(w_fnet).astype(BF16)

    w_out_b = w_out.astype(BF16)
    w_ff1_b = w_ff1.astype(BF16)
    w_ff2_b = w_ff2.astype(BF16)

    for layer in range(depth):
        q, kt, ve, u, u8, f = _in_projection(layer, xs, mods, g_norm1, w_in_p, gq, gk, e_mat,
                                         cos_t, sin_t)
        attn = _attention(layer, q, kt, ve, lam_p, g_sub)
        y_fwd, y_bwd = [_s5_direction(layer, dr, u, u8, a_re, a_im, ldt, b_re, b_im, ct_re,
                                      ct_im, c_stack) for dr in range(2)]
        fn = _fourier_inner(layer, _fourier_outer(f, w_outer), m_lat, c64_tab, wf_blk)
        fn_ctx = _fourier_context(layer, f, m_ctx, c64_tab, wf_blk)
        xs = _out_ffn(layer, xs, attn, y_fwd, y_bwd, u, fn, fn_ctx, mods, d_skip, w_glu_b, b_glu_r,
                      w_out_b, g_norm2, w_ff1_b, w_ff2_b, latent_only=layer == depth - 1)
    return xs
```

```python
import functools
import math

import numpy as np
import jax
import jax.numpy as jnp
from jax import lax
from jax.experimental import pallas as pl
from jax.experimental.pallas import tpu as pltpu

F32 = jnp.float32
BF16 = jnp.bfloat16

D_MODEL = 1024
GRID_W = 64
N_HEADS = 4
HEAD_DIM = 64
V_DIM = 2 * HEAD_DIM
QK_WIDTH = N_HEADS * 2 * HEAD_DIM
ATTN_WIDTH = N_HEADS * V_DIM
SSM_WIDTH = D_MODEL // 4
SSM_P = 16
SSM_G = SSM_WIDTH // SSM_P
SSM_N = 64
SSM_STATE = SSM_G * SSM_N
FNET_WIDTH = D_MODEL // 4
FNET_G = 4
FNET_C = FNET_WIDTH // FNET_G
IN_WIDTH = 2 * QK_WIDTH + ATTN_WIDTH + SSM_WIDTH + FNET_WIDTH
D_FF = 4 * D_MODEL
ROPE_BASE = 10000.0
EPS = 1e-6
SCALE = HEAD_DIM ** -0.5
LOG2E = math.log2(math.e)

TILE = 256
LANES = 128
SUBLANES = 8
FF_CHUNK = 1024
MOD_BLOCK = 1536
FOURIER_CHUNK = 8192
FOURIER_GROUP = 4
S5_BLOCK = 8
S5_CHUNK_ROWS = 256
VMEM_LIMIT = 48 * 1024 * 1024
S5_VMEM_LIMIT = 56 * 1024 * 1024

def _const_spec(shape, layer=None):
    nd = len(shape)
    if layer is None:
        return pl.BlockSpec(shape, lambda *_: (0,) * nd, pipeline_mode=pl.Buffered(1))
    return pl.BlockSpec((1,) + tuple(shape), lambda *_: (layer,) + (0,) * nd,
                        pipeline_mode=pl.Buffered(1))


def _params(sem, vmem=None):
    return pltpu.CompilerParams(dimension_semantics=sem, vmem_limit_bytes=vmem or VMEM_LIMIT)


def _blocked_lane(j, h, period, nh, tile_major):
    return ((h * period + j) if tile_major else (j * nh + h)) * LANES


def _rows_to_blocked(tok, scr_ref, blk_ref, period, tile_major=False):
    n, w = tok.shape
    nh = w // LANES
    for h in range(nh):
        scr_ref[h] = tok[:, h * LANES:(h + 1) * LANES]
    for j in range(period):
        for h in range(nh):
            lo = _blocked_lane(j, h, period, nh, tile_major)
            blk_ref[:, lo:lo + LANES] = scr_ref[h, pl.ds(j, n // period, stride=period), :]


def _blocked_to_rows(blk, scr_ref, period, tile_major=False):
    nh, n, _ = scr_ref.shape
    for j in range(period):
        for h in range(nh):
            lo = _blocked_lane(j, h, period, nh, tile_major)
            scr_ref[h, pl.ds(j, n // period, stride=period), :] = blk[:, lo:lo + LANES]
    return jnp.concatenate([scr_ref[h] for h in range(nh)], axis=1)


def _mod_kernel(act_ref, w_ref, b_ref, o_ref):
    a = act_ref[...]
    a = a * jax.nn.sigmoid(a)
    o_ref[0] = jnp.dot(a.astype(BF16), w_ref[0].astype(BF16),
                       preferred_element_type=F32) + b_ref[0]


def _modulation(act, w_mod, b_mod):
    depth, d, n = w_mod.shape
    bn = MOD_BLOCK
    return pl.pallas_call(
        _mod_kernel,
        grid=(depth, n // bn),
        in_specs=[pl.BlockSpec((SUBLANES, d), lambda l, j: (0, 0)),
                  pl.BlockSpec((1, d, bn), lambda l, j: (l, 0, j)),
                  pl.BlockSpec((1, 1, bn), lambda l, j: (l, 0, j))],
        out_specs=pl.BlockSpec((1, SUBLANES, bn), lambda l, j: (l, 0, j)),
        out_shape=jax.ShapeDtypeStruct((depth, SUBLANES, n), F32),
        compiler_params=_params(("parallel", "parallel")),
        name="modulation",
    )(act, w_mod, b_mod.reshape(depth, 1, n))


def _inproj_kernel(x_ref, mod_ref, g1_ref, w_ref, gq_ref, gk_ref, e_ref, cos_ref, sin_ref,
                   q_ref, kt_ref, ve_ref, u_ref, u8_ref, f_ref, stage_ref):
    d = D_MODEL
    x = x_ref[0]
    ms = jnp.mean(x * x, axis=-1, keepdims=True)
    xn = x * lax.rsqrt(ms + EPS) * g1_ref[0]
    mod = mod_ref[0, 0]
    h = xn * (1.0 + mod[:, d:2 * d]) + mod[:, :d]
    proj = jnp.dot(h.astype(BF16), w_ref[0], preferred_element_type=F32)
    e = e_ref[...]
    cos = cos_ref[...]
    sin = sin_ref[...]

    def qk_norm(z, g):
        sq = z * z
        hi = sq.astype(BF16)
        lo = (sq - hi.astype(F32)).astype(BF16)
        parts = []
        for j in range(QK_WIDTH // 256):
            sl = slice(j * 256, (j + 1) * 256)
            parts.append(jnp.dot(hi[:, sl], e, preferred_element_type=F32)
                         + jnp.dot(lo[:, sl], e, preferred_element_type=F32))
        ssum = jnp.concatenate(parts, axis=-1)
        return z * lax.rsqrt(ssum * (1.0 / HEAD_DIM) + EPS) * g

    def rope(zh):
        return zh * cos + pltpu.roll(zh, LANES // 2, 1) * sin

    qn = qk_norm(proj[:, :QK_WIDTH], gq_ref[0])
    kn = qk_norm(proj[:, QK_WIDTH:2 * QK_WIDTH], gk_ref[0])
    row_map = (lax.broadcasted_iota(jnp.int32, (LANES, TILE), 0) // 32) % 2
    ones = jnp.ones((TILE, LANES), BF16)
    for hh in range(N_HEADS):
        sl = slice(hh * LANES, (hh + 1) * LANES)
        q_ref[0, :, sl] = (rope(qn[:, sl]) * (SCALE * LOG2E)).astype(BF16)
        kt = rope(kn[:, sl]).T
        kt_ref[0, hh, 0] = jnp.where(row_map == 0, kt, 0.0).astype(BF16)
        kt_ref[0, hh, 1] = jnp.where(row_map == 1, kt, 0.0).astype(BF16)
        vo = 2 * QK_WIDTH + hh * V_DIM
        ve_ref[0, hh, :, :V_DIM] = proj[:, vo:vo + V_DIM].astype(BF16)
        ve_ref[0, hh, :, V_DIM:] = ones
    uo = 2 * QK_WIDTH + ATTN_WIDTH
    u_ref[0] = proj[:, uo:uo + SSM_WIDTH]
    _rows_to_blocked(proj[:, uo:uo + SSM_WIDTH], stage_ref, u8_ref.at[0], S5_BLOCK, tile_major=True)
    f_ref[0] = proj[:, uo + SSM_WIDTH:]


def _in_projection(layer, xs, mods, g_norm1, w_in, gq, gk, e_mat, cos_t, sin_t):
    b, r, d = xs.shape
    nt = r // TILE
    depth = w_in.shape[0]
    return pl.pallas_call(
        _inproj_kernel,
        grid=(b, nt),
        in_specs=[
            pl.BlockSpec((1, TILE, d), lambda i, t: (i, t, 0)),
            pl.BlockSpec((1, 1, 1, 6 * d),
                         lambda i, t: (layer, jnp.where(t == nt - 1, b, i), 0, 0)),
            _const_spec((1, d), layer),
            _const_spec((d, IN_WIDTH), layer),
            _const_spec((1, QK_WIDTH), layer),
            _const_spec((1, QK_WIDTH), layer),
            _const_spec((256, 256)),
            pl.BlockSpec((TILE, LANES), lambda i, t: (t, 0)),
            pl.BlockSpec((TILE, LANES), lambda i, t: (t, 0)),
        ],
        out_specs=[
            pl.BlockSpec((1, TILE, QK_WIDTH), lambda i, t: (i, t, 0)),
            pl.BlockSpec((1, N_HEADS, 2, LANES, TILE), lambda i, t: (i, 0, 0, 0, t)),
            pl.BlockSpec((1, N_HEADS, TILE, 2 * V_DIM), lambda i, t: (i, 0, t, 0)),
            pl.BlockSpec((1, TILE, SSM_WIDTH), lambda i, t: (i, t, 0)),
            pl.BlockSpec((1, TILE // S5_BLOCK, S5_BLOCK * SSM_WIDTH), lambda i, t: (i, t, 0)),
            pl.BlockSpec((1, TILE, FNET_WIDTH), lambda i, t: (i, t, 0)),
        ],
        out_shape=[
            jax.ShapeDtypeStruct((b, r, QK_WIDTH), BF16),
            jax.ShapeDtypeStruct((b, N_HEADS, 2, LANES, r), BF16),
            jax.ShapeDtypeStruct((b, N_HEADS, r, 2 * V_DIM), BF16),
            jax.ShapeDtypeStruct((b, r, SSM_WIDTH), F32),
            jax.ShapeDtypeStruct((b, r // S5_BLOCK, S5_BLOCK * SSM_WIDTH), F32),
            jax.ShapeDtypeStruct((b, r, FNET_WIDTH), F32),
        ],
        scratch_shapes=[pltpu.VMEM((SSM_WIDTH // LANES, TILE, LANES), F32)],
        compiler_params=_params(("parallel", "parallel")),
        name="in_projection",
    )(xs, mods.reshape(depth, SUBLANES, 1, 6 * d), g_norm1.reshape(depth, 1, d), w_in,
      gq, gk, e_mat, cos_t, sin_t)


def _attn_kernel(q_ref, kt_ref, ve_ref, lam_ref, gs_ref, o_ref, *, nt, ctx_len):
    t = pl.program_id(2)
    q = q_ref[0]
    r = ve_ref.shape[2]

    def attend(k0):
        outs = []
        ve = ve_ref[0, 0, k0:, :]
        scores = [jnp.dot(q, kt_ref[0, 0, mp, :, k0:], preferred_element_type=F32)
                  for mp in range(2)]
        for s in scores:
            p = jnp.exp2(s - jnp.max(s, axis=1, keepdims=True))
            acc = jnp.dot(p.astype(BF16), ve, preferred_element_type=F32)
            outs.append(acc[:, :V_DIM] / acc[:, V_DIM:])
        lp = lam_ref[0]
        s1 = jnp.sum(lp[0:1] * lp[1:2], axis=-1, keepdims=True)
        s2 = jnp.sum(lp[2:3] * lp[3:4], axis=-1, keepdims=True)
        lam = jnp.exp(s1) - jnp.exp(s2) + lp[4:5, 0:1]
        a = outs[0] - lam * outs[1]
        a = a * lax.rsqrt(jnp.mean(a * a, axis=-1, keepdims=True) + EPS)
        o_ref[0] = (a * gs_ref[0] * lp[4:5, 1:2]).astype(BF16)

    @pl.when(t < nt - 1)
    def _latent_queries():
        attend(0)

    @pl.when(t == nt - 1)
    def _context_queries():
        attend(r - ctx_len)


def _attention(layer, q, kt, ve, lam_p, g_subln):
    b, r, _ = q.shape
    nt = r // TILE
    return pl.pallas_call(
        functools.partial(_attn_kernel, nt=nt, ctx_len=TILE),
        grid=(b, N_HEADS, nt),
        in_specs=[
            pl.BlockSpec((1, TILE, LANES), lambda i, h, t: (i, t, h)),
            pl.BlockSpec((1, 1, 2, LANES, r), lambda i, h, t: (i, h, 0, 0, 0),
                         pipeline_mode=pl.Buffered(1)),
            pl.BlockSpec((1, 1, r, 2 * V_DIM), lambda i, h, t: (i, h, 0, 0),
                         pipeline_mode=pl.Buffered(1)),
            _const_spec((SUBLANES, LANES), layer),
            _const_spec((1, V_DIM), layer),
        ],
        out_specs=pl.BlockSpec((1, TILE, V_DIM), lambda i, h, t: (i, t, h)),
        out_shape=jax.ShapeDtypeStruct((b, r, ATTN_WIDTH), BF16),
        compiler_params=_params(("parallel", "parallel", "arbitrary")),
        name="diff_attention",
    )(q, kt, ve, lam_p, g_subln)


def _zoh(are_ref, aim_ref, ldt_ref):
    a_re = are_ref[0, 0]
    a_im = aim_ref[0, 0]
    dt = jnp.exp(ldt_ref[0, 0])
    mag = jnp.exp(dt * a_re)
    ang = dt * a_im
    ab_re = mag * jnp.cos(ang)
    ab_im = mag * jnp.sin(ang)
    den = a_re * a_re + a_im * a_im
    n_re = ab_re - 1.0
    f_re = (n_re * a_re + ab_im * a_im) / den
    f_im = (ab_im * a_re - n_re * a_im) / den
    return ab_re, ab_im, f_re, f_im


def _complex_powers(base_re, base_im, n):
    pows = [(jnp.ones_like(base_re), jnp.zeros_like(base_im))]
    for _ in range(n):
        pr, pi = pows[-1]
        pows.append((pr * base_re - pi * base_im, pr * base_im + pi * base_re))
    return pows


def _fill_scan_tables(pw_ref, base_re, base_im, reverse):
    shape = (SUBLANES, SSM_STATE)
    pows = _complex_powers(base_re, base_im, SUBLANES)
    row = lax.broadcasted_iota(jnp.int32, shape, 0)
    zero = jnp.zeros(shape, F32)
    for idx, k in enumerate((1, 2, 4)):
        mask = (row + k <= SUBLANES - 1) if reverse else (row >= k)
        pw_ref[2 * idx] = jnp.where(mask, jnp.broadcast_to(pows[k][0], shape), zero)
        pw_ref[2 * idx + 1] = jnp.where(mask, jnp.broadcast_to(pows[k][1], shape), zero)
    pcr = zero
    pci = zero
    for tt in range(SUBLANES):
        e = (SUBLANES - tt) if reverse else tt + 1
        pcr = jnp.where(row == tt, jnp.broadcast_to(pows[e][0], shape), pcr)
        pci = jnp.where(row == tt, jnp.broadcast_to(pows[e][1], shape), pci)
    pw_ref[6] = pcr
    pw_ref[7] = pci


def _scan_rows(s_ref, pw_ref, carry, nrows, reverse, exclusive, groups=None):
    ns = SSM_STATE
    shape = (SUBLANES, ns)
    ngroups = nrows // SUBLANES
    edge = (SUBLANES - 1) if reverse else 0

    def group(g, carry):
        cr, ci = carry
        gi = (ngroups - 1 - g) if reverse else g
        r0 = pl.multiple_of(gi * SUBLANES, SUBLANES)
        xr = s_ref[pl.ds(r0, SUBLANES), :ns]
        xi = s_ref[pl.ds(r0, SUBLANES), ns:]
        for idx, k in enumerate((1, 2, 4)):
            sh = (SUBLANES - k) if reverse else k
            sr = pltpu.roll(xr, sh, 0)
            si = pltpu.roll(xi, sh, 0)
            pr = pw_ref[2 * idx]
            pi = pw_ref[2 * idx + 1]
            xr, xi = xr + pr * sr - pi * si, xi + pr * si + pi * sr
        pr = pw_ref[6]
        pi = pw_ref[7]
        xr, xi = xr + pr * cr - pi * ci, xi + pr * ci + pi * cr
        if exclusive:
            row = lax.broadcasted_iota(jnp.int32, shape, 0)
            sh = (SUBLANES - 1) if reverse else 1
            er = jnp.where(row == edge, cr, pltpu.roll(xr, sh, 0))
            ei = jnp.where(row == edge, ci, pltpu.roll(xi, sh, 0))
        else:
            er, ei = xr, xi
        s_ref[pl.ds(r0, SUBLANES), :ns] = er
        s_ref[pl.ds(r0, SUBLANES), ns:] = ei
        last = 0 if reverse else SUBLANES - 1
        return (jnp.broadcast_to(xr[last:last + 1], shape),
                jnp.broadcast_to(xi[last:last + 1], shape))

    if groups is None:
        return lax.fori_loop(0, ngroups, group, carry)
    for g in groups:
        carry = group(g, carry)
    return carry


def _s5_ctx_kernel(u_ref, are_ref, aim_ref, ldt_ref, bre_ref, bim_ref, c_ref,
                   o_ref, st_ref, bbar_ref, pw_ref, s_ref, y_ref, *, reverse):
    ns = SSM_STATE
    ab_re, ab_im, f_re, f_im = _zoh(are_ref, aim_ref, ldt_ref)
    bre = bre_ref[0, 0]
    bim = bim_ref[0, 0]
    bbar_ref[:, :ns] = (f_re * bre - f_im * bim).astype(BF16)
    bbar_ref[:, ns:] = (f_re * bim + f_im * bre).astype(BF16)
    _fill_scan_tables(pw_ref, ab_re, ab_im, reverse)
    s_ref[...] = jnp.dot(u_ref[0].astype(BF16), bbar_ref[...], preferred_element_type=F32)
    zero = jnp.zeros((SUBLANES, ns), F32)
    cr, ci = _scan_rows(s_ref, pw_ref, (zero, zero), TILE, reverse, exclusive=False)
    st_ref[0, 0] = cr
    st_ref[0, 1] = ci
    y = jnp.dot(s_ref[...].astype(BF16), c_ref[0, 0], preferred_element_type=F32)
    _rows_to_blocked(y, y_ref, o_ref.at[0], S5_BLOCK, tile_major=True)


def _s5_block_kernel(u8_ref, s0_ref, are_ref, aim_ref, ldt_ref, bre_ref, bim_ref, ctre_ref,
                     ctim_ref, o_ref, wx_ref, wc_ref, wt_ref, pw_ref, carry_ref, s_ref,
                     *, reverse):
    ns = SSM_STATE
    w = SSM_WIDTH
    nb = S5_BLOCK
    nh = w // LANES
    hs = ns // nh
    hw = nb * LANES
    c = pl.program_id(0)
    b = pl.program_id(1)

    @pl.when((c == 0) & (b == 0))
    def _build_maps():
        ab_re, ab_im, f_re, f_im = _zoh(are_ref, aim_ref, ldt_ref)
        bre = bre_ref[0, 0]
        bim = bim_ref[0, 0]
        bb_re = f_re * bre - f_im * bim
        bb_im = f_re * bim + f_im * bre
        pows = _complex_powers(ab_re, ab_im, nb)
        ctre = ctre_ref[0, 0]
        ctim = ctim_ref[0, 0]
        ct_stack = jnp.concatenate([ctre, -ctim], axis=1).astype(BF16)
        chan = [slice(h * LANES, (h + 1) * LANES) for h in range(nh)]
        stat = [slice(h * hs, (h + 1) * hs) for h in range(nh)]
        tok = [slice(t * LANES, (t + 1) * LANES) for t in range(nb)]
        taps = []
        for e in range(nb):
            pr, pi = pows[e]
            xr = pr * bb_re - pi * bb_im
            xi = pr * bb_im + pi * bb_re
            i = e if reverse else nb - 1 - e
            for h in range(nh):
                wx_ref[h, tok[i], :hs] = xr[chan[h], stat[h]].astype(BF16)
                wx_ref[h, tok[i], hs:] = xi[chan[h], stat[h]].astype(BF16)
            xk = jnp.concatenate([xr, xi], axis=1).astype(BF16)
            taps.append(lax.dot_general(xk, ct_stack, (((1,), (1,)), ((), ())),
                                        preferred_element_type=F32).astype(BF16))
        zero_blk = jnp.zeros((LANES, LANES), BF16)
        for i in range(nb):
            for j in range(nb):
                lag = (i - j) if reverse else (j - i)
                for h in range(nh):
                    wt_ref[h, tok[i], tok[j]] = taps[lag][chan[h], chan[h]] if lag >= 0 else zero_blk
        for j in range(nb):
            pr, pi = pows[nb - j] if reverse else pows[j + 1]
            c_re = ctre * pr - ctim * pi
            c_im = -(ctre * pi + ctim * pr)
            for h in range(nh):
                wc_ref[h, :hs, tok[j]] = c_re[chan[h], stat[h]].T.astype(BF16)
                wc_ref[h, hs:, tok[j]] = c_im[chan[h], stat[h]].T.astype(BF16)
        _fill_scan_tables(pw_ref, pows[nb][0], pows[nb][1], reverse)

    @pl.when(c == 0)
    def _load_state():
        carry_ref[b] = s0_ref[0]

    u8 = u8_ref[0].astype(BF16)
    for h in range(nh):
        x = jnp.dot(u8[:, h * hw:(h + 1) * hw], wx_ref[h], preferred_element_type=F32)
        s_ref[:, h * hs:(h + 1) * hs] = x[:, :hs]
        s_ref[:, ns + h * hs:ns + (h + 1) * hs] = x[:, hs:]
    carry = (carry_ref[b, 0], carry_ref[b, 1])
    pairs = nb // 2
    per = (S5_CHUNK_ROWS // SUBLANES) // (nh * pairs)
    for h in range(nh):
        for jp in range(pairs):
            j0 = 2 * jp
            rows = slice(j0 * LANES, hw) if reverse else slice(0, (j0 + 2) * LANES)
            cols = slice(j0 * LANES, (j0 + 2) * LANES)
            o_ref[0, :, h * hw + cols.start:h * hw + cols.stop] = jnp.dot(
                u8[:, h * hw + rows.start:h * hw + rows.stop], wt_ref[h, rows, cols],
                preferred_element_type=F32)
            k = h * pairs + jp
            carry = _scan_rows(s_ref, pw_ref, carry, S5_CHUNK_ROWS, reverse, exclusive=True,
                               groups=range(k * per, (k + 1) * per))
    carry_ref[b, 0] = carry[0]
    carry_ref[b, 1] = carry[1]
    for h in range(nh):
        y = (jnp.dot(s_ref[:, h * hs:(h + 1) * hs].astype(BF16), wc_ref[h, :hs],
                     preferred_element_type=F32)
             + jnp.dot(s_ref[:, ns + h * hs:ns + (h + 1) * hs].astype(BF16), wc_ref[h, hs:],
                       preferred_element_type=F32))
        o_ref[0, :, h * hw:(h + 1) * hw] = o_ref[0, :, h * hw:(h + 1) * hw] + y


def _s5_direction(layer, direction, u, u8, a_re, a_im, ldt, b_re, b_im, ct_re, ct_im, c_stack):
    b, r, w = u.shape
    nt = r // TILE
    rows = r // S5_BLOCK
    wide = S5_BLOCK * w
    tile_rows = TILE // S5_BLOCK
    ns = SSM_STATE
    reverse = direction == 1

    def dir_spec(shape):
        nd = len(shape)
        return pl.BlockSpec((1, 1) + tuple(shape), lambda *_: (layer, direction) + (0,) * nd,
                            pipeline_mode=pl.Buffered(1))

    y_ctx, state = pl.pallas_call(
        functools.partial(_s5_ctx_kernel, reverse=reverse),
        grid=(b,),
        in_specs=[pl.BlockSpec((1, TILE, w), lambda i: (i, nt - 1, 0)),
                  dir_spec((1, ns)), dir_spec((1, ns)), dir_spec((1, ns)),
                  dir_spec((w, ns)), dir_spec((w, ns)), dir_spec((2 * ns, w))],
        out_specs=[pl.BlockSpec((1, tile_rows, wide), lambda i: (i, 0, 0)),
                   pl.BlockSpec((1, 2, SUBLANES, ns), lambda i: (i, 0, 0, 0))],
        out_shape=[jax.ShapeDtypeStruct((b, tile_rows, wide), F32),
                   jax.ShapeDtypeStruct((b, 2, SUBLANES, ns), F32)],
        scratch_shapes=[pltpu.VMEM((w, 2 * ns), BF16),
                        pltpu.VMEM((8, SUBLANES, ns), F32),
                        pltpu.VMEM((TILE, 2 * ns), F32),
                        pltpu.VMEM((w // LANES, TILE, LANES), F32)],
        compiler_params=_params(("parallel",)),
        name="s5_context",
    )(u, a_re, a_im, ldt, b_re, b_im, c_stack)

    nchunks = (r - TILE) // (S5_BLOCK * S5_CHUNK_ROWS)

    def chunk(c, i):
        return (i, nchunks - 1 - c if reverse else c, 0)

    y_lat = pl.pallas_call(
        functools.partial(_s5_block_kernel, reverse=reverse),
        grid=(nchunks, b),
        in_specs=[pl.BlockSpec((1, S5_CHUNK_ROWS, wide), chunk),
                  pl.BlockSpec((1, 2, SUBLANES, ns), lambda c, i: (i, 0, 0, 0)),
                  dir_spec((1, ns)), dir_spec((1, ns)), dir_spec((1, ns)),
                  dir_spec((w, ns)), dir_spec((w, ns)), dir_spec((w, ns)), dir_spec((w, ns))],
        out_specs=pl.BlockSpec((1, S5_CHUNK_ROWS, wide), chunk),
        out_shape=jax.ShapeDtypeStruct((b, rows - tile_rows, wide), F32),
        scratch_shapes=[pltpu.VMEM((w // LANES, wide // (w // LANES), 2 * ns // (w // LANES)), BF16),
                        pltpu.VMEM((w // LANES, 2 * ns // (w // LANES), wide // (w // LANES)), BF16),
                        pltpu.VMEM((w // LANES, wide // (w // LANES), wide // (w // LANES)), BF16),
                        pltpu.VMEM((8, SUBLANES, ns), F32),
                        pltpu.VMEM((b, 2, SUBLANES, ns), F32),
                        pltpu.VMEM((S5_CHUNK_ROWS, 2 * ns), F32)],
        compiler_params=_params(("arbitrary", "arbitrary"), vmem=S5_VMEM_LIMIT),
        name="s5_blocks_reverse" if reverse else "s5_blocks_forward",
    )(u8, state, a_re, a_im, ldt, b_re, b_im, ct_re, ct_im)
    return y_lat, y_ctx


def _fourier_outer_kernel(w_ref, x_ref, g_ref):
    g_ref[0] = jnp.dot(w_ref[...], x_ref[0].astype(BF16),
                       preferred_element_type=F32).astype(BF16)


def _fourier_outer(f, w_outer):
    b, r, w = f.shape
    nt = r // TILE
    l2n = nt - 1
    flat = TILE * w
    return pl.pallas_call(
        _fourier_outer_kernel,
        grid=(b, flat // FOURIER_CHUNK),
        in_specs=[_const_spec((2 * l2n, l2n)),
                  pl.BlockSpec((1, l2n, FOURIER_CHUNK), lambda i, j: (i, 0, j))],
        out_specs=pl.BlockSpec((1, 2 * l2n, FOURIER_CHUNK), lambda i, j: (i, 0, j)),
        out_shape=jax.ShapeDtypeStruct((b, 2 * l2n, flat), BF16),
        compiler_params=_params(("parallel", "parallel")),
        name="fourier_outer",
    )(w_outer, f.reshape(b, nt, flat))


def _fourier_tail(y_re, y_im, c64_ref, wf_ref):
    w = FNET_WIDTH
    z = (jnp.dot(y_re.astype(BF16), c64_ref[:w], preferred_element_type=F32)
         + jnp.dot(y_im.astype(BF16), c64_ref[w:], preferred_element_type=F32))
    return jnp.dot(z.astype(BF16), wf_ref[0], preferred_element_type=F32)


def _fourier_inner_kernel(gr_ref, gi_ref, m_ref, c64_ref, wf_ref, o_ref):
    w = FNET_WIDTH
    ys = [jnp.dot(m_ref[i, :, :TILE], gr_ref[0, i], preferred_element_type=F32)
          + jnp.dot(m_ref[i, :, TILE:], gi_ref[0, i], preferred_element_type=F32)
          for i in range(FOURIER_GROUP)]
    for i, y in enumerate(ys):
        o_ref[0, :, i * w:(i + 1) * w] = _fourier_tail(y[:TILE], y[TILE:], c64_ref, wf_ref)


def _fourier_inner(layer, g, m_tab, c64_tab, wf_blk):
    b, two_l2n, flat = g.shape
    l2n = two_l2n // 2
    w = flat // TILE
    steps = l2n // FOURIER_GROUP
    g4 = g.reshape(b, two_l2n, TILE, w)
    return pl.pallas_call(
        _fourier_inner_kernel,
        grid=(b, steps),
        in_specs=[
            pl.BlockSpec((1, FOURIER_GROUP, TILE, w), lambda i, j: (i, j, 0, 0)),
            pl.BlockSpec((1, FOURIER_GROUP, TILE, w), lambda i, j: (i, steps + j, 0, 0)),
            pl.BlockSpec((FOURIER_GROUP, 2 * TILE, 2 * TILE), lambda i, j: (j, 0, 0)),
            _const_spec((2 * w, w)),
            _const_spec((w, w), layer),
        ],
        out_specs=pl.BlockSpec((1, TILE, FOURIER_GROUP * w), lambda i, j: (i, 0, j)),
        out_shape=jax.ShapeDtypeStruct((b, TILE, l2n * w), F32),
        compiler_params=_params(("parallel", "parallel")),
        name="fourier_inner",
    )(g4, g4, m_tab, c64_tab, wf_blk)


def _fourier_ctx_kernel(x_ref, m_ref, c64_ref, wf_ref, o_ref, z_ref, *, l2n):
    y = jnp.dot(m_ref[...], x_ref[0].astype(BF16), preferred_element_type=F32)
    z = _fourier_tail(y[:TILE], y[TILE:], c64_ref, wf_ref)
    _rows_to_blocked(z, z_ref, o_ref.at[0], l2n)


def _fourier_context(layer, f, m_ctx, c64_tab, wf_blk):
    b, r, w = f.shape
    nt = r // TILE
    l2n = nt - 1
    return pl.pallas_call(
        functools.partial(_fourier_ctx_kernel, l2n=l2n),
        grid=(b,),
        in_specs=[
            pl.BlockSpec((1, TILE, w), lambda i: (i, nt - 1, 0)),
            _const_spec((2 * TILE, TILE)),
            _const_spec((2 * w, w)),
            _const_spec((w, w), layer),
        ],
        out_specs=pl.BlockSpec((1, TILE // l2n, l2n * w), lambda i: (i, 0, 0)),
        out_shape=jax.ShapeDtypeStruct((b, TILE // l2n, l2n * w), F32),
        scratch_shapes=[pltpu.VMEM((w // LANES, TILE, LANES), F32)],
        compiler_params=_params(("parallel",)),
        name="fourier_context",
    )(f, m_ctx, c64_tab, wf_blk)


def _outffn_kernel(x_ref, a_ref, yf_ref, yb_ref, yfc_ref, ybc_ref, u_ref, fn_ref, fnc_ref, mod_ref,
                   dsk_ref, wglu_ref, bglu_ref, wo_ref, g2_ref, w1_ref, w2_ref, o_ref, ytok_ref,
                   ftok_ref, *, l2n, nt):
    d = D_MODEL
    x = x_ref[0]
    mod = mod_ref[0, 0]
    o1 = ATTN_WIDTH
    o2 = ATTN_WIDTH + SSM_WIDTH
    is_ctx = pl.program_id(1) == nt - 1
    y_blk = jnp.where(is_ctx, yfc_ref[0] + ybc_ref[0], yf_ref[0] + yb_ref[0])
    y_scan = _blocked_to_rows(y_blk, ytok_ref, S5_BLOCK, tile_major=True)
    fnet = _blocked_to_rows(jnp.where(is_ctx, fnc_ref[0], fn_ref[0]), ftok_ref, l2n)
    hg = jax.nn.gelu(y_scan + dsk_ref[0] * u_ref[0])
    zg = jnp.dot(hg.astype(BF16), wglu_ref[0], preferred_element_type=F32) + bglu_ref[0]
    s5 = (hg * jax.nn.sigmoid(zg)).astype(BF16)
    mix = (jnp.dot(a_ref[0], wo_ref[0, :o1], preferred_element_type=F32)
           + jnp.dot(s5, wo_ref[0, o1:o2], preferred_element_type=F32)
           + jnp.dot(fnet.astype(BF16), wo_ref[0, o2:], preferred_element_type=F32))
    x1 = x + mod[:, 2 * d:3 * d] * mix
    ms = jnp.mean(x1 * x1, axis=-1, keepdims=True)
    hn = x1 * lax.rsqrt(ms + EPS) * g2_ref[0]
    h = (hn * (1.0 + mod[:, 4 * d:5 * d]) + mod[:, 3 * d:4 * d]).astype(BF16)
    acc = jnp.zeros((TILE, d), F32)
    for cc in range(D_FF // FF_CHUNK):
        sl = slice(cc * FF_CHUNK, (cc + 1) * FF_CHUNK)
        t = jnp.dot(h, w1_ref[0, :, sl], preferred_element_type=F32)
        t = jnp.square(jnp.maximum(t, 0.0))
        acc = acc + jnp.dot(t.astype(BF16), w2_ref[0, sl, :], preferred_element_type=F32)
    o_ref[0] = x1 + mod[:, 5 * d:] * acc


def _out_ffn(layer, xs, attn, y_fwd, y_bwd, u, fn, fn_ctx, mods, d_skip, w_glu, b_glu, w_out,
             g_norm2, w_ff1, w_ff2, latent_only):
    b, r, d = xs.shape
    nt = r // TILE
    l2n = nt - 1
    depth = w_out.shape[0]
    s5_shape = (1, TILE // S5_BLOCK, S5_BLOCK * SSM_WIDTH)
    fn_shape = (1, TILE // l2n, l2n * FNET_WIDTH)

    def latent(shape):
        return pl.BlockSpec(shape, lambda i, t: (i, jnp.minimum(t, nt - 2), 0))

    def context(shape):
        return pl.BlockSpec(shape, lambda i, t: (i, 0, 0))

    def tile(width):
        return pl.BlockSpec((1, TILE, width), lambda i, t: (i, t, 0))

    return pl.pallas_call(
        functools.partial(_outffn_kernel, l2n=l2n, nt=nt),
        grid=(b, nt - 1 if latent_only else nt),
        in_specs=[
            tile(d), tile(ATTN_WIDTH), latent(s5_shape), latent(s5_shape), context(s5_shape),
            context(s5_shape), tile(SSM_WIDTH), latent(fn_shape), context(fn_shape),
            pl.BlockSpec((1, 1, 1, 6 * d),
                         lambda i, t: (layer, jnp.where(t == nt - 1, b, i), 0, 0)),
            _const_spec((1, SSM_WIDTH), layer),
            _const_spec((SSM_WIDTH, SSM_WIDTH), layer),
            _const_spec((1, SSM_WIDTH), layer),
            _const_spec((d, d), layer),
            _const_spec((1, d), layer),
            _const_spec((d, D_FF), layer),
            _const_spec((D_FF, d), layer),
        ],
        out_specs=tile(d),
        out_shape=jax.ShapeDtypeStruct((b, r - TILE if latent_only else r, d), F32),
        scratch_shapes=[pltpu.VMEM((SSM_WIDTH // LANES, TILE, LANES), F32),
                        pltpu.VMEM((FNET_WIDTH // LANES, TILE, LANES), F32)],
        input_output_aliases={} if latent_only else {0: 0},
        compiler_params=_params(("parallel", "parallel")),
        name="out_ffn",
    )(xs, attn, y_fwd[0], y_bwd[0], y_fwd[1], y_bwd[1], u, fn, fn_ctx,
      mods.reshape(depth, SUBLANES, 1, 6 * d), d_skip, w_glu, b_glu,
      w_out, g_norm2.reshape(depth, 1, d), w_ff1, w_ff2)


def _head_lane_tables():
    j = np.arange(LANES)
    half = j // 64
    mp = (j // 32) % 2
    idx = j % 32
    src_in_head = mp * HEAD_DIM + half * 32 + idx
    gain_idx = half * 32 + idx
    return src_in_head, gain_idx, mp


def _in_proj_column_order():
    src_in_head, _, _ = _head_lane_tables()
    qk = np.concatenate([h * LANES + src_in_head for h in range(N_HEADS)])
    return np.concatenate([qk, QK_WIDTH + qk, np.arange(2 * QK_WIDTH, IN_WIDTH)])


def _same_map_matrix():
    i = np.arange(256)
    head = i // LANES
    mp = ((i % LANES) // 32) % 2
    same = (head[:, None] == head[None, :]) & (mp[:, None] == mp[None, :])
    return same.astype(np.float32)


def _rope_tables(seq, ctx_len):
    t = jnp.arange(seq)
    row = (t // GRID_W).astype(F32)
    col = (t % GRID_W).astype(F32)
    n_freq = HEAD_DIM // 4
    inv = jnp.power(ROPE_BASE, -jnp.arange(n_freq, dtype=F32) / n_freq)
    ang = jnp.concatenate([row[:, None] * inv, col[:, None] * inv], axis=-1)
    cos = jnp.tile(jnp.cos(ang), (1, 4))
    sign = np.where(np.arange(LANES) < LANES // 2, -1.0, 1.0).astype(np.float32)
    sin = jnp.tile(jnp.sin(ang), (1, 4)) * sign
    cos = jnp.concatenate([cos, jnp.ones((ctx_len, LANES), F32)], axis=0)
    sin = jnp.concatenate([sin, jnp.zeros((ctx_len, LANES), F32)], axis=0)
    return cos, sin


def _dft_tables(length):
    l2n = length // TILE
    k2 = np.arange(l2n)
    phi = 2.0 * np.pi * ((k2[:, None] * k2[None, :]) % l2n) / l2n
    w_outer = np.concatenate([np.cos(phi), -np.sin(phi)], axis=0).astype(np.float32)
    l1 = np.arange(TILE)
    alpha = 2.0 * np.pi * ((k2[:, None] * l1[None, :]) % length) / length
    beta = 2.0 * np.pi * ((l1[:, None] * l1[None, :]) % TILE) / TILE
    ca = jnp.asarray(np.cos(alpha).astype(np.float32))[:, None, :]
    sa = jnp.asarray(np.sin(alpha).astype(np.float32))[:, None, :]
    cb = jnp.asarray(np.cos(beta).astype(np.float32))[None]
    sb = jnp.asarray(np.sin(beta).astype(np.float32))[None]
    norm = 1.0 / math.sqrt(length)
    ct = (ca * cb - sa * sb) * norm
    st = (sa * cb + ca * sb) * norm
    m = jnp.concatenate([jnp.concatenate([ct, st], axis=2),
                         jnp.concatenate([-st, ct], axis=2)], axis=1).astype(BF16)
    return jnp.asarray(w_outer).astype(BF16), m


def _tile_dft_table():
    l1 = np.arange(TILE)
    beta = 2.0 * np.pi * ((l1[:, None] * l1[None, :]) % TILE) / TILE
    tab = np.concatenate([np.cos(beta), -np.sin(beta)], axis=0) / math.sqrt(TILE)
    return jnp.asarray(tab.astype(np.float32)).astype(BF16)


def _channel_dft_table():
    c = np.arange(FNET_C)
    th = 2.0 * np.pi * ((c[:, None] * c[None, :]) % FNET_C) / FNET_C
    eye = np.eye(FNET_G)
    norm = 1.0 / math.sqrt(FNET_C)
    cblk = np.kron(eye, np.cos(th)) * norm
    sblk = np.kron(eye, np.sin(th)) * norm
    return jnp.asarray(np.concatenate([cblk, sblk], axis=0).astype(np.float32)).astype(BF16)


def _block_diag(blocks):
    g = blocks.shape[-3]
    eye = jnp.eye(g, dtype=blocks.dtype)
    out = jnp.einsum('...gab,gh->...gahb', blocks, eye)
    return out.reshape(blocks.shape[:-3] + (g * blocks.shape[-2], g * blocks.shape[-1]))


def kernel(x, c, ctx, c_ctx, w_mod, b_mod, g_norm1, w_in, g_qnorm, g_knorm, lam_q1, lam_k1, lam_q2, lam_k2, g_subln, ssm_a_re, ssm_a_im, ssm_log_dt, ssm_b_re, ssm_b_im, ssm_c_re, ssm_c_im, ssm_d, w_glu, b_glu, w_fnet, w_out, g_norm2, w_ff1, w_ff2):
    bsz, seq, d = x.shape
    ctx_len = ctx.shape[1]
    depth = w_mod.shape[0]
    assert d == D_MODEL and ctx_len == TILE and seq % TILE == 0 and seq % GRID_W == 0
    assert bsz + 1 <= SUBLANES

    assert (seq // TILE) % FOURIER_GROUP == 0 and seq % (S5_BLOCK * S5_CHUNK_ROWS) == 0
    xs = jnp.concatenate([x, ctx], axis=1)
    act = jnp.concatenate([c, c_ctx[None], jnp.zeros((SUBLANES - bsz - 1, d), F32)], axis=0)
    mods = _modulation(act, w_mod, b_mod)

    w_in_p = jnp.take(w_in, jnp.asarray(_in_proj_column_order()), axis=2).astype(BF16)
    _, gain_idx, _ = _head_lane_tables()
    gq = jnp.tile(g_qnorm[:, gain_idx], (1, N_HEADS)).reshape(depth, 1, QK_WIDTH)
    gk = jnp.tile(g_knorm[:, gain_idx], (1, N_HEADS)).reshape(depth, 1, QK_WIDTH)
    e_mat = jnp.asarray(_same_map_matrix()).astype(BF16)
    cos_t, sin_t = _rope_tables(seq, ctx_len)
    lam_init = np.array([0.8 - 0.6 * math.exp(-0.3 * i) for i in range(depth)], np.float32)
    pad = jnp.zeros((depth, LANES - HEAD_DIM), F32)
    lam_rows = [jnp.concatenate([v, pad], axis=1) for v in (lam_q1, lam_k1, lam_q2, lam_k2)]
    const_row = np.zeros((depth, LANES), np.float32)
    const_row[:, 0] = lam_init
    const_row[:, 1] = 1.0 - lam_init
    lam_p = jnp.stack(lam_rows + [jnp.asarray(const_row)]
                      + [jnp.zeros((depth, LANES), F32)] * (SUBLANES - 5), axis=1)
    g_sub = g_subln.reshape(depth, 1, V_DIM)

    a_re = ssm_a_re.reshape(depth, 2, 1, SSM_STATE)
    a_im = ssm_a_im.reshape(depth, 2, 1, SSM_STATE)
    ldt = jnp.repeat(ssm_log_dt, SSM_N, axis=-1).reshape(depth, 2, 1, SSM_STATE)
    b_re = _block_diag(jnp.swapaxes(ssm_b_re, -1, -2))
    b_im = _block_diag(jnp.swapaxes(ssm_b_im, -1, -2))
    c_stack = jnp.concatenate([_block_diag(jnp.swapaxes(ssm_c_re, -1, -2)),
                               -_block_diag(jnp.swapaxes(ssm_c_im, -1, -2))],
                              axis=2).astype(BF16)
    ct_re = _block_diag(ssm_c_re)
    ct_im = _block_diag(ssm_c_im)
    d_skip = ssm_d.reshape(depth, 1, SSM_WIDTH)
    w_glu_b = w_glu.astype(BF16)
    b_glu_r = b_glu.reshape(depth, 1, SSM_WIDTH)

    w_outer, m_lat = _dft_tables(seq)
    m_ctx = _tile_dft_table()
    c64_tab = _channel_dft_table()
    wf_blk = _block_diag(w_fnet).astype(BF16)

    w_out_b = w_out.astype(BF16)
    w_ff1_b = w_ff1.astype(BF16)
    w_ff2_b = w_ff2.astype(BF16)

    for layer in range(depth):
        q, kt, ve, u, u8, f = _in_projection(layer, xs, mods, g_norm1, w_in_p, gq, gk, e_mat,
                                         cos_t, sin_t)
        attn = _attention(layer, q, kt, ve, lam_p, g_sub)
        y_fwd, y_bwd = [_s5_direction(layer, dr, u, u8, a_re, a_im, ldt, b_re, b_im, ct_re,
                                      ct_im, c_stack) for dr in range(2)]
        fn = _fourier_inner(layer, _fourier_outer(f, w_outer), m_lat, c64_tab, wf_blk)
        fn_ctx = _fourier_context(layer, f, m_ctx, c64_tab, wf_blk)
        xs = _out_ffn(layer, xs, attn, y_fwd, y_bwd, u, fn, fn_ctx, mods, d_skip, w_glu_b, b_glu_r,
                      w_out_b, g_norm2, w_ff1_b, w_ff2_b, latent_only=layer == depth - 1)
    return xs
```

```python
import functools
import math

import numpy as np
import jax
import jax.numpy as jnp
from jax import lax
from jax.experimental import pallas as pl
from jax.experimental.pallas import tpu as pltpu

F32 = jnp.float32
BF16 = jnp.bfloat16

D_MODEL = 1024
GRID_W = 64
N_HEADS = 4
HEAD_DIM = 64
V_DIM = 2 * HEAD_DIM
QK_WIDTH = N_HEADS * 2 * HEAD_DIM
ATTN_WIDTH = N_HEADS * V_DIM
SSM_WIDTH = D_MODEL // 4
SSM_P = 16
SSM_G = SSM_WIDTH // SSM_P
SSM_N = 64
SSM_STATE = SSM_G * SSM_N
FNET_WIDTH = D_MODEL // 4
FNET_G = 4
FNET_C = FNET_WIDTH // FNET_G
IN_WIDTH = 2 * QK_WIDTH + ATTN_WIDTH + SSM_WIDTH + FNET_WIDTH
D_FF = 4 * D_MODEL
ROPE_BASE = 10000.0
EPS = 1e-6
SCALE = HEAD_DIM ** -0.5
LOG2E = math.log2(math.e)

TILE = 256
LANES = 128
SUBLANES = 8
FF_CHUNK = 1024
MOD_BLOCK = 1536
FOURIER_CHUNK = 8192
FOURIER_GROUP = 4
S5_BLOCK = 8
S5_CHUNK_ROWS = 256
VMEM_LIMIT = 48 * 1024 * 1024
S5_VMEM_LIMIT = 56 * 1024 * 1024

def _const_spec(shape, layer=None):
    nd = len(shape)
    if layer is None:
        return pl.BlockSpec(shape, lambda *_: (0,) * nd, pipeline_mode=pl.Buffered(1))
    return pl.BlockSpec((1,) + tuple(shape), lambda *_: (layer,) + (0,) * nd,
                        pipeline_mode=pl.Buffered(1))


def _params(sem, vmem=None):
    return pltpu.CompilerParams(dimension_semantics=sem, vmem_limit_bytes=vmem or VMEM_LIMIT)


def _blocked_lane(j, h, period, nh, tile_major):
    return ((h * period + j) if tile_major else (j * nh + h)) * LANES


def _rows_to_blocked(tok, scr_ref, blk_ref, period, tile_major=False):
    n, w = tok.shape
    nh = w // LANES
    for h in range(nh):
        scr_ref[h] = tok[:, h * LANES:(h + 1) * LANES]
    for j in range(period):
        for h in range(nh):
            lo = _blocked_lane(j, h, period, nh, tile_major)
            blk_ref[:, lo:lo + LANES] = scr_ref[h, pl.ds(j, n // period, stride=period), :]


def _blocked_to_rows(blk, scr_ref, period, tile_major=False):
    nh, n, _ = scr_ref.shape
    for j in range(period):
        for h in range(nh):
            lo = _blocked_lane(j, h, period, nh, tile_major)
            scr_ref[h, pl.ds(j, n // period, stride=period), :] = blk[:, lo:lo + LANES]
    return jnp.concatenate([scr_ref[h] for h in range(nh)], axis=1)


def _mod_kernel(act_ref, w_ref, b_ref, o_ref):
    a = act_ref[...]
    a = a * jax.nn.sigmoid(a)
    o_ref[0] = jnp.dot(a.astype(BF16), w_ref[0].astype(BF16),
                       preferred_element_type=F32) + b_ref[0]


def _modulation(act, w_mod, b_mod):
    depth, d, n = w_mod.shape
    bn = MOD_BLOCK
    return pl.pallas_call(
        _mod_kernel,
        grid=(depth, n // bn),
        in_specs=[pl.BlockSpec((SUBLANES, d), lambda l, j: (0, 0)),
                  pl.BlockSpec((1, d, bn), lambda l, j: (l, 0, j)),
                  pl.BlockSpec((1, 1, bn), lambda l, j: (l, 0, j))],
        out_specs=pl.BlockSpec((1, SUBLANES, bn), lambda l, j: (l, 0, j)),
        out_shape=jax.ShapeDtypeStruct((depth, SUBLANES, n), F32),
        compiler_params=_params(("parallel", "parallel")),
        name="modulation",
    )(act, w_mod, b_mod.reshape(depth, 1, n))


def _inproj_kernel(x_ref, mod_ref, g1_ref, w_ref, gq_ref, gk_ref, e_ref, cos_ref, sin_ref,
                   q_ref, kt_ref, ve_ref, u_ref, u8_ref, f_ref, stage_ref, *, nt):
    d = D_MODEL
    bsz = x_ref.shape[0]
    is_ctx = pl.program_id(0) == nt - 1
    e = e_ref[...]
    cos = cos_ref[...]
    sin = sin_ref[...]

    def project(i):
        x = x_ref[i]
        ms = jnp.mean(x * x, axis=-1, keepdims=True)
        xn = x * lax.rsqrt(ms + EPS) * g1_ref[0]
        mod = mod_ref[0, pl.ds(jnp.where(is_ctx, bsz, i), 1), :]
        h = xn * (1.0 + mod[:, d:2 * d]) + mod[:, :d]
        return jnp.dot(h.astype(BF16), w_ref[0], preferred_element_type=F32)

    def qk_norm(z, g):
        sq = z * z
        hi = sq.astype(BF16)
        lo = (sq - hi.astype(F32)).astype(BF16)
        parts = []
        for j in range(QK_WIDTH // 256):
            sl = slice(j * 256, (j + 1) * 256)
            parts.append(jnp.dot(hi[:, sl], e, preferred_element_type=F32)
                         + jnp.dot(lo[:, sl], e, preferred_element_type=F32))
        ssum = jnp.concatenate(parts, axis=-1)
        return z * lax.rsqrt(ssum * (1.0 / HEAD_DIM) + EPS) * g

    def rope(zh):
        return zh * cos + pltpu.roll(zh, LANES // 2, 1) * sin

    row_map = (lax.broadcasted_iota(jnp.int32, (LANES, TILE), 0) // 32) % 2
    ones = jnp.ones((TILE, LANES), BF16)

    def finish(i, proj):
        qn = qk_norm(proj[:, :QK_WIDTH], gq_ref[0])
        kn = qk_norm(proj[:, QK_WIDTH:2 * QK_WIDTH], gk_ref[0])
        for hh in range(N_HEADS):
            sl = slice(hh * LANES, (hh + 1) * LANES)
            q_ref[i, :, sl] = (rope(qn[:, sl]) * (SCALE * LOG2E)).astype(BF16)
            kt = rope(kn[:, sl]).T
            kt_ref[i, hh, 0] = jnp.where(row_map == 0, kt, 0.0).astype(BF16)
            kt_ref[i, hh, 1] = jnp.where(row_map == 1, kt, 0.0).astype(BF16)
            vo = 2 * QK_WIDTH + hh * V_DIM
            ve_ref[i, hh, :, :V_DIM] = proj[:, vo:vo + V_DIM].astype(BF16)
            ve_ref[i, hh, :, V_DIM:] = ones
        uo = 2 * QK_WIDTH + ATTN_WIDTH
        u_ref[i] = proj[:, uo:uo + SSM_WIDTH]
        _rows_to_blocked(proj[:, uo:uo + SSM_WIDTH], stage_ref, u8_ref.at[i], S5_BLOCK,
                         tile_major=True)
        f_ref[i] = proj[:, uo + SSM_WIDTH:]

    projs = [project(i) for i in range(bsz)]
    for i, proj in enumerate(projs):
        finish(i, proj)


def _in_projection(layer, xs, mods, g_norm1, w_in, gq, gk, e_mat, cos_t, sin_t):
    b, r, d = xs.shape
    nt = r // TILE
    depth = w_in.shape[0]
    return pl.pallas_call(
        functools.partial(_inproj_kernel, nt=nt),
        grid=(nt,),
        in_specs=[
            pl.BlockSpec((b, TILE, d), lambda t: (0, t, 0)),
            _const_spec((SUBLANES, 6 * d), layer),
            _const_spec((1, d), layer),
            _const_spec((d, IN_WIDTH), layer),
            _const_spec((1, QK_WIDTH), layer),
            _const_spec((1, QK_WIDTH), layer),
            _const_spec((256, 256)),
            pl.BlockSpec((TILE, LANES), lambda t: (t, 0)),
            pl.BlockSpec((TILE, LANES), lambda t: (t, 0)),
        ],
        out_specs=[
            pl.BlockSpec((b, TILE, QK_WIDTH), lambda t: (0, t, 0)),
            pl.BlockSpec((b, N_HEADS, 2, LANES, TILE), lambda t: (0, 0, 0, 0, t)),
            pl.BlockSpec((b, N_HEADS, TILE, 2 * V_DIM), lambda t: (0, 0, t, 0)),
            pl.BlockSpec((b, TILE, SSM_WIDTH), lambda t: (0, t, 0)),
            pl.BlockSpec((b, TILE // S5_BLOCK, S5_BLOCK * SSM_WIDTH), lambda t: (0, t, 0)),
            pl.BlockSpec((b, TILE, FNET_WIDTH), lambda t: (0, t, 0)),
        ],
        out_shape=[
            jax.ShapeDtypeStruct((b, r, QK_WIDTH), BF16),
            jax.ShapeDtypeStruct((b, N_HEADS, 2, LANES, r), BF16),
            jax.ShapeDtypeStruct((b, N_HEADS, r, 2 * V_DIM), BF16),
            jax.ShapeDtypeStruct((b, r, SSM_WIDTH), F32),
            jax.ShapeDtypeStruct((b, r // S5_BLOCK, S5_BLOCK * SSM_WIDTH), F32),
            jax.ShapeDtypeStruct((b, r, FNET_WIDTH), F32),
        ],
        scratch_shapes=[pltpu.VMEM((SSM_WIDTH // LANES, TILE, LANES), F32)],
        compiler_params=_params(("parallel",)),
        name="in_projection",
    )(xs, mods, g_norm1.reshape(depth, 1, d), w_in, gq, gk, e_mat, cos_t, sin_t)


def _attn_kernel(q_ref, kt_ref, ve_ref, lam_ref, gs_ref, o_ref, *, nt, ctx_len):
    t = pl.program_id(2)
    q = q_ref[0]
    r = ve_ref.shape[2]

    def attend(k0):
        outs = []
        ve = ve_ref[0, 0, k0:, :]
        scores = [jnp.dot(q, kt_ref[0, 0, mp, :, k0:], preferred_element_type=F32)
                  for mp in range(2)]
        for s in scores:
            p = jnp.exp2(s - jnp.max(s, axis=1, keepdims=True))
            acc = jnp.dot(p.astype(BF16), ve, preferred_element_type=F32)
            outs.append(acc[:, :V_DIM] / acc[:, V_DIM:])
        lp = lam_ref[0]
        s1 = jnp.sum(lp[0:1] * lp[1:2], axis=-1, keepdims=True)
        s2 = jnp.sum(lp[2:3] * lp[3:4], axis=-1, keepdims=True)
        lam = jnp.exp(s1) - jnp.exp(s2) + lp[4:5, 0:1]
        a = outs[0] - lam * outs[1]
        a = a * lax.rsqrt(jnp.mean(a * a, axis=-1, keepdims=True) + EPS)
        o_ref[0] = (a * gs_ref[0] * lp[4:5, 1:2]).astype(BF16)

    @pl.when(t < nt - 1)
    def _latent_queries():
        attend(0)

    @pl.when(t == nt - 1)
    def _context_queries():
        attend(r - ctx_len)


def _attention(layer, q, kt, ve, lam_p, g_subln):
    b, r, _ = q.shape
    nt = r // TILE
    return pl.pallas_call(
        functools.partial(_attn_kernel, nt=nt, ctx_len=TILE),
        grid=(b, N_HEADS, nt),
        in_specs=[
            pl.BlockSpec((1, TILE, LANES), lambda i, h, t: (i, t, h)),
            pl.BlockSpec((1, 1, 2, LANES, r), lambda i, h, t: (i, h, 0, 0, 0),
                         pipeline_mode=pl.Buffered(1)),
            pl.BlockSpec((1, 1, r, 2 * V_DIM), lambda i, h, t: (i, h, 0, 0),
                         pipeline_mode=pl.Buffered(1)),
            _const_spec((SUBLANES, LANES), layer),
            _const_spec((1, V_DIM), layer),
        ],
        out_specs=pl.BlockSpec((1, TILE, V_DIM), lambda i, h, t: (i, t, h)),
        out_shape=jax.ShapeDtypeStruct((b, r, ATTN_WIDTH), BF16),
        compiler_params=_params(("parallel", "parallel", "arbitrary")),
        name="diff_attention",
    )(q, kt, ve, lam_p, g_subln)


def _zoh(are_ref, aim_ref, ldt_ref):
    a_re = are_ref[0, 0]
    a_im = aim_ref[0, 0]
    dt = jnp.exp(ldt_ref[0, 0])
    mag = jnp.exp(dt * a_re)
    ang = dt * a_im
    ab_re = mag * jnp.cos(ang)
    ab_im = mag * jnp.sin(ang)
    den = a_re * a_re + a_im * a_im
    n_re = ab_re - 1.0
    f_re = (n_re * a_re + ab_im * a_im) / den
    f_im = (ab_im * a_re - n_re * a_im) / den
    return ab_re, ab_im, f_re, f_im


def _complex_powers(base_re, base_im, n):
    pows = [(jnp.ones_like(base_re), jnp.zeros_like(base_im))]
    for _ in range(n):
        pr, pi = pows[-1]
        pows.append((pr * base_re - pi * base_im, pr * base_im + pi * base_re))
    return pows


def _fill_scan_tables(pw_ref, base_re, base_im, reverse):
    shape = (SUBLANES, SSM_STATE)
    pows = _complex_powers(base_re, base_im, SUBLANES)
    row = lax.broadcasted_iota(jnp.int32, shape, 0)
    zero = jnp.zeros(shape, F32)
    for idx, k in enumerate((1, 2, 4)):
        mask = (row + k <= SUBLANES - 1) if reverse else (row >= k)
        pw_ref[2 * idx] = jnp.where(mask, jnp.broadcast_to(pows[k][0], shape), zero)
        pw_ref[2 * idx + 1] = jnp.where(mask, jnp.broadcast_to(pows[k][1], shape), zero)
    pcr = zero
    pci = zero
    for tt in range(SUBLANES):
        e = (SUBLANES - tt) if reverse else tt + 1
        pcr = jnp.where(row == tt, jnp.broadcast_to(pows[e][0], shape), pcr)
        pci = jnp.where(row == tt, jnp.broadcast_to(pows[e][1], shape), pci)
    pw_ref[6] = pcr
    pw_ref[7] = pci


def _scan_rows(s_ref, pw_ref, carry, nrows, reverse, exclusive, groups=None):
    ns = SSM_STATE
    shape = (SUBLANES, ns)
    ngroups = nrows // SUBLANES
    edge = (SUBLANES - 1) if reverse else 0

    def group(g, carry):
        cr, ci = carry
        gi = (ngroups - 1 - g) if reverse else g
        r0 = pl.multiple_of(gi * SUBLANES, SUBLANES)
        xr = s_ref[pl.ds(r0, SUBLANES), :ns]
        xi = s_ref[pl.ds(r0, SUBLANES), ns:]
        for idx, k in enumerate((1, 2, 4)):
            sh = (SUBLANES - k) if reverse else k
            sr = pltpu.roll(xr, sh, 0)
            si = pltpu.roll(xi, sh, 0)
            pr = pw_ref[2 * idx]
            pi = pw_ref[2 * idx + 1]
            xr, xi = xr + pr * sr - pi * si, xi + pr * si + pi * sr
        pr = pw_ref[6]
        pi = pw_ref[7]
        xr, xi = xr + pr * cr - pi * ci, xi + pr * ci + pi * cr
        if exclusive:
            row = lax.broadcasted_iota(jnp.int32, shape, 0)
            sh = (SUBLANES - 1) if reverse else 1
            er = jnp.where(row == edge, cr, pltpu.roll(xr, sh, 0))
            ei = jnp.where(row == edge, ci, pltpu.roll(xi, sh, 0))
        else:
            er, ei = xr, xi
        s_ref[pl.ds(r0, SUBLANES), :ns] = er
        s_ref[pl.ds(r0, SUBLANES), ns:] = ei
        last = 0 if reverse else SUBLANES - 1
        return (jnp.broadcast_to(xr[last:last + 1], shape),
                jnp.broadcast_to(xi[last:last + 1], shape))

    if groups is None:
        return lax.fori_loop(0, ngroups, group, carry)
    for g in groups:
        carry = group(g, carry)
    return carry


def _s5_ctx_kernel(u_ref, are_ref, aim_ref, ldt_ref, bre_ref, bim_ref, c_ref,
                   o_ref, st_ref, bbar_ref, pw_ref, s_ref, y_ref, *, reverse):
    ns = SSM_STATE
    ab_re, ab_im, f_re, f_im = _zoh(are_ref, aim_ref, ldt_ref)
    bre = bre_ref[0, 0]
    bim = bim_ref[0, 0]
    bbar_ref[:, :ns] = (f_re * bre - f_im * bim).astype(BF16)
    bbar_ref[:, ns:] = (f_re * bim + f_im * bre).astype(BF16)
    _fill_scan_tables(pw_ref, ab_re, ab_im, reverse)
    s_ref[...] = jnp.dot(u_ref[0].astype(BF16), bbar_ref[...], preferred_element_type=F32)
    zero = jnp.zeros((SUBLANES, ns), F32)
    cr, ci = _scan_rows(s_ref, pw_ref, (zero, zero), TILE, reverse, exclusive=False)
    st_ref[0, 0] = cr
    st_ref[0, 1] = ci
    y = jnp.dot(s_ref[...].astype(BF16), c_ref[0, 0], preferred_element_type=F32)
    _rows_to_blocked(y, y_ref, o_ref.at[0], S5_BLOCK, tile_major=True)


def _s5_block_kernel(u8_ref, s0_ref, are_ref, aim_ref, ldt_ref, bre_ref, bim_ref, ctre_ref,
                     ctim_ref, o_ref, wx_ref, wc_ref, wt_ref, pw_ref, carry_ref, s_ref,
                     *, reverse):
    ns = SSM_STATE
    w = SSM_WIDTH
    nb = S5_BLOCK
    nh = w // LANES
    hs = ns // nh
    hw = nb * LANES
    c = pl.program_id(0)
    b = pl.program_id(1)

    @pl.when((c == 0) & (b == 0))
    def _build_maps():
        ab_re, ab_im, f_re, f_im = _zoh(are_ref, aim_ref, ldt_ref)
        bre = bre_ref[0, 0]
        bim = bim_ref[0, 0]
        bb_re = f_re * bre - f_im * bim
        bb_im = f_re * bim + f_im * bre
        pows = _complex_powers(ab_re, ab_im, nb)
        ctre = ctre_ref[0, 0]
        ctim = ctim_ref[0, 0]
        ct_stack = jnp.concatenate([ctre, -ctim], axis=1).astype(BF16)
        chan = [slice(h * LANES, (h + 1) * LANES) for h in range(nh)]
        stat = [slice(h * hs, (h + 1) * hs) for h in range(nh)]
        tok = [slice(t * LANES, (t + 1) * LANES) for t in range(nb)]
        taps = []
        for e in range(nb):
            pr, pi = pows[e]
            xr = pr * bb_re - pi * bb_im
            xi = pr * bb_im + pi * bb_re
            i = e if reverse else nb - 1 - e
            for h in range(nh):
                wx_ref[h, tok[i], :hs] = xr[chan[h], stat[h]].astype(BF16)
                wx_ref[h, tok[i], hs:] = xi[chan[h], stat[h]].astype(BF16)
            xk = jnp.concatenate([xr, xi], axis=1).astype(BF16)
            taps.append(lax.dot_general(xk, ct_stack, (((1,), (1,)), ((), ())),
                                        preferred_element_type=F32).astype(BF16))
        zero_blk = jnp.zeros((LANES, LANES), BF16)
        for i in range(nb):
            for j in range(nb):
                lag = (i - j) if reverse else (j - i)
                for h in range(nh):
                    wt_ref[h, tok[i], tok[j]] = taps[lag][chan[h], chan[h]] if lag >= 0 else zero_blk
        for j in range(nb):
            pr, pi = pows[nb - j] if reverse else pows[j + 1]
            c_re = ctre * pr - ctim * pi
            c_im = -(ctre * pi + ctim * pr)
            for h in range(nh):
                wc_ref[h, :hs, tok[j]] = c_re[chan[h], stat[h]].T.astype(BF16)
                wc_ref[h, hs:, tok[j]] = c_im[chan[h], stat[h]].T.astype(BF16)
        _fill_scan_tables(pw_ref, pows[nb][0], pows[nb][1], reverse)

    @pl.when(c == 0)
    def _load_state():
        carry_ref[b] = s0_ref[0]

    u8 = u8_ref[0].astype(BF16)
    for h in range(nh):
        x = jnp.dot(u8[:, h * hw:(h + 1) * hw], wx_ref[h], preferred_element_type=F32)
        s_ref[:, h * hs:(h + 1) * hs] = x[:, :hs]
        s_ref[:, ns + h * hs:ns + (h + 1) * hs] = x[:, hs:]
    carry = (carry_ref[b, 0], carry_ref[b, 1])
    pairs = nb // 2
    per = (S5_CHUNK_ROWS // SUBLANES) // (nh * pairs)
    for h in range(nh):
        for jp in range(pairs):
            j0 = 2 * jp
            rows = slice(j0 * LANES, hw) if reverse else slice(0, (j0 + 2) * LANES)
            cols = slice(j0 * LANES, (j0 + 2) * LANES)
            o_ref[0, :, h * hw + cols.start:h * hw + cols.stop] = jnp.dot(
                u8[:, h * hw + rows.start:h * hw + rows.stop], wt_ref[h, rows, cols],
                preferred_element_type=F32)
            k = h * pairs + jp
            carry = _scan_rows(s_ref, pw_ref, carry, S5_CHUNK_ROWS, reverse, exclusive=True,
                               groups=range(k * per, (k + 1) * per))
    carry_ref[b, 0] = carry[0]
    carry_ref[b, 1] = carry[1]
    for h in range(nh):
        y = (jnp.dot(s_ref[:, h * hs:(h + 1) * hs].astype(BF16), wc_ref[h, :hs],
                     preferred_element_type=F32)
             + jnp.dot(s_ref[:, ns + h * hs:ns + (h + 1) * hs].astype(BF16), wc_ref[h, hs:],
                       preferred_element_type=F32))
        o_ref[0, :, h * hw:(h + 1) * hw] = o_ref[0, :, h * hw:(h + 1) * hw] + y


def _s5_direction(layer, direction, u, u8, a_re, a_im, ldt, b_re, b_im, ct_re, ct_im, c_stack):
    b, r, w = u.shape
    nt = r // TILE
    rows = r // S5_BLOCK
    wide = S5_BLOCK * w
    tile_rows = TILE // S5_BLOCK
    ns = SSM_STATE
    reverse = direction == 1

    def dir_spec(shape):
        nd = len(shape)
        return pl.BlockSpec((1, 1) + tuple(shape), lambda *_: (layer, direction) + (0,) * nd,
                            pipeline_mode=pl.Buffered(1))

    y_ctx, state = pl.pallas_call(
        functools.partial(_s5_ctx_kernel, reverse=reverse),
        grid=(b,),
        in_specs=[pl.BlockSpec((1, TILE, w), lambda i: (i, nt - 1, 0)),
                  dir_spec((1, ns)), dir_spec((1, ns)), dir_spec((1, ns)),
                  dir_spec((w, ns)), dir_spec((w, ns)), dir_spec((2 * ns, w))],
        out_specs=[pl.BlockSpec((1, tile_rows, wide), lambda i: (i, 0, 0)),
                   pl.BlockSpec((1, 2, SUBLANES, ns), lambda i: (i, 0, 0, 0))],
        out_shape=[jax.ShapeDtypeStruct((b, tile_rows, wide), F32),
                   jax.ShapeDtypeStruct((b, 2, SUBLANES, ns), F32)],
        scratch_shapes=[pltpu.VMEM((w, 2 * ns), BF16),
                        pltpu.VMEM((8, SUBLANES, ns), F32),
                        pltpu.VMEM((TILE, 2 * ns), F32),
                        pltpu.VMEM((w // LANES, TILE, LANES), F32)],
        compiler_params=_params(("parallel",)),
        name="s5_context",
    )(u, a_re, a_im, ldt, b_re, b_im, c_stack)

    nchunks = (r - TILE) // (S5_BLOCK * S5_CHUNK_ROWS)

    def chunk(c, i):
        return (i, nchunks - 1 - c if reverse else c, 0)

    y_lat = pl.pallas_call(
        functools.partial(_s5_block_kernel, reverse=reverse),
        grid=(nchunks, b),
        in_specs=[pl.BlockSpec((1, S5_CHUNK_ROWS, wide), chunk),
                  pl.BlockSpec((1, 2, SUBLANES, ns), lambda c, i: (i, 0, 0, 0)),
                  dir_spec((1, ns)), dir_spec((1, ns)), dir_spec((1, ns)),
                  dir_spec((w, ns)), dir_spec((w, ns)), dir_spec((w, ns)), dir_spec((w, ns))],
        out_specs=pl.BlockSpec((1, S5_CHUNK_ROWS, wide), chunk),
        out_shape=jax.ShapeDtypeStruct((b, rows - tile_rows, wide), F32),
        scratch_shapes=[pltpu.VMEM((w // LANES, wide // (w // LANES), 2 * ns // (w // LANES)), BF16),
                        pltpu.VMEM((w // LANES, 2 * ns // (w // LANES), wide // (w // LANES)), BF16),
                        pltpu.VMEM((w // LANES, wide // (w // LANES), wide // (w // LANES)), BF16),
                        pltpu.VMEM((8, SUBLANES, ns), F32),
                        pltpu.VMEM((b, 2, SUBLANES, ns), F32),
                        pltpu.VMEM((S5_CHUNK_ROWS, 2 * ns), F32)],
        compiler_params=_params(("arbitrary", "arbitrary"), vmem=S5_VMEM_LIMIT),
        name="s5_blocks_reverse" if reverse else "s5_blocks_forward",
    )(u8, state, a_re, a_im, ldt, b_re, b_im, ct_re, ct_im)
    return y_lat, y_ctx


def _fourier_outer_kernel(w_ref, x_ref, g_ref):
    g_ref[0] = jnp.dot(w_ref[...], x_ref[0].astype(BF16),
                       preferred_element_type=F32).astype(BF16)


def _fourier_outer(f, w_outer):
    b, r, w = f.shape
    nt = r // TILE
    l2n = nt - 1
    flat = TILE * w
    return pl.pallas_call(
        _fourier_outer_kernel,
        grid=(b, flat // FOURIER_CHUNK),
        in_specs=[_const_spec((2 * l2n, l2n)),
                  pl.BlockSpec((1, l2n, FOURIER_CHUNK), lambda i, j: (i, 0, j))],
        out_specs=pl.BlockSpec((1, 2 * l2n, FOURIER_CHUNK), lambda i, j: (i, 0, j)),
        out_shape=jax.ShapeDtypeStruct((b, 2 * l2n, flat), BF16),
        compiler_params=_params(("parallel", "parallel")),
        name="fourier_outer",
    )(w_outer, f.reshape(b, nt, flat))


def _fourier_tail(y_re, y_im, c64_ref, wf_ref):
    w = FNET_WIDTH
    z = (jnp.dot(y_re.astype(BF16), c64_ref[:w], preferred_element_type=F32)
         + jnp.dot(y_im.astype(BF16), c64_ref[w:], preferred_element_type=F32))
    return jnp.dot(z.astype(BF16), wf_ref[0], preferred_element_type=F32)


def _fourier_inner_kernel(gr_ref, gi_ref, m_ref, c64_ref, wf_ref, o_ref):
    w = FNET_WIDTH
    ys = [jnp.dot(m_ref[i, :, :TILE], gr_ref[0, i], preferred_element_type=F32)
          + jnp.dot(m_ref[i, :, TILE:], gi_ref[0, i], preferred_element_type=F32)
          for i in range(FOURIER_GROUP)]
    for i, y in enumerate(ys):
        o_ref[0, :, i * w:(i + 1) * w] = _fourier_tail(y[:TILE], y[TILE:], c64_ref, wf_ref)


def _fourier_inner(layer, g, m_tab, c64_tab, wf_blk):
    b, two_l2n, flat = g.shape
    l2n = two_l2n // 2
    w = flat // TILE
    steps = l2n // FOURIER_GROUP
    g4 = g.reshape(b, two_l2n, TILE, w)
    return pl.pallas_call(
        _fourier_inner_kernel,
        grid=(b, steps),
        in_specs=[
            pl.BlockSpec((1, FOURIER_GROUP, TILE, w), lambda i, j: (i, j, 0, 0)),
            pl.BlockSpec((1, FOURIER_GROUP, TILE, w), lambda i, j: (i, steps + j, 0, 0)),
            pl.BlockSpec((FOURIER_GROUP, 2 * TILE, 2 * TILE), lambda i, j: (j, 0, 0)),
            _const_spec((2 * w, w)),
            _const_spec((w, w), layer),
        ],
        out_specs=pl.BlockSpec((1, TILE, FOURIER_GROUP * w), lambda i, j: (i, 0, j)),
        out_shape=jax.ShapeDtypeStruct((b, TILE, l2n * w), F32),
        compiler_params=_params(("parallel", "parallel")),
        name="fourier_inner",
    )(g4, g4, m_tab, c64_tab, wf_blk)


def _fourier_ctx_kernel(x_ref, m_ref, c64_ref, wf_ref, o_ref, z_ref, *, l2n):
    y = jnp.dot(m_ref[...], x_ref[0].astype(BF16), preferred_element_type=F32)
    z = _fourier_tail(y[:TILE], y[TILE:], c64_ref, wf_ref)
    _rows_to_blocked(z, z_ref, o_ref.at[0], l2n)


def _fourier_context(layer, f, m_ctx, c64_tab, wf_blk):
    b, r, w = f.shape
    nt = r // TILE
    l2n = nt - 1
    return pl.pallas_call(
        functools.partial(_fourier_ctx_kernel, l2n=l2n),
        grid=(b,),
        in_specs=[
            pl.BlockSpec((1, TILE, w), lambda i: (i, nt - 1, 0)),
            _const_spec((2 * TILE, TILE)),
            _const_spec((2 * w, w)),
            _const_spec((w, w), layer),
        ],
        out_specs=pl.BlockSpec((1, TILE // l2n, l2n * w), lambda i: (i, 0, 0)),
        out_shape=jax.ShapeDtypeStruct((b, TILE // l2n, l2n * w), F32),
        scratch_shapes=[pltpu.VMEM((w // LANES, TILE, LANES), F32)],
        compiler_params=_params(("parallel",)),
        name="fourier_context",
    )(f, m_ctx, c64_tab, wf_blk)


def _outffn_kernel(x_ref, a_ref, yf_ref, yb_ref, yfc_ref, ybc_ref, u_ref, fn_ref, fnc_ref, mod_ref,
                   dsk_ref, wglu_ref, bglu_ref, wo_ref, g2_ref, w1_ref, w2_ref, o_ref, ytok_ref,
                   ftok_ref, *, l2n, nt):
    d = D_MODEL
    x = x_ref[0]
    mod = mod_ref[0, 0]
    o1 = ATTN_WIDTH
    o2 = ATTN_WIDTH + SSM_WIDTH
    is_ctx = pl.program_id(1) == nt - 1
    y_blk = jnp.where(is_ctx, yfc_ref[0] + ybc_ref[0], yf_ref[0] + yb_ref[0])
    y_scan = _blocked_to_rows(y_blk, ytok_ref, S5_BLOCK, tile_major=True)
    fnet = _blocked_to_rows(jnp.where(is_ctx, fnc_ref[0], fn_ref[0]), ftok_ref, l2n)
    hg = jax.nn.gelu(y_scan + dsk_ref[0] * u_ref[0])
    zg = jnp.dot(hg.astype(BF16), wglu_ref[0], preferred_element_type=F32) + bglu_ref[0]
    s5 = (hg * jax.nn.sigmoid(zg)).astype(BF16)
    mix = (jnp.dot(a_ref[0], wo_ref[0, :o1], preferred_element_type=F32)
           + jnp.dot(s5, wo_ref[0, o1:o2], preferred_element_type=F32)
           + jnp.dot(fnet.astype(BF16), wo_ref[0, o2:], preferred_element_type=F32))
    x1 = x + mod[:, 2 * d:3 * d] * mix
    ms = jnp.mean(x1 * x1, axis=-1, keepdims=True)
    hn = x1 * lax.rsqrt(ms + EPS) * g2_ref[0]
    h = (hn * (1.0 + mod[:, 4 * d:5 * d]) + mod[:, 3 * d:4 * d]).astype(BF16)
    acc = jnp.zeros((TILE, d), F32)
    for cc in range(D_FF // FF_CHUNK):
        sl = slice(cc * FF_CHUNK, (cc + 1) * FF_CHUNK)
        t = jnp.dot(h, w1_ref[0, :, sl], preferred_element_type=F32)
        t = jnp.square(jnp.maximum(t, 0.0))
        acc = acc + jnp.dot(t.astype(BF16), w2_ref[0, sl, :], preferred_element_type=F32)
    o_ref[0] = x1 + mod[:, 5 * d:] * acc


def _out_ffn(layer, xs, attn, y_fwd, y_bwd, u, fn, fn_ctx, mods, d_skip, w_glu, b_glu, w_out,
             g_norm2, w_ff1, w_ff2, latent_only):
    b, r, d = xs.shape
    nt = r // TILE
    l2n = nt - 1
    depth = w_out.shape[0]
    s5_shape = (1, TILE // S5_BLOCK, S5_BLOCK * SSM_WIDTH)
    fn_shape = (1, TILE // l2n, l2n * FNET_WIDTH)

    def latent(shape):
        return pl.BlockSpec(shape, lambda i, t: (i, jnp.minimum(t, nt - 2), 0))

    def context(shape):
        return pl.BlockSpec(shape, lambda i, t: (i, 0, 0))

    def tile(width):
        return pl.BlockSpec((1, TILE, width), lambda i, t: (i, t, 0))

    return pl.pallas_call(
        functools.partial(_outffn_kernel, l2n=l2n, nt=nt),
        grid=(b, nt - 1 if latent_only else nt),
        in_specs=[
            tile(d), tile(ATTN_WIDTH), latent(s5_shape), latent(s5_shape), context(s5_shape),
            context(s5_shape), tile(SSM_WIDTH), latent(fn_shape), context(fn_shape),
            pl.BlockSpec((1, 1, 1, 6 * d),
                         lambda i, t: (layer, jnp.where(t == nt - 1, b, i), 0, 0)),
            _const_spec((1, SSM_WIDTH), layer),
            _const_spec((SSM_WIDTH, SSM_WIDTH), layer),
            _const_spec((1, SSM_WIDTH), layer),
            _const_spec((d, d), layer),
            _const_spec((1, d), layer),
            _const_spec((d, D_FF), layer),
            _const_spec((D_FF, d), layer),
        ],
        out_specs=tile(d),
        out_shape=jax.ShapeDtypeStruct((b, r - TILE if latent_only else r, d), F32),
        scratch_shapes=[pltpu.VMEM((SSM_WIDTH // LANES, TILE, LANES), F32),
                        pltpu.VMEM((FNET_WIDTH // LANES, TILE, LANES), F32)],
        input_output_aliases={} if latent_only else {0: 0},
        compiler_params=_params(("parallel", "parallel")),
        name="out_ffn",
    )(xs, attn, y_fwd[0], y_bwd[0], y_fwd[1], y_bwd[1], u, fn, fn_ctx,
      mods.reshape(depth, SUBLANES, 1, 6 * d), d_skip, w_glu, b_glu,
      w_out, g_norm2.reshape(depth, 1, d), w_ff1, w_ff2)


def _head_lane_tables():
    j = np.arange(LANES)
    half = j // 64
    mp = (j // 32) % 2
    idx = j % 32
    src_in_head = mp * HEAD_DIM + half * 32 + idx
    gain_idx = half * 32 + idx
    return src_in_head, gain_idx, mp


def _in_proj_column_order():
    src_in_head, _, _ = _head_lane_tables()
    qk = np.concatenate([h * LANES + src_in_head for h in range(N_HEADS)])
    return np.concatenate([qk, QK_WIDTH + qk, np.arange(2 * QK_WIDTH, IN_WIDTH)])


def _same_map_matrix():
    i = np.arange(256)
    head = i // LANES
    mp = ((i % LANES) // 32) % 2
    same = (head[:, None] == head[None, :]) & (mp[:, None] == mp[None, :])
    return same.astype(np.float32)


def _rope_tables(seq, ctx_len):
    t = jnp.arange(seq)
    row = (t // GRID_W).astype(F32)
    col = (t % GRID_W).astype(F32)
    n_freq = HEAD_DIM // 4
    inv = jnp.power(ROPE_BASE, -jnp.arange(n_freq, dtype=F32) / n_freq)
    ang = jnp.concatenate([row[:, None] * inv, col[:, None] * inv], axis=-1)
    cos = jnp.tile(jnp.cos(ang), (1, 4))
    sign = np.where(np.arange(LANES) < LANES // 2, -1.0, 1.0).astype(np.float32)
    sin = jnp.tile(jnp.sin(ang), (1, 4)) * sign
    cos = jnp.concatenate([cos, jnp.ones((ctx_len, LANES), F32)], axis=0)
    sin = jnp.concatenate([sin, jnp.zeros((ctx_len, LANES), F32)], axis=0)
    return cos, sin


def _dft_tables(length):
    l2n = length // TILE
    k2 = np.arange(l2n)
    phi = 2.0 * np.pi * ((k2[:, None] * k2[None, :]) % l2n) / l2n
    w_outer = np.concatenate([np.cos(phi), -np.sin(phi)], axis=0).astype(np.float32)
    l1 = np.arange(TILE)
    alpha = 2.0 * np.pi * ((k2[:, None] * l1[None, :]) % length) / length
    beta = 2.0 * np.pi * ((l1[:, None] * l1[None, :]) % TILE) / TILE
    ca = jnp.asarray(np.cos(alpha).astype(np.float32))[:, None, :]
    sa = jnp.asarray(np.sin(alpha).astype(np.float32))[:, None, :]
    cb = jnp.asarray(np.cos(beta).astype(np.float32))[None]
    sb = jnp.asarray(np.sin(beta).astype(np.float32))[None]
    norm = 1.0 / math.sqrt(length)
    ct = (ca * cb - sa * sb) * norm
    st = (sa * cb + ca * sb) * norm
    m = jnp.concatenate([jnp.concatenate([ct, st], axis=2),
                         jnp.concatenate([-st, ct], axis=2)], axis=1).astype(BF16)
    return jnp.asarray(w_outer).astype(BF16), m


def _tile_dft_table():
    l1 = np.arange(TILE)
    beta = 2.0 * np.pi * ((l1[:, None] * l1[None, :]) % TILE) / TILE
    tab = np.concatenate([np.cos(beta), -np.sin(beta)], axis=0) / math.sqrt(TILE)
    return jnp.asarray(tab.astype(np.float32)).astype(BF16)


def _channel_dft_table():
    c = np.arange(FNET_C)
    th = 2.0 * np.pi * ((c[:, None] * c[None, :]) % FNET_C) / FNET_C
    eye = np.eye(FNET_G)
    norm = 1.0 / math.sqrt(FNET_C)
    cblk = np.kron(eye, np.cos(th)) * norm
    sblk = np.kron(eye, np.sin(th)) * norm
    return jnp.asarray(np.concatenate([cblk, sblk], axis=0).astype(np.float32)).astype(BF16)


def _block_diag(blocks):
    g = blocks.shape[-3]
    eye = jnp.eye(g, dtype=blocks.dtype)
    out = jnp.einsum('...gab,gh->...gahb', blocks, eye)
    return out.reshape(blocks.shape[:-3] + (g * blocks.shape[-2], g * blocks.shape[-1]))


def kernel(x, c, ctx, c_ctx, w_mod, b_mod, g_norm1, w_in, g_qnorm, g_knorm, lam_q1, lam_k1, lam_q2, lam_k2, g_subln, ssm_a_re, ssm_a_im, ssm_log_dt, ssm_b_re, ssm_b_im, ssm_c_re, ssm_c_im, ssm_d, w_glu, b_glu, w_fnet, w_out, g_norm2, w_ff1, w_ff2):
    bsz, seq, d = x.shape
    ctx_len = ctx.shape[1]
    depth = w_mod.shape[0]
    assert d == D_MODEL and ctx_len == TILE and seq % TILE == 0 and seq % GRID_W == 0
    assert bsz + 1 <= SUBLANES

    assert (seq // TILE) % FOURIER_GROUP == 0 and seq % (S5_BLOCK * S5_CHUNK_ROWS) == 0
    xs = jnp.concatenate([x, ctx], axis=1)
    act = jnp.concatenate([c, c_ctx[None], jnp.zeros((SUBLANES - bsz - 1, d), F32)], axis=0)
    mods = _modulation(act, w_mod, b_mod)

    w_in_p = jnp.take(w_in, jnp.asarray(_in_proj_column_order()), axis=2).astype(BF16)
    _, gain_idx, _ = _head_lane_tables()
    gq = jnp.tile(g_qnorm[:, gain_idx], (1, N_HEADS)).reshape(depth, 1, QK_WIDTH)
    gk = jnp.tile(g_knorm[:, gain_idx], (1, N_HEADS)).reshape(depth, 1, QK_WIDTH)
    e_mat = jnp.asarray(_same_map_matrix()).astype(BF16)
    cos_t, sin_t = _rope_tables(seq, ctx_len)
    lam_init = np.array([0.8 - 0.6 * math.exp(-0.3 * i) for i in range(depth)], np.float32)
    pad = jnp.zeros((depth, LANES - HEAD_DIM), F32)
    lam_rows = [jnp.concatenate([v, pad], axis=1) for v in (lam_q1, lam_k1, lam_q2, lam_k2)]
    const_row = np.zeros((depth, LANES), np.float32)
    const_row[:, 0] = lam_init
    const_row[:, 1] = 1.0 - lam_init
    lam_p = jnp.stack(lam_rows + [jnp.asarray(const_row)]
                      + [jnp.zeros((depth, LANES), F32)] * (SUBLANES - 5), axis=1)
    g_sub = g_subln.reshape(depth, 1, V_DIM)

    a_re = ssm_a_re.reshape(depth, 2, 1, SSM_STATE)
    a_im = ssm_a_im.reshape(depth, 2, 1, SSM_STATE)
    ldt = jnp.repeat(ssm_log_dt, SSM_N, axis=-1).reshape(depth, 2, 1, SSM_STATE)
    b_re = _block_diag(jnp.swapaxes(ssm_b_re, -1, -2))
    b_im = _block_diag(jnp.swapaxes(ssm_b_im, -1, -2))
    c_stack = jnp.concatenate([_block_diag(jnp.swapaxes(ssm_c_re, -1, -2)),
                               -_block_diag(jnp.swapaxes(ssm_c_im, -1, -2))],
                              axis=2).astype(BF16)
    ct_re = _block_diag(ssm_c_re)
    ct_im = _block_diag(ssm_c_im)
    d_skip = ssm_d.reshape(depth, 1, SSM_WIDTH)
    w_glu_b = w_glu.astype(BF16)
    b_glu_r = b_glu.reshape(depth, 1, SSM_WIDTH)

    w_outer, m_lat = _dft_tables(seq)
    m_ctx = _tile_dft_table()
    c64_tab = _channel_dft_table()
    wf_blk = _block_diag(w_fnet).astype(BF16)

    w_out_b = w_out.astype(BF16)
    w_ff1_b = w_ff1.astype(BF16)
    w_ff2_b = w_ff2.astype(BF16)

    for layer in range(depth):
        q, kt, ve, u, u8, f = _in_projection(layer, xs, mods, g_norm1, w_in_p, gq, gk, e_mat,
                                         cos_t, sin_t)
        attn = _attention(layer, q, kt, ve, lam_p, g_sub)
        y_fwd, y_bwd = [_s5_direction(layer, dr, u, u8, a_re, a_im, ldt, b_re, b_im, ct_re,
                                      ct_im, c_stack) for dr in range(2)]
        fn = _fourier_inner(layer, _fourier_outer(f, w_outer), m_lat, c64_tab, wf_blk)
        fn_ctx = _fourier_context(layer, f, m_ctx, c64_tab, wf_blk)
        xs = _out_ffn(layer, xs, attn, y_fwd, y_bwd, u, fn, fn_ctx, mods, d_skip, w_glu_b, b_glu_r,
                      w_out_b, g_norm2, w_ff1_b, w_ff2_b, latent_only=layer == depth - 1)
    return xs
```

```python
import functools
import math

import numpy as np
import jax
import jax.numpy as jnp
from jax import lax
from jax.experimental import pallas as pl
from jax.experimental.pallas import tpu as pltpu

F32 = jnp.float32
BF16 = jnp.bfloat16

D_MODEL = 1024
GRID_W = 64
N_HEADS = 4
HEAD_DIM = 64
V_DIM = 2 * HEAD_DIM
QK_WIDTH = N_HEADS * 2 * HEAD_DIM
ATTN_WIDTH = N_HEADS * V_DIM
SSM_WIDTH = D_MODEL // 4
SSM_P = 16
SSM_G = SSM_WIDTH // SSM_P
SSM_N = 64
SSM_STATE = SSM_G * SSM_N
FNET_WIDTH = D_MODEL // 4
FNET_G = 4
FNET_C = FNET_WIDTH // FNET_G
IN_WIDTH = 2 * QK_WIDTH + ATTN_WIDTH + SSM_WIDTH + FNET_WIDTH
D_FF = 4 * D_MODEL
ROPE_BASE = 10000.0
EPS = 1e-6
SCALE = HEAD_DIM ** -0.5
LOG2E = math.log2(math.e)

TILE = 256
LANES = 128
SUBLANES = 8
FF_CHUNK = 1024
MOD_BLOCK = 1536
FOURIER_CHUNK = 8192
FOURIER_GROUP = 4
S5_BLOCK = 8
S5_CHUNK_ROWS = 256
VMEM_LIMIT = 48 * 1024 * 1024
S5_VMEM_LIMIT = 56 * 1024 * 1024

def _const_spec(shape, layer=None):
    nd = len(shape)
    if layer is None:
        return pl.BlockSpec(shape, lambda *_: (0,) * nd, pipeline_mode=pl.Buffered(1))
    return pl.BlockSpec((1,) + tuple(shape), lambda *_: (layer,) + (0,) * nd,
                        pipeline_mode=pl.Buffered(1))


def _params(sem, vmem=None):
    return pltpu.CompilerParams(dimension_semantics=sem, vmem_limit_bytes=vmem or VMEM_LIMIT)


def _blocked_lane(j, h, period, nh, tile_major):
    return ((h * period + j) if tile_major else (j * nh + h)) * LANES


def _rows_to_blocked(tok, scr_ref, blk_ref, period, tile_major=False):
    n, w = tok.shape
    nh = w // LANES
    for h in range(nh):
        scr_ref[h] = tok[:, h * LANES:(h + 1) * LANES]
    for j in range(period):
        for h in range(nh):
            lo = _blocked_lane(j, h, period, nh, tile_major)
            blk_ref[:, lo:lo + LANES] = scr_ref[h, pl.ds(j, n // period, stride=period), :]


def _blocked_to_rows(blk, scr_ref, period, tile_major=False):
    nh, n, _ = scr_ref.shape
    for j in range(period):
        for h in range(nh):
            lo = _blocked_lane(j, h, period, nh, tile_major)
            scr_ref[h, pl.ds(j, n // period, stride=period), :] = blk[:, lo:lo + LANES]
    return jnp.concatenate([scr_ref[h] for h in range(nh)], axis=1)


def _mod_kernel(act_ref, w_ref, b_ref, o_ref):
    a = act_ref[...]
    a = a * jax.nn.sigmoid(a)
    o_ref[0] = jnp.dot(a.astype(BF16), w_ref[0].astype(BF16),
                       preferred_element_type=F32) + b_ref[0]


def _modulation(act, w_mod, b_mod):
    depth, d, n = w_mod.shape
    bn = MOD_BLOCK
    return pl.pallas_call(
        _mod_kernel,
        grid=(depth, n // bn),
        in_specs=[pl.BlockSpec((SUBLANES, d), lambda l, j: (0, 0)),
                  pl.BlockSpec((1, d, bn), lambda l, j: (l, 0, j)),
                  pl.BlockSpec((1, 1, bn), lambda l, j: (l, 0, j))],
        out_specs=pl.BlockSpec((1, SUBLANES, bn), lambda l, j: (l, 0, j)),
        out_shape=jax.ShapeDtypeStruct((depth, SUBLANES, n), F32),
        compiler_params=_params(("parallel", "parallel")),
        name="modulation",
    )(act, w_mod, b_mod.reshape(depth, 1, n))


def _inproj_kernel(x_ref, mod_ref, g1_ref, w_ref, gq_ref, gk_ref, e_ref, cos_ref, sin_ref,
                   q_ref, kt_ref, ve_ref, u_ref, u8_ref, f_ref, stage_ref, *, nt):
    d = D_MODEL
    bsz = x_ref.shape[0]
    is_ctx = pl.program_id(0) == nt - 1
    e = e_ref[...]
    cos = cos_ref[...]
    sin = sin_ref[...]

    def project(i):
        x = x_ref[i]
        ms = jnp.mean(x * x, axis=-1, keepdims=True)
        xn = x * lax.rsqrt(ms + EPS) * g1_ref[0]
        mod = mod_ref[0, pl.ds(jnp.where(is_ctx, bsz, i), 1), :]
        h = xn * (1.0 + mod[:, d:2 * d]) + mod[:, :d]
        return jnp.dot(h.astype(BF16), w_ref[0], preferred_element_type=F32)

    def qk_norm(z, g):
        sq = z * z
        hi = sq.astype(BF16)
        lo = (sq - hi.astype(F32)).astype(BF16)
        parts = []
        for j in range(QK_WIDTH // 256):
            sl = slice(j * 256, (j + 1) * 256)
            parts.append(jnp.dot(hi[:, sl], e, preferred_element_type=F32)
                         + jnp.dot(lo[:, sl], e, preferred_element_type=F32))
        ssum = jnp.concatenate(parts, axis=-1)
        return z * lax.rsqrt(ssum * (1.0 / HEAD_DIM) + EPS) * g

    def rope(zh):
        return zh * cos + pltpu.roll(zh, LANES // 2, 1) * sin

    row_map = (lax.broadcasted_iota(jnp.int32, (LANES, TILE), 0) // 32) % 2
    ones = jnp.ones((TILE, LANES), BF16)

    def finish(i, proj):
        qn = qk_norm(proj[:, :QK_WIDTH], gq_ref[0])
        kn = qk_norm(proj[:, QK_WIDTH:2 * QK_WIDTH], gk_ref[0])
        for hh in range(N_HEADS):
            sl = slice(hh * LANES, (hh + 1) * LANES)
            q_ref[i, :, sl] = (rope(qn[:, sl]) * (SCALE * LOG2E)).astype(BF16)
            kt = rope(kn[:, sl]).T
            kt_ref[i, hh, 0] = jnp.where(row_map == 0, kt, 0.0).astype(BF16)
            kt_ref[i, hh, 1] = jnp.where(row_map == 1, kt, 0.0).astype(BF16)
            vo = 2 * QK_WIDTH + hh * V_DIM
            ve_ref[i, hh, :, :V_DIM] = proj[:, vo:vo + V_DIM].astype(BF16)
            ve_ref[i, hh, :, V_DIM:] = ones
        uo = 2 * QK_WIDTH + ATTN_WIDTH
        u_ref[i] = proj[:, uo:uo + SSM_WIDTH]
        _rows_to_blocked(proj[:, uo:uo + SSM_WIDTH], stage_ref, u8_ref.at[i], S5_BLOCK,
                         tile_major=True)
        f_ref[i] = proj[:, uo + SSM_WIDTH:]

    projs = [project(i) for i in range(bsz)]
    for i, proj in enumerate(projs):
        finish(i, proj)


def _in_projection(layer, xs, mods, g_norm1, w_in, gq, gk, e_mat, cos_t, sin_t):
    b, r, d = xs.shape
    nt = r // TILE
    depth = w_in.shape[0]
    return pl.pallas_call(
        functools.partial(_inproj_kernel, nt=nt),
        grid=(nt,),
        in_specs=[
            pl.BlockSpec((b, TILE, d), lambda t: (0, t, 0)),
            _const_spec((SUBLANES, 6 * d), layer),
            _const_spec((1, d), layer),
            _const_spec((d, IN_WIDTH), layer),
            _const_spec((1, QK_WIDTH), layer),
            _const_spec((1, QK_WIDTH), layer),
            _const_spec((256, 256)),
            pl.BlockSpec((TILE, LANES), lambda t: (t, 0)),
            pl.BlockSpec((TILE, LANES), lambda t: (t, 0)),
        ],
        out_specs=[
            pl.BlockSpec((b, TILE, QK_WIDTH), lambda t: (0, t, 0)),
            pl.BlockSpec((b, N_HEADS, 2, LANES, TILE), lambda t: (0, 0, 0, 0, t)),
            pl.BlockSpec((b, N_HEADS, TILE, 2 * V_DIM), lambda t: (0, 0, t, 0)),
            pl.BlockSpec((b, TILE, SSM_WIDTH), lambda t: (0, t, 0)),
            pl.BlockSpec((b, TILE // S5_BLOCK, S5_BLOCK * SSM_WIDTH), lambda t: (0, t, 0)),
            pl.BlockSpec((b, TILE, FNET_WIDTH), lambda t: (0, t, 0)),
        ],
        out_shape=[
            jax.ShapeDtypeStruct((b, r, QK_WIDTH), BF16),
            jax.ShapeDtypeStruct((b, N_HEADS, 2, LANES, r), BF16),
            jax.ShapeDtypeStruct((b, N_HEADS, r, 2 * V_DIM), BF16),
            jax.ShapeDtypeStruct((b, r, SSM_WIDTH), F32),
            jax.ShapeDtypeStruct((b, r // S5_BLOCK, S5_BLOCK * SSM_WIDTH), F32),
            jax.ShapeDtypeStruct((b, r, FNET_WIDTH), F32),
        ],
        scratch_shapes=[pltpu.VMEM((SSM_WIDTH // LANES, TILE, LANES), F32)],
        compiler_params=_params(("parallel",)),
        name="in_projection",
    )(xs, mods, g_norm1.reshape(depth, 1, d), w_in, gq, gk, e_mat, cos_t, sin_t)


def _attn_kernel(q_ref, kt_ref, ve_ref, lam_ref, gs_ref, o_ref, *, nt, ctx_len):
    t = pl.program_id(2)
    q = q_ref[0]
    r = ve_ref.shape[2]

    def attend(k0):
        outs = []
        ve = ve_ref[0, 0, k0:, :]
        scores = [jnp.dot(q, kt_ref[0, 0, mp, :, k0:], preferred_element_type=F32)
                  for mp in range(2)]
        for s in scores:
            p = jnp.exp2(s - jnp.max(s, axis=1, keepdims=True))
            acc = jnp.dot(p.astype(BF16), ve, preferred_element_type=F32)
            outs.append(acc[:, :V_DIM] / acc[:, V_DIM:])
        lp = lam_ref[0]
        s1 = jnp.sum(lp[0:1] * lp[1:2], axis=-1, keepdims=True)
        s2 = jnp.sum(lp[2:3] * lp[3:4], axis=-1, keepdims=True)
        lam = jnp.exp(s1) - jnp.exp(s2) + lp[4:5, 0:1]
        a = outs[0] - lam * outs[1]
        a = a * lax.rsqrt(jnp.mean(a * a, axis=-1, keepdims=True) + EPS)
        o_ref[0] = (a * gs_ref[0] * lp[4:5, 1:2]).astype(BF16)

    @pl.when(t < nt - 1)
    def _latent_queries():
        attend(0)

    @pl.when(t == nt - 1)
    def _context_queries():
        attend(r - ctx_len)


def _attention(layer, q, kt, ve, lam_p, g_subln):
    b, r, _ = q.shape
    nt = r // TILE
    return pl.pallas_call(
        functools.partial(_attn_kernel, nt=nt, ctx_len=TILE),
        grid=(b, N_HEADS, nt),
        in_specs=[
            pl.BlockSpec((1, TILE, LANES), lambda i, h, t: (i, t, h)),
            pl.BlockSpec((1, 1, 2, LANES, r), lambda i, h, t: (i, h, 0, 0, 0),
                         pipeline_mode=pl.Buffered(1)),
            pl.BlockSpec((1, 1, r, 2 * V_DIM), lambda i, h, t: (i, h, 0, 0),
                         pipeline_mode=pl.Buffered(1)),
            _const_spec((SUBLANES, LANES), layer),
            _const_spec((1, V_DIM), layer),
        ],
        out_specs=pl.BlockSpec((1, TILE, V_DIM), lambda i, h, t: (i, t, h)),
        out_shape=jax.ShapeDtypeStruct((b, r, ATTN_WIDTH), BF16),
        compiler_params=_params(("parallel", "parallel", "arbitrary")),
        name="diff_attention",
    )(q, kt, ve, lam_p, g_subln)


def _zoh(are_ref, aim_ref, ldt_ref):
    a_re = are_ref[0, 0]
    a_im = aim_ref[0, 0]
    dt = jnp.exp(ldt_ref[0, 0])
    mag = jnp.exp(dt * a_re)
    ang = dt * a_im
    ab_re = mag * jnp.cos(ang)
    ab_im = mag * jnp.sin(ang)
    den = a_re * a_re + a_im * a_im
    n_re = ab_re - 1.0
    f_re = (n_re * a_re + ab_im * a_im) / den
    f_im = (ab_im * a_re - n_re * a_im) / den
    return ab_re, ab_im, f_re, f_im


def _complex_powers(base_re, base_im, n):
    pows = [(jnp.ones_like(base_re), jnp.zeros_like(base_im))]
    for _ in range(n):
        pr, pi = pows[-1]
        pows.append((pr * base_re - pi * base_im, pr * base_im + pi * base_re))
    return pows


def _fill_scan_tables(pw_ref, base_re, base_im, reverse):
    shape = (SUBLANES, SSM_STATE)
    pows = _complex_powers(base_re, base_im, SUBLANES)
    row = lax.broadcasted_iota(jnp.int32, shape, 0)
    zero = jnp.zeros(shape, F32)
    for idx, k in enumerate((1, 2, 4)):
        mask = (row + k <= SUBLANES - 1) if reverse else (row >= k)
        pw_ref[2 * idx] = jnp.where(mask, jnp.broadcast_to(pows[k][0], shape), zero)
        pw_ref[2 * idx + 1] = jnp.where(mask, jnp.broadcast_to(pows[k][1], shape), zero)
    pcr = zero
    pci = zero
    for tt in range(SUBLANES):
        e = (SUBLANES - tt) if reverse else tt + 1
        pcr = jnp.where(row == tt, jnp.broadcast_to(pows[e][0], shape), pcr)
        pci = jnp.where(row == tt, jnp.broadcast_to(pows[e][1], shape), pci)
    pw_ref[6] = pcr
    pw_ref[7] = pci


def _scan_rows(s_ref, pw_ref, carry, nrows, reverse, exclusive, groups=None):
    ns = SSM_STATE
    shape = (SUBLANES, ns)
    ngroups = nrows // SUBLANES
    edge = (SUBLANES - 1) if reverse else 0

    def group(g, carry):
        cr, ci = carry
        gi = (ngroups - 1 - g) if reverse else g
        r0 = pl.multiple_of(gi * SUBLANES, SUBLANES)
        xr = s_ref[pl.ds(r0, SUBLANES), :ns]
        xi = s_ref[pl.ds(r0, SUBLANES), ns:]
        for idx, k in enumerate((1, 2, 4)):
            sh = (SUBLANES - k) if reverse else k
            sr = pltpu.roll(xr, sh, 0)
            si = pltpu.roll(xi, sh, 0)
            pr = pw_ref[2 * idx]
            pi = pw_ref[2 * idx + 1]
            xr, xi = xr + pr * sr - pi * si, xi + pr * si + pi * sr
        pr = pw_ref[6]
        pi = pw_ref[7]
        xr, xi = xr + pr * cr - pi * ci, xi + pr * ci + pi * cr
        if exclusive:
            row = lax.broadcasted_iota(jnp.int32, shape, 0)
            sh = (SUBLANES - 1) if reverse else 1
            er = jnp.where(row == edge, cr, pltpu.roll(xr, sh, 0))
            ei = jnp.where(row == edge, ci, pltpu.roll(xi, sh, 0))
        else:
            er, ei = xr, xi
        s_ref[pl.ds(r0, SUBLANES), :ns] = er
        s_ref[pl.ds(r0, SUBLANES), ns:] = ei
        last = 0 if reverse else SUBLANES - 1
        return (jnp.broadcast_to(xr[last:last + 1], shape),
                jnp.broadcast_to(xi[last:last + 1], shape))

    if groups is None:
        return lax.fori_loop(0, ngroups, group, carry)
    for g in groups:
        carry = group(g, carry)
    return carry


def _s5_ctx_kernel(u_ref, are_ref, aim_ref, ldt_ref, bre_ref, bim_ref, c_ref,
                   o_ref, st_ref, bbar_ref, pw_ref, s_ref, y_ref, *, reverse):
    ns = SSM_STATE
    ab_re, ab_im, f_re, f_im = _zoh(are_ref, aim_ref, ldt_ref)
    bre = bre_ref[0, 0]
    bim = bim_ref[0, 0]
    bbar_ref[:, :ns] = (f_re * bre - f_im * bim).astype(BF16)
    bbar_ref[:, ns:] = (f_re * bim + f_im * bre).astype(BF16)
    _fill_scan_tables(pw_ref, ab_re, ab_im, reverse)
    s_ref[...] = jnp.dot(u_ref[0].astype(BF16), bbar_ref[...], preferred_element_type=F32)
    zero = jnp.zeros((SUBLANES, ns), F32)
    cr, ci = _scan_rows(s_ref, pw_ref, (zero, zero), TILE, reverse, exclusive=False)
    st_ref[0, 0] = cr
    st_ref[0, 1] = ci
    y = jnp.dot(s_ref[...].astype(BF16), c_ref[0, 0], preferred_element_type=F32)
    _rows_to_blocked(y, y_ref, o_ref.at[0], S5_BLOCK, tile_major=True)


def _s5_block_kernel(u8_ref, s0_ref, are_ref, aim_ref, ldt_ref, bre_ref, bim_ref, ctre_ref,
                     ctim_ref, o_ref, wx_ref, wc_ref, wt_ref, pw_ref, carry_ref, s_ref,
                     *, reverse):
    ns = SSM_STATE
    w = SSM_WIDTH
    nb = S5_BLOCK
    nh = w // LANES
    hs = ns // nh
    hw = nb * LANES
    c = pl.program_id(0)
    b = pl.program_id(1)

    @pl.when((c == 0) & (b == 0))
    def _build_maps():
        ab_re, ab_im, f_re, f_im = _zoh(are_ref, aim_ref, ldt_ref)
        bre = bre_ref[0, 0]
        bim = bim_ref[0, 0]
        bb_re = f_re * bre - f_im * bim
        bb_im = f_re * bim + f_im * bre
        pows = _complex_powers(ab_re, ab_im, nb)
        ctre = ctre_ref[0, 0]
        ctim = ctim_ref[0, 0]
        ct_stack = jnp.concatenate([ctre, -ctim], axis=1).astype(BF16)
        chan = [slice(h * LANES, (h + 1) * LANES) for h in range(nh)]
        stat = [slice(h * hs, (h + 1) * hs) for h in range(nh)]
        tok = [slice(t * LANES, (t + 1) * LANES) for t in range(nb)]
        taps = []
        for e in range(nb):
            pr, pi = pows[e]
            xr = pr * bb_re - pi * bb_im
            xi = pr * bb_im + pi * bb_re
            i = e if reverse else nb - 1 - e
            for h in range(nh):
                wx_ref[h, tok[i], :hs] = xr[chan[h], stat[h]].astype(BF16)
                wx_ref[h, tok[i], hs:] = xi[chan[h], stat[h]].astype(BF16)
            xk = jnp.concatenate([xr, xi], axis=1).astype(BF16)
            taps.append(lax.dot_general(xk, ct_stack, (((1,), (1,)), ((), ())),
                                        preferred_element_type=F32).astype(BF16))
        zero_blk = jnp.zeros((LANES, LANES), BF16)
        for i in range(nb):
            for j in range(nb):
                lag = (i - j) if reverse else (j - i)
                for h in range(nh):
                    wt_ref[h, tok[i], tok[j]] = taps[lag][chan[h], chan[h]] if lag >= 0 else zero_blk
        for j in range(nb):
            pr, pi = pows[nb - j] if reverse else pows[j + 1]
            c_re = ctre * pr - ctim * pi
            c_im = -(ctre * pi + ctim * pr)
            for h in range(nh):
                wc_ref[h, :hs, tok[j]] = c_re[chan[h], stat[h]].T.astype(BF16)
                wc_ref[h, hs:, tok[j]] = c_im[chan[h], stat[h]].T.astype(BF16)
        _fill_scan_tables(pw_ref, pows[nb][0], pows[nb][1], reverse)

    @pl.when(c == 0)
    def _load_state():
        carry_ref[b] = s0_ref[0]

    u8 = u8_ref[0].astype(BF16)
    for h in range(nh):
        x = jnp.dot(u8[:, h * hw:(h + 1) * hw], wx_ref[h], preferred_element_type=F32)
        s_ref[:, h * hs:(h + 1) * hs] = x[:, :hs]
        s_ref[:, ns + h * hs:ns + (h + 1) * hs] = x[:, hs:]
    carry = (carry_ref[b, 0], carry_ref[b, 1])
    pairs = nb // 2
    per = (S5_CHUNK_ROWS // SUBLANES) // (nh * pairs)
    for h in range(nh):
        for jp in range(pairs):
            j0 = 2 * jp
            rows = slice(j0 * LANES, hw) if reverse else slice(0, (j0 + 2) * LANES)
            cols = slice(j0 * LANES, (j0 + 2) * LANES)
            o_ref[0, :, h * hw + cols.start:h * hw + cols.stop] = jnp.dot(
                u8[:, h * hw + rows.start:h * hw + rows.stop], wt_ref[h, rows, cols],
                preferred_element_type=F32)
            k = h * pairs + jp
            carry = _scan_rows(s_ref, pw_ref, carry, S5_CHUNK_ROWS, reverse, exclusive=True,
                               groups=range(k * per, (k + 1) * per))
    carry_ref[b, 0] = carry[0]
    carry_ref[b, 1] = carry[1]
    for h in range(nh):
        y = (jnp.dot(s_ref[:, h * hs:(h + 1) * hs].astype(BF16), wc_ref[h, :hs],
                     preferred_element_type=F32)
             + jnp.dot(s_ref[:, ns + h * hs:ns + (h + 1) * hs].astype(BF16), wc_ref[h, hs:],
                       preferred_element_type=F32))
        o_ref[0, :, h * hw:(h + 1) * hw] = o_ref[0, :, h * hw:(h + 1) * hw] + y


def _s5_direction(layer, direction, u, u8, a_re, a_im, ldt, b_re, b_im, ct_re, ct_im, c_stack):
    b, r, w = u.shape
    nt = r // TILE
    rows = r // S5_BLOCK
    wide = S5_BLOCK * w
    tile_rows = TILE // S5_BLOCK
    ns = SSM_STATE
    reverse = direction == 1

    def dir_spec(shape):
        nd = len(shape)
        return pl.BlockSpec((1, 1) + tuple(shape), lambda *_: (layer, direction) + (0,) * nd,
                            pipeline_mode=pl.Buffered(1))

    y_ctx, state = pl.pallas_call(
        functools.partial(_s5_ctx_kernel, reverse=reverse),
        grid=(b,),
        in_specs=[pl.BlockSpec((1, TILE, w), lambda i: (i, nt - 1, 0)),
                  dir_spec((1, ns)), dir_spec((1, ns)), dir_spec((1, ns)),
                  dir_spec((w, ns)), dir_spec((w, ns)), dir_spec((2 * ns, w))],
        out_specs=[pl.BlockSpec((1, tile_rows, wide), lambda i: (i, 0, 0)),
                   pl.BlockSpec((1, 2, SUBLANES, ns), lambda i: (i, 0, 0, 0))],
        out_shape=[jax.ShapeDtypeStruct((b, tile_rows, wide), F32),
                   jax.ShapeDtypeStruct((b, 2, SUBLANES, ns), F32)],
        scratch_shapes=[pltpu.VMEM((w, 2 * ns), BF16),
                        pltpu.VMEM((8, SUBLANES, ns), F32),
                        pltpu.VMEM((TILE, 2 * ns), F32),
                        pltpu.VMEM((w // LANES, TILE, LANES), F32)],
        compiler_params=_params(("parallel",)),
        name="s5_context",
    )(u, a_re, a_im, ldt, b_re, b_im, c_stack)

    nchunks = (r - TILE) // (S5_BLOCK * S5_CHUNK_ROWS)

    def chunk(c, i):
        return (i, nchunks - 1 - c if reverse else c, 0)

    y_lat = pl.pallas_call(
        functools.partial(_s5_block_kernel, reverse=reverse),
        grid=(nchunks, b),
        in_specs=[pl.BlockSpec((1, S5_CHUNK_ROWS, wide), chunk),
                  pl.BlockSpec((1, 2, SUBLANES, ns), lambda c, i: (i, 0, 0, 0)),
                  dir_spec((1, ns)), dir_spec((1, ns)), dir_spec((1, ns)),
                  dir_spec((w, ns)), dir_spec((w, ns)), dir_spec((w, ns)), dir_spec((w, ns))],
        out_specs=pl.BlockSpec((1, S5_CHUNK_ROWS, wide), chunk),
        out_shape=jax.ShapeDtypeStruct((b, rows - tile_rows, wide), F32),
        scratch_shapes=[pltpu.VMEM((w // LANES, wide // (w // LANES), 2 * ns // (w // LANES)), BF16),
                        pltpu.VMEM((w // LANES, 2 * ns // (w // LANES), wide // (w // LANES)), BF16),
                        pltpu.VMEM((w // LANES, wide // (w // LANES), wide // (w // LANES)), BF16),
                        pltpu.VMEM((8, SUBLANES, ns), F32),
                        pltpu.VMEM((b, 2, SUBLANES, ns), F32),
                        pltpu.VMEM((S5_CHUNK_ROWS, 2 * ns), F32)],
        compiler_params=_params(("arbitrary", "arbitrary"), vmem=S5_VMEM_LIMIT),
        name="s5_blocks_reverse" if reverse else "s5_blocks_forward",
    )(u8, state, a_re, a_im, ldt, b_re, b_im, ct_re, ct_im)
    return y_lat, y_ctx


def _fourier_outer_kernel(w_ref, x_ref, g_ref):
    l2n, rows, w = x_ref.shape[1:]
    x = x_ref[0].reshape(l2n, rows * w).astype(BF16)
    g = jnp.dot(w_ref[...], x, preferred_element_type=F32)
    g_ref[0] = g.reshape(g_ref.shape[1:]).astype(BF16)


def _fourier_outer(f, w_outer):
    b, r, w = f.shape
    nt = r // TILE
    l2n = nt - 1
    flat = TILE * w
    return pl.pallas_call(
        _fourier_outer_kernel,
        grid=(b, flat // FOURIER_CHUNK),
        in_specs=[_const_spec((2 * l2n, l2n)),
                  pl.BlockSpec((1, l2n, FOURIER_CHUNK // w, w), lambda i, j: (i, 0, j, 0))],
        out_specs=pl.BlockSpec((1, 2 * l2n, FOURIER_CHUNK // w, w), lambda i, j: (i, 0, j, 0)),
        out_shape=jax.ShapeDtypeStruct((b, 2 * l2n, TILE, w), BF16),
        compiler_params=_params(("parallel", "parallel")),
        name="fourier_outer",
    )(w_outer, f.reshape(b, nt, TILE, w))


def _fourier_tail(y_re, y_im, c64_ref, wf_ref):
    w = FNET_WIDTH
    z = (jnp.dot(y_re.astype(BF16), c64_ref[:w], preferred_element_type=F32)
         + jnp.dot(y_im.astype(BF16), c64_ref[w:], preferred_element_type=F32))
    return jnp.dot(z.astype(BF16), wf_ref[0], preferred_element_type=F32)


def _fourier_inner_kernel(gr_ref, gi_ref, m_ref, c64_ref, wf_ref, o_ref):
    w = FNET_WIDTH
    ys = [jnp.dot(m_ref[i, :, :TILE], gr_ref[0, i], preferred_element_type=F32)
          + jnp.dot(m_ref[i, :, TILE:], gi_ref[0, i], preferred_element_type=F32)
          for i in range(FOURIER_GROUP)]
    for i, y in enumerate(ys):
        o_ref[0, :, i * w:(i + 1) * w] = _fourier_tail(y[:TILE], y[TILE:], c64_ref, wf_ref)


def _fourier_inner(layer, g, m_tab, c64_tab, wf_blk):
    b, two_l2n, _, w = g.shape
    l2n = two_l2n // 2
    steps = l2n // FOURIER_GROUP
    g4 = g
    return pl.pallas_call(
        _fourier_inner_kernel,
        grid=(b, steps),
        in_specs=[
            pl.BlockSpec((1, FOURIER_GROUP, TILE, w), lambda i, j: (i, j, 0, 0)),
            pl.BlockSpec((1, FOURIER_GROUP, TILE, w), lambda i, j: (i, steps + j, 0, 0)),
            pl.BlockSpec((FOURIER_GROUP, 2 * TILE, 2 * TILE), lambda i, j: (j, 0, 0)),
            _const_spec((2 * w, w)),
            _const_spec((w, w), layer),
        ],
        out_specs=pl.BlockSpec((1, TILE, FOURIER_GROUP * w), lambda i, j: (i, 0, j)),
        out_shape=jax.ShapeDtypeStruct((b, TILE, l2n * w), F32),
        compiler_params=_params(("parallel", "parallel")),
        name="fourier_inner",
    )(g4, g4, m_tab, c64_tab, wf_blk)


def _fourier_ctx_kernel(x_ref, m_ref, c64_ref, wf_ref, o_ref, z_ref, *, l2n):
    y = jnp.dot(m_ref[...], x_ref[0].astype(BF16), preferred_element_type=F32)
    z = _fourier_tail(y[:TILE], y[TILE:], c64_ref, wf_ref)
    _rows_to_blocked(z, z_ref, o_ref.at[0], l2n)


def _fourier_context(layer, f, m_ctx, c64_tab, wf_blk):
    b, r, w = f.shape
    nt = r // TILE
    l2n = nt - 1
    return pl.pallas_call(
        functools.partial(_fourier_ctx_kernel, l2n=l2n),
        grid=(b,),
        in_specs=[
            pl.BlockSpec((1, TILE, w), lambda i: (i, nt - 1, 0)),
            _const_spec((2 * TILE, TILE)),
            _const_spec((2 * w, w)),
            _const_spec((w, w), layer),
        ],
        out_specs=pl.BlockSpec((1, TILE // l2n, l2n * w), lambda i: (i, 0, 0)),
        out_shape=jax.ShapeDtypeStruct((b, TILE // l2n, l2n * w), F32),
        scratch_shapes=[pltpu.VMEM((w // LANES, TILE, LANES), F32)],
        compiler_params=_params(("parallel",)),
        name="fourier_context",
    )(f, m_ctx, c64_tab, wf_blk)


def _outffn_kernel(x_ref, a_ref, yf_ref, yb_ref, yfc_ref, ybc_ref, u_ref, fn_ref, fnc_ref, mod_ref,
                   dsk_ref, wglu_ref, bglu_ref, wo_ref, g2_ref, w1_ref, w2_ref, o_ref, ytok_ref,
                   ftok_ref, *, l2n, nt):
    d = D_MODEL
    bsz = x_ref.shape[0]
    o1 = ATTN_WIDTH
    o2 = ATTN_WIDTH + SSM_WIDTH
    is_ctx = pl.program_id(0) == nt - 1

    def mixing_head(i):
        mod = mod_ref[0, pl.ds(jnp.where(is_ctx, bsz, i), 1), :]
        y_blk = jnp.where(is_ctx, yfc_ref[i] + ybc_ref[i], yf_ref[i] + yb_ref[i])
        y_scan = _blocked_to_rows(y_blk, ytok_ref.at[i], S5_BLOCK, tile_major=True)
        fnet = _blocked_to_rows(jnp.where(is_ctx, fnc_ref[i], fn_ref[i]), ftok_ref.at[i], l2n)
        hg = jax.nn.gelu(y_scan + dsk_ref[0] * u_ref[i])
        zg = jnp.dot(hg.astype(BF16), wglu_ref[0], preferred_element_type=F32) + bglu_ref[0]
        s5 = (hg * jax.nn.sigmoid(zg)).astype(BF16)
        mix = (jnp.dot(a_ref[i], wo_ref[0, :o1], preferred_element_type=F32)
               + jnp.dot(s5, wo_ref[0, o1:o2], preferred_element_type=F32)
               + jnp.dot(fnet.astype(BF16), wo_ref[0, o2:], preferred_element_type=F32))
        x1 = x_ref[i] + mod[:, 2 * d:3 * d] * mix
        ms = jnp.mean(x1 * x1, axis=-1, keepdims=True)
        hn = x1 * lax.rsqrt(ms + EPS) * g2_ref[0]
        h = (hn * (1.0 + mod[:, 4 * d:5 * d]) + mod[:, 3 * d:4 * d]).astype(BF16)
        return x1, h, mod[:, 5 * d:]

    def mlp(i, x1, h, gate):
        acc = jnp.zeros((TILE, d), F32)
        for cc in range(D_FF // FF_CHUNK):
            sl = slice(cc * FF_CHUNK, (cc + 1) * FF_CHUNK)
            t = jnp.dot(h, w1_ref[0, :, sl], preferred_element_type=F32)
            t = jnp.square(jnp.maximum(t, 0.0))
            acc = acc + jnp.dot(t.astype(BF16), w2_ref[0, sl, :], preferred_element_type=F32)
        o_ref[i] = x1 + gate * acc

    heads = [mixing_head(i) for i in range(bsz)]
    for i, (x1, h, gate) in enumerate(heads):
        mlp(i, x1, h, gate)


def _out_ffn(layer, xs, attn, y_fwd, y_bwd, u, fn, fn_ctx, mods, d_skip, w_glu, b_glu, w_out,
             g_norm2, w_ff1, w_ff2, latent_only):
    b, r, d = xs.shape
    nt = r // TILE
    l2n = nt - 1
    depth = w_out.shape[0]
    s5_shape = (b, TILE // S5_BLOCK, S5_BLOCK * SSM_WIDTH)
    fn_shape = (b, TILE // l2n, l2n * FNET_WIDTH)

    def latent(shape):
        return pl.BlockSpec(shape, lambda t: (0, jnp.minimum(t, nt - 2), 0))

    def context(shape):
        return pl.BlockSpec(shape, lambda t: (0, 0, 0))

    def tile(width):
        return pl.BlockSpec((b, TILE, width), lambda t: (0, t, 0))

    return pl.pallas_call(
        functools.partial(_outffn_kernel, l2n=l2n, nt=nt),
        grid=(nt - 1 if latent_only else nt,),
        in_specs=[
            tile(d), tile(ATTN_WIDTH), latent(s5_shape), latent(s5_shape), context(s5_shape),
            context(s5_shape), tile(SSM_WIDTH), latent(fn_shape), context(fn_shape),
            _const_spec((SUBLANES, 6 * d), layer),
            _const_spec((1, SSM_WIDTH), layer),
            _const_spec((SSM_WIDTH, SSM_WIDTH), layer),
            _const_spec((1, SSM_WIDTH), layer),
            _const_spec((d, d), layer),
            _const_spec((1, d), layer),
            _const_spec((d, D_FF), layer),
            _const_spec((D_FF, d), layer),
        ],
        out_specs=tile(d),
        out_shape=jax.ShapeDtypeStruct((b, r - TILE if latent_only else r, d), F32),
        scratch_shapes=[pltpu.VMEM((b, SSM_WIDTH // LANES, TILE, LANES), F32),
                        pltpu.VMEM((b, FNET_WIDTH // LANES, TILE, LANES), F32)],
        input_output_aliases={} if latent_only else {0: 0},
        compiler_params=_params(("parallel",)),
        name="out_ffn",
    )(xs, attn, y_fwd[0], y_bwd[0], y_fwd[1], y_bwd[1], u, fn, fn_ctx, mods, d_skip, w_glu, b_glu,
      w_out, g_norm2.reshape(depth, 1, d), w_ff1, w_ff2)


def _head_lane_tables():
    j = np.arange(LANES)
    half = j // 64
    mp = (j // 32) % 2
    idx = j % 32
    src_in_head = mp * HEAD_DIM + half * 32 + idx
    gain_idx = half * 32 + idx
    return src_in_head, gain_idx, mp


def _in_proj_column_order():
    src_in_head, _, _ = _head_lane_tables()
    qk = np.concatenate([h * LANES + src_in_head for h in range(N_HEADS)])
    return np.concatenate([qk, QK_WIDTH + qk, np.arange(2 * QK_WIDTH, IN_WIDTH)])


def _same_map_matrix():
    i = np.arange(256)
    head = i // LANES
    mp = ((i % LANES) // 32) % 2
    same = (head[:, None] == head[None, :]) & (mp[:, None] == mp[None, :])
    return same.astype(np.float32)


def _rope_tables(seq, ctx_len):
    t = jnp.arange(seq)
    row = (t // GRID_W).astype(F32)
    col = (t % GRID_W).astype(F32)
    n_freq = HEAD_DIM // 4
    inv = jnp.power(ROPE_BASE, -jnp.arange(n_freq, dtype=F32) / n_freq)
    ang = jnp.concatenate([row[:, None] * inv, col[:, None] * inv], axis=-1)
    cos = jnp.tile(jnp.cos(ang), (1, 4))
    sign = np.where(np.arange(LANES) < LANES // 2, -1.0, 1.0).astype(np.float32)
    sin = jnp.tile(jnp.sin(ang), (1, 4)) * sign
    cos = jnp.concatenate([cos, jnp.ones((ctx_len, LANES), F32)], axis=0)
    sin = jnp.concatenate([sin, jnp.zeros((ctx_len, LANES), F32)], axis=0)
    return cos, sin


def _dft_tables(length):
    l2n = length // TILE
    k2 = np.arange(l2n)
    phi = 2.0 * np.pi * ((k2[:, None] * k2[None, :]) % l2n) / l2n
    w_outer = np.concatenate([np.cos(phi), -np.sin(phi)], axis=0).astype(np.float32)
    l1 = np.arange(TILE)
    alpha = 2.0 * np.pi * ((k2[:, None] * l1[None, :]) % length) / length
    beta = 2.0 * np.pi * ((l1[:, None] * l1[None, :]) % TILE) / TILE
    ca = jnp.asarray(np.cos(alpha).astype(np.float32))[:, None, :]
    sa = jnp.asarray(np.sin(alpha).astype(np.float32))[:, None, :]
    cb = jnp.asarray(np.cos(beta).astype(np.float32))[None]
    sb = jnp.asarray(np.sin(beta).astype(np.float32))[None]
    norm = 1.0 / math.sqrt(length)
    ct = (ca * cb - sa * sb) * norm
    st = (sa * cb + ca * sb) * norm
    m = jnp.concatenate([jnp.concatenate([ct, st], axis=2),
                         jnp.concatenate([-st, ct], axis=2)], axis=1).astype(BF16)
    return jnp.asarray(w_outer).astype(BF16), m


def _tile_dft_table():
    l1 = np.arange(TILE)
    beta = 2.0 * np.pi * ((l1[:, None] * l1[None, :]) % TILE) / TILE
    tab = np.concatenate([np.cos(beta), -np.sin(beta)], axis=0) / math.sqrt(TILE)
    return jnp.asarray(tab.astype(np.float32)).astype(BF16)


def _channel_dft_table():
    c = np.arange(FNET_C)
    th = 2.0 * np.pi * ((c[:, None] * c[None, :]) % FNET_C) / FNET_C
    eye = np.eye(FNET_G)
    norm = 1.0 / math.sqrt(FNET_C)
    cblk = np.kron(eye, np.cos(th)) * norm
    sblk = np.kron(eye, np.sin(th)) * norm
    return jnp.asarray(np.concatenate([cblk, sblk], axis=0).astype(np.float32)).astype(BF16)


def _block_diag(blocks):
    g = blocks.shape[-3]
    eye = jnp.eye(g, dtype=blocks.dtype)
    out = jnp.einsum('...gab,gh->...gahb', blocks, eye)
    return out.reshape(blocks.shape[:-3] + (g * blocks.shape[-2], g * blocks.shape[-1]))


def kernel(x, c, ctx, c_ctx, w_mod, b_mod, g_norm1, w_in, g_qnorm, g_knorm, lam_q1, lam_k1, lam_q2, lam_k2, g_subln, ssm_a_re, ssm_a_im, ssm_log_dt, ssm_b_re, ssm_b_im, ssm_c_re, ssm_c_im, ssm_d, w_glu, b_glu, w_fnet, w_out, g_norm2, w_ff1, w_ff2):
    bsz, seq, d = x.shape
    ctx_len = ctx.shape[1]
    depth = w_mod.shape[0]
    assert d == D_MODEL and ctx_len == TILE and seq % TILE == 0 and seq % GRID_W == 0
    assert bsz + 1 <= SUBLANES

    assert (seq // TILE) % FOURIER_GROUP == 0 and seq % (S5_BLOCK * S5_CHUNK_ROWS) == 0
    xs = jnp.concatenate([x, ctx], axis=1)
    act = jnp.concatenate([c, c_ctx[None], jnp.zeros((SUBLANES - bsz - 1, d), F32)], axis=0)
    mods = _modulation(act, w_mod, b_mod)

    w_in_p = jnp.take(w_in, jnp.asarray(_in_proj_column_order()), axis=2).astype(BF16)
    _, gain_idx, _ = _head_lane_tables()
    gq = jnp.tile(g_qnorm[:, gain_idx], (1, N_HEADS)).reshape(depth, 1, QK_WIDTH)
    gk = jnp.tile(g_knorm[:, gain_idx], (1, N_HEADS)).reshape(depth, 1, QK_WIDTH)
    e_mat = jnp.asarray(_same_map_matrix()).astype(BF16)
    cos_t, sin_t = _rope_tables(seq, ctx_len)
    lam_init = np.array([0.8 - 0.6 * math.exp(-0.3 * i) for i in range(depth)], np.float32)
    pad = jnp.zeros((depth, LANES - HEAD_DIM), F32)
    lam_rows = [jnp.concatenate([v, pad], axis=1) for v in (lam_q1, lam_k1, lam_q2, lam_k2)]
    const_row = np.zeros((depth, LANES), np.float32)
    const_row[:, 0] = lam_init
    const_row[:, 1] = 1.0 - lam_init
    lam_p = jnp.stack(lam_rows + [jnp.asarray(const_row)]
                      + [jnp.zeros((depth, LANES), F32)] * (SUBLANES - 5), axis=1)
    g_sub = g_subln.reshape(depth, 1, V_DIM)

    a_re = ssm_a_re.reshape(depth, 2, 1, SSM_STATE)
    a_im = ssm_a_im.reshape(depth, 2, 1, SSM_STATE)
    ldt = jnp.repeat(ssm_log_dt, SSM_N, axis=-1).reshape(depth, 2, 1, SSM_STATE)
    b_re = _block_diag(jnp.swapaxes(ssm_b_re, -1, -2))
    b_im = _block_diag(jnp.swapaxes(ssm_b_im, -1, -2))
    c_stack = jnp.concatenate([_block_diag(jnp.swapaxes(ssm_c_re, -1, -2)),
                               -_block_diag(jnp.swapaxes(ssm_c_im, -1, -2))],
                              axis=2).astype(BF16)
    ct_re = _block_diag(ssm_c_re)
    ct_im = _block_diag(ssm_c_im)
    d_skip = ssm_d.reshape(depth, 1, SSM_WIDTH)
    w_glu_b = w_glu.astype(BF16)
    b_glu_r = b_glu.reshape(depth, 1, SSM_WIDTH)

    w_outer, m_lat = _dft_tables(seq)
    m_ctx = _tile_dft_table()
    c64_tab = _channel_dft_table()
    wf_blk = _block_diag(w_fnet).astype(BF16)

    w_out_b = w_out.astype(BF16)
    w_ff1_b = w_ff1.astype(BF16)
    w_ff2_b = w_ff2.astype(BF16)

    for layer in range(depth):
        q, kt, ve, u, u8, f = _in_projection(layer, xs, mods, g_norm1, w_in_p, gq, gk, e_mat,
                                         cos_t, sin_t)
        attn = _attention(layer, q, kt, ve, lam_p, g_sub)
        y_fwd, y_bwd = [_s5_direction(layer, dr, u, u8, a_re, a_im, ldt, b_re, b_im, ct_re,
                                      ct_im, c_stack) for dr in range(2)]
        fn = _fourier_inner(layer, _fourier_outer(f, w_outer), m_lat, c64_tab, wf_blk)
        fn_ctx = _fourier_context(layer, f, m_ctx, c64_tab, wf_blk)
        xs = _out_ffn(layer, xs, attn, y_fwd, y_bwd, u, fn, fn_ctx, mods, d_skip, w_glu_b, b_glu_r,
                      w_out_b, g_norm2, w_ff1_b, w_ff2_b, latent_only=layer == depth - 1)
    return xs
```

```python
import functools
import math

import numpy as np
import jax
import jax.numpy as jnp
from jax import lax
from jax.experimental import pallas as pl
from jax.experimental.pallas import tpu as pltpu

F32 = jnp.float32
BF16 = jnp.bfloat16

D_MODEL = 1024
GRID_W = 64
N_HEADS = 4
HEAD_DIM = 64
V_DIM = 2 * HEAD_DIM
QK_WIDTH = N_HEADS * 2 * HEAD_DIM
ATTN_WIDTH = N_HEADS * V_DIM
SSM_WIDTH = D_MODEL // 4
SSM_P = 16
SSM_G = SSM_WIDTH // SSM_P
SSM_N = 64
SSM_STATE = SSM_G * SSM_N
FNET_WIDTH = D_MODEL // 4
FNET_G = 4
FNET_C = FNET_WIDTH // FNET_G
IN_WIDTH = 2 * QK_WIDTH + ATTN_WIDTH + SSM_WIDTH + FNET_WIDTH
D_FF = 4 * D_MODEL
ROPE_BASE = 10000.0
EPS = 1e-6
SCALE = HEAD_DIM ** -0.5
LOG2E = math.log2(math.e)

TILE = 256
LANES = 128
SUBLANES = 8
FF_CHUNK = 1024
MOD_BLOCK = 1536
FOURIER_CHUNK = 8192
FOURIER_GROUP = 4
S5_BLOCK = 8
S5_CHUNK_ROWS = 256
VMEM_LIMIT = 48 * 1024 * 1024
S5_VMEM_LIMIT = 56 * 1024 * 1024

def _const_spec(shape, layer=None):
    nd = len(shape)
    if layer is None:
        return pl.BlockSpec(shape, lambda *_: (0,) * nd, pipeline_mode=pl.Buffered(1))
    return pl.BlockSpec((1,) + tuple(shape), lambda *_: (layer,) + (0,) * nd,
                        pipeline_mode=pl.Buffered(1))


def _params(sem, vmem=None):
    return pltpu.CompilerParams(dimension_semantics=sem, vmem_limit_bytes=vmem or VMEM_LIMIT)


def _blocked_lane(j, h, period, nh, tile_major):
    return ((h * period + j) if tile_major else (j * nh + h)) * LANES


def _rows_to_blocked(tok, scr_ref, blk_ref, period, tile_major=False):
    n, w = tok.shape
    nh = w // LANES
    for h in range(nh):
        scr_ref[h] = tok[:, h * LANES:(h + 1) * LANES]
    for j in range(period):
        for h in range(nh):
            lo = _blocked_lane(j, h, period, nh, tile_major)
            blk_ref[:, lo:lo + LANES] = scr_ref[h, pl.ds(j, n // period, stride=period), :]


def _blocked_to_rows(blk, scr_ref, period, tile_major=False):
    nh, n, _ = scr_ref.shape
    for j in range(period):
        for h in range(nh):
            lo = _blocked_lane(j, h, period, nh, tile_major)
            scr_ref[h, pl.ds(j, n // period, stride=period), :] = blk[:, lo:lo + LANES]
    return jnp.concatenate([scr_ref[h] for h in range(nh)], axis=1)


def _mod_kernel(act_ref, w_ref, b_ref, o_ref):
    a = act_ref[...]
    a = a * jax.nn.sigmoid(a)
    o_ref[0] = jnp.dot(a.astype(BF16), w_ref[0].astype(BF16),
                       preferred_element_type=F32) + b_ref[0]


def _modulation(act, w_mod, b_mod):
    depth, d, n = w_mod.shape
    bn = MOD_BLOCK
    return pl.pallas_call(
        _mod_kernel,
        grid=(depth, n // bn),
        in_specs=[pl.BlockSpec((SUBLANES, d), lambda l, j: (0, 0)),
                  pl.BlockSpec((1, d, bn), lambda l, j: (l, 0, j)),
                  pl.BlockSpec((1, 1, bn), lambda l, j: (l, 0, j))],
        out_specs=pl.BlockSpec((1, SUBLANES, bn), lambda l, j: (l, 0, j)),
        out_shape=jax.ShapeDtypeStruct((depth, SUBLANES, n), F32),
        compiler_params=_params(("parallel", "parallel")),
        name="modulation",
    )(act, w_mod, b_mod.reshape(depth, 1, n))


def _inproj_kernel(x_ref, xc_ref, mod_ref, g1_ref, w_ref, gq_ref, gk_ref, e_ref, cos_ref, sin_ref,
                   q_ref, kt_ref, ve_ref, u_ref, u8_ref, f_ref, stage_ref, *, nt):
    d = D_MODEL
    bsz = x_ref.shape[0]
    is_ctx = pl.program_id(0) == nt - 1
    e = e_ref[...]
    cos = cos_ref[...]
    sin = sin_ref[...]

    def project(i):
        x = jnp.where(is_ctx, xc_ref[i], x_ref[i])
        ms = jnp.mean(x * x, axis=-1, keepdims=True)
        xn = x * lax.rsqrt(ms + EPS) * g1_ref[0]
        mod = mod_ref[0, pl.ds(jnp.where(is_ctx, bsz, i), 1), :]
        h = xn * (1.0 + mod[:, d:2 * d]) + mod[:, :d]
        return jnp.dot(h.astype(BF16), w_ref[0], preferred_element_type=F32)

    def qk_norm(z, g):
        sq = z * z
        hi = sq.astype(BF16)
        lo = (sq - hi.astype(F32)).astype(BF16)
        parts = []
        for j in range(QK_WIDTH // 256):
            sl = slice(j * 256, (j + 1) * 256)
            parts.append(jnp.dot(hi[:, sl], e, preferred_element_type=F32)
                         + jnp.dot(lo[:, sl], e, preferred_element_type=F32))
        ssum = jnp.concatenate(parts, axis=-1)
        return z * lax.rsqrt(ssum * (1.0 / HEAD_DIM) + EPS) * g

    def rope(zh):
        return zh * cos + pltpu.roll(zh, LANES // 2, 1) * sin

    row_map = (lax.broadcasted_iota(jnp.int32, (LANES, TILE), 0) // 32) % 2
    ones = jnp.ones((TILE, LANES), BF16)

    def finish(i, proj):
        qn = qk_norm(proj[:, :QK_WIDTH], gq_ref[0])
        kn = qk_norm(proj[:, QK_WIDTH:2 * QK_WIDTH], gk_ref[0])
        for hh in range(N_HEADS):
            sl = slice(hh * LANES, (hh + 1) * LANES)
            q_ref[i, :, sl] = (rope(qn[:, sl]) * (SCALE * LOG2E)).astype(BF16)
            kt = rope(kn[:, sl]).T
            kt_ref[i, hh, 0] = jnp.where(row_map == 0, kt, 0.0).astype(BF16)
            kt_ref[i, hh, 1] = jnp.where(row_map == 1, kt, 0.0).astype(BF16)
            vo = 2 * QK_WIDTH + hh * V_DIM
            ve_ref[i, hh, :, :V_DIM] = proj[:, vo:vo + V_DIM].astype(BF16)
            ve_ref[i, hh, :, V_DIM:] = ones
        uo = 2 * QK_WIDTH + ATTN_WIDTH
        u_ref[i] = proj[:, uo:uo + SSM_WIDTH]
        _rows_to_blocked(proj[:, uo:uo + SSM_WIDTH], stage_ref, u8_ref.at[i], S5_BLOCK,
                         tile_major=True)
        f_ref[i, 0] = proj[:, uo + SSM_WIDTH:]

    projs = [project(i) for i in range(bsz)]
    for i, proj in enumerate(projs):
        finish(i, proj)


def _stream_specs(x_lat, x_ctx, nt):
    b, _, d = x_lat.shape
    ctx_block = x_ctx.shape[1] // TILE - 1
    return [pl.BlockSpec((b, TILE, d), lambda t: (0, jnp.minimum(t, nt - 2), 0)),
            pl.BlockSpec((b, TILE, d), lambda t: (0, ctx_block, 0))]


def _in_projection(layer, x_lat, x_ctx, r, mods, g_norm1, w_in, gq, gk, e_mat, cos_t, sin_t):
    b, _, d = x_lat.shape
    nt = r // TILE
    depth = w_in.shape[0]
    return pl.pallas_call(
        functools.partial(_inproj_kernel, nt=nt),
        grid=(nt,),
        in_specs=_stream_specs(x_lat, x_ctx, nt) + [
            _const_spec((SUBLANES, 6 * d), layer),
            _const_spec((1, d), layer),
            _const_spec((d, IN_WIDTH), layer),
            _const_spec((1, QK_WIDTH), layer),
            _const_spec((1, QK_WIDTH), layer),
            _const_spec((256, 256)),
            pl.BlockSpec((TILE, LANES), lambda t: (t, 0)),
            pl.BlockSpec((TILE, LANES), lambda t: (t, 0)),
        ],
        out_specs=[
            pl.BlockSpec((b, TILE, QK_WIDTH), lambda t: (0, t, 0)),
            pl.BlockSpec((b, N_HEADS, 2, LANES, TILE), lambda t: (0, 0, 0, 0, t)),
            pl.BlockSpec((b, N_HEADS, TILE, 2 * V_DIM), lambda t: (0, 0, t, 0)),
            pl.BlockSpec((b, TILE, SSM_WIDTH), lambda t: (0, t, 0)),
            pl.BlockSpec((b, TILE // S5_BLOCK, S5_BLOCK * SSM_WIDTH), lambda t: (0, t, 0)),
            pl.BlockSpec((b, 1, TILE, FNET_WIDTH), lambda t: (0, t, 0, 0)),
        ],
        out_shape=[
            jax.ShapeDtypeStruct((b, r, QK_WIDTH), BF16),
            jax.ShapeDtypeStruct((b, N_HEADS, 2, LANES, r), BF16),
            jax.ShapeDtypeStruct((b, N_HEADS, r, 2 * V_DIM), BF16),
            jax.ShapeDtypeStruct((b, r, SSM_WIDTH), F32),
            jax.ShapeDtypeStruct((b, r // S5_BLOCK, S5_BLOCK * SSM_WIDTH), F32),
            jax.ShapeDtypeStruct((b, nt, TILE, FNET_WIDTH), F32),
        ],
        scratch_shapes=[pltpu.VMEM((SSM_WIDTH // LANES, TILE, LANES), F32)],
        compiler_params=_params(("parallel",)),
        name="in_projection",
    )(x_lat, x_ctx, mods, g_norm1.reshape(depth, 1, d), w_in, gq, gk, e_mat, cos_t, sin_t)


def _attn_kernel(q_ref, kt_ref, ve_ref, lam_ref, gs_ref, o_ref, *, nt, ctx_len):
    t = pl.program_id(2)
    q = q_ref[0]
    r = ve_ref.shape[2]

    def attend(k0):
        outs = []
        ve = ve_ref[0, 0, k0:, :]
        scores = [jnp.dot(q, kt_ref[0, 0, mp, :, k0:], preferred_element_type=F32)
                  for mp in range(2)]
        for s in scores:
            p = jnp.exp2(s - jnp.max(s, axis=1, keepdims=True))
            acc = jnp.dot(p.astype(BF16), ve, preferred_element_type=F32)
            outs.append(acc[:, :V_DIM] / acc[:, V_DIM:])
        lp = lam_ref[0]
        s1 = jnp.sum(lp[0:1] * lp[1:2], axis=-1, keepdims=True)
        s2 = jnp.sum(lp[2:3] * lp[3:4], axis=-1, keepdims=True)
        lam = jnp.exp(s1) - jnp.exp(s2) + lp[4:5, 0:1]
        a = outs[0] - lam * outs[1]
        a = a * lax.rsqrt(jnp.mean(a * a, axis=-1, keepdims=True) + EPS)
        o_ref[0] = (a * gs_ref[0] * lp[4:5, 1:2]).astype(BF16)

    @pl.when(t < nt - 1)
    def _latent_queries():
        attend(0)

    @pl.when(t == nt - 1)
    def _context_queries():
        attend(r - ctx_len)


def _attention(layer, q, kt, ve, lam_p, g_subln):
    b, r, _ = q.shape
    nt = r // TILE
    return pl.pallas_call(
        functools.partial(_attn_kernel, nt=nt, ctx_len=TILE),
        grid=(b, N_HEADS, nt),
        in_specs=[
            pl.BlockSpec((1, TILE, LANES), lambda i, h, t: (i, t, h)),
            pl.BlockSpec((1, 1, 2, LANES, r), lambda i, h, t: (i, h, 0, 0, 0),
                         pipeline_mode=pl.Buffered(1)),
            pl.BlockSpec((1, 1, r, 2 * V_DIM), lambda i, h, t: (i, h, 0, 0),
                         pipeline_mode=pl.Buffered(1)),
            _const_spec((SUBLANES, LANES), layer),
            _const_spec((1, V_DIM), layer),
        ],
        out_specs=pl.BlockSpec((1, TILE, V_DIM), lambda i, h, t: (i, t, h)),
        out_shape=jax.ShapeDtypeStruct((b, r, ATTN_WIDTH), BF16),
        compiler_params=_params(("parallel", "parallel", "arbitrary")),
        name="diff_attention",
    )(q, kt, ve, lam_p, g_subln)


def _zoh(are_ref, aim_ref, ldt_ref):
    a_re = are_ref[0, 0]
    a_im = aim_ref[0, 0]
    dt = jnp.exp(ldt_ref[0, 0])
    mag = jnp.exp(dt * a_re)
    ang = dt * a_im
    ab_re = mag * jnp.cos(ang)
    ab_im = mag * jnp.sin(ang)
    den = a_re * a_re + a_im * a_im
    n_re = ab_re - 1.0
    f_re = (n_re * a_re + ab_im * a_im) / den
    f_im = (ab_im * a_re - n_re * a_im) / den
    return ab_re, ab_im, f_re, f_im


def _complex_powers(base_re, base_im, n):
    pows = [(jnp.ones_like(base_re), jnp.zeros_like(base_im))]
    for _ in range(n):
        pr, pi = pows[-1]
        pows.append((pr * base_re - pi * base_im, pr * base_im + pi * base_re))
    return pows


def _fill_scan_tables(pw_ref, base_re, base_im, reverse):
    shape = (SUBLANES, SSM_STATE)
    pows = _complex_powers(base_re, base_im, SUBLANES)
    row = lax.broadcasted_iota(jnp.int32, shape, 0)
    zero = jnp.zeros(shape, F32)
    for idx, k in enumerate((1, 2, 4)):
        mask = (row + k <= SUBLANES - 1) if reverse else (row >= k)
        pw_ref[2 * idx] = jnp.where(mask, jnp.broadcast_to(pows[k][0], shape), zero)
        pw_ref[2 * idx + 1] = jnp.where(mask, jnp.broadcast_to(pows[k][1], shape), zero)
    pcr = zero
    pci = zero
    for tt in range(SUBLANES):
        e = (SUBLANES - tt) if reverse else tt + 1
        pcr = jnp.where(row == tt, jnp.broadcast_to(pows[e][0], shape), pcr)
        pci = jnp.where(row == tt, jnp.broadcast_to(pows[e][1], shape), pci)
    pw_ref[6] = pcr
    pw_ref[7] = pci


def _scan_rows(s_ref, pw_ref, carry, nrows, reverse, exclusive, groups=None):
    ns = SSM_STATE
    shape = (SUBLANES, ns)
    ngroups = nrows // SUBLANES
    edge = (SUBLANES - 1) if reverse else 0

    def group(g, carry):
        cr, ci = carry
        gi = (ngroups - 1 - g) if reverse else g
        r0 = pl.multiple_of(gi * SUBLANES, SUBLANES)
        xr = s_ref[pl.ds(r0, SUBLANES), :ns]
        xi = s_ref[pl.ds(r0, SUBLANES), ns:]
        for idx, k in enumerate((1, 2, 4)):
            sh = (SUBLANES - k) if reverse else k
            sr = pltpu.roll(xr, sh, 0)
            si = pltpu.roll(xi, sh, 0)
            pr = pw_ref[2 * idx]
            pi = pw_ref[2 * idx + 1]
            xr, xi = xr + pr * sr - pi * si, xi + pr * si + pi * sr
        pr = pw_ref[6]
        pi = pw_ref[7]
        xr, xi = xr + pr * cr - pi * ci, xi + pr * ci + pi * cr
        if exclusive:
            row = lax.broadcasted_iota(jnp.int32, shape, 0)
            sh = (SUBLANES - 1) if reverse else 1
            er = jnp.where(row == edge, cr, pltpu.roll(xr, sh, 0))
            ei = jnp.where(row == edge, ci, pltpu.roll(xi, sh, 0))
        else:
            er, ei = xr, xi
        s_ref[pl.ds(r0, SUBLANES), :ns] = er
        s_ref[pl.ds(r0, SUBLANES), ns:] = ei
        last = 0 if reverse else SUBLANES - 1
        return (jnp.broadcast_to(xr[last:last + 1], shape),
                jnp.broadcast_to(xi[last:last + 1], shape))

    if groups is None:
        return lax.fori_loop(0, ngroups, group, carry)
    for g in groups:
        carry = group(g, carry)
    return carry


def _s5_ctx_kernel(u_ref, are_ref, aim_ref, ldt_ref, bre_ref, bim_ref, c_ref,
                   o_ref, st_ref, bbar_ref, pw_ref, s_ref, y_ref, *, reverse):
    ns = SSM_STATE
    ab_re, ab_im, f_re, f_im = _zoh(are_ref, aim_ref, ldt_ref)
    bre = bre_ref[0, 0]
    bim = bim_ref[0, 0]
    bbar_ref[:, :ns] = (f_re * bre - f_im * bim).astype(BF16)
    bbar_ref[:, ns:] = (f_re * bim + f_im * bre).astype(BF16)
    _fill_scan_tables(pw_ref, ab_re, ab_im, reverse)
    s_ref[...] = jnp.dot(u_ref[0].astype(BF16), bbar_ref[...], preferred_element_type=F32)
    zero = jnp.zeros((SUBLANES, ns), F32)
    cr, ci = _scan_rows(s_ref, pw_ref, (zero, zero), TILE, reverse, exclusive=False)
    st_ref[0, 0] = cr
    st_ref[0, 1] = ci
    y = jnp.dot(s_ref[...].astype(BF16), c_ref[0, 0], preferred_element_type=F32)
    _rows_to_blocked(y, y_ref, o_ref.at[0], S5_BLOCK, tile_major=True)


def _s5_block_kernel(u8_ref, s0_ref, are_ref, aim_ref, ldt_ref, bre_ref, bim_ref, ctre_ref,
                     ctim_ref, o_ref, wx_ref, wc_ref, wt_ref, pw_ref, carry_ref, s_ref,
                     *, reverse):
    ns = SSM_STATE
    w = SSM_WIDTH
    nb = S5_BLOCK
    nh = w // LANES
    hs = ns // nh
    hw = nb * LANES
    c = pl.program_id(0)
    b = pl.program_id(1)

    @pl.when((c == 0) & (b == 0))
    def _build_maps():
        ab_re, ab_im, f_re, f_im = _zoh(are_ref, aim_ref, ldt_ref)
        bre = bre_ref[0, 0]
        bim = bim_ref[0, 0]
        bb_re = f_re * bre - f_im * bim
        bb_im = f_re * bim + f_im * bre
        pows = _complex_powers(ab_re, ab_im, nb)
        ctre = ctre_ref[0, 0]
        ctim = ctim_ref[0, 0]
        ct_stack = jnp.concatenate([ctre, -ctim], axis=1).astype(BF16)
        chan = [slice(h * LANES, (h + 1) * LANES) for h in range(nh)]
        stat = [slice(h * hs, (h + 1) * hs) for h in range(nh)]
        tok = [slice(t * LANES, (t + 1) * LANES) for t in range(nb)]
        taps = []
        for e in range(nb):
            pr, pi = pows[e]
            xr = pr * bb_re - pi * bb_im
            xi = pr * bb_im + pi * bb_re
            i = e if reverse else nb - 1 - e
            for h in range(nh):
                wx_ref[h, tok[i], :hs] = xr[chan[h], stat[h]].astype(BF16)
                wx_ref[h, tok[i], hs:] = xi[chan[h], stat[h]].astype(BF16)
            xk = jnp.concatenate([xr, xi], axis=1).astype(BF16)
            taps.append(lax.dot_general(xk, ct_stack, (((1,), (1,)), ((), ())),
                                        preferred_element_type=F32).astype(BF16))
        zero_blk = jnp.zeros((LANES, LANES), BF16)
        for i in range(nb):
            for j in range(nb):
                lag = (i - j) if reverse else (j - i)
                for h in range(nh):
                    wt_ref[h, tok[i], tok[j]] = taps[lag][chan[h], chan[h]] if lag >= 0 else zero_blk
        for j in range(nb):
            pr, pi = pows[nb - j] if reverse else pows[j + 1]
            c_re = ctre * pr - ctim * pi
            c_im = -(ctre * pi + ctim * pr)
            for h in range(nh):
                wc_ref[h, :hs, tok[j]] = c_re[chan[h], stat[h]].T.astype(BF16)
                wc_ref[h, hs:, tok[j]] = c_im[chan[h], stat[h]].T.astype(BF16)
        _fill_scan_tables(pw_ref, pows[nb][0], pows[nb][1], reverse)

    @pl.when(c == 0)
    def _load_state():
        carry_ref[b] = s0_ref[0]

    u8 = u8_ref[0].astype(BF16)
    for h in range(nh):
        x = jnp.dot(u8[:, h * hw:(h + 1) * hw], wx_ref[h], preferred_element_type=F32)
        s_ref[:, h * hs:(h + 1) * hs] = x[:, :hs]
        s_ref[:, ns + h * hs:ns + (h + 1) * hs] = x[:, hs:]
    carry = (carry_ref[b, 0], carry_ref[b, 1])
    pairs = nb // 2
    per = (S5_CHUNK_ROWS // SUBLANES) // (nh * pairs)
    for h in range(nh):
        for jp in range(pairs):
            j0 = 2 * jp
            rows = slice(j0 * LANES, hw) if reverse else slice(0, (j0 + 2) * LANES)
            cols = slice(j0 * LANES, (j0 + 2) * LANES)
            o_ref[0, :, h * hw + cols.start:h * hw + cols.stop] = jnp.dot(
                u8[:, h * hw + rows.start:h * hw + rows.stop], wt_ref[h, rows, cols],
                preferred_element_type=F32)
            k = h * pairs + jp
            carry = _scan_rows(s_ref, pw_ref, carry, S5_CHUNK_ROWS, reverse, exclusive=True,
                               groups=range(k * per, (k + 1) * per))
    carry_ref[b, 0] = carry[0]
    carry_ref[b, 1] = carry[1]
    for h in range(nh):
        y = (jnp.dot(s_ref[:, h * hs:(h + 1) * hs].astype(BF16), wc_ref[h, :hs],
                     preferred_element_type=F32)
             + jnp.dot(s_ref[:, ns + h * hs:ns + (h + 1) * hs].astype(BF16), wc_ref[h, hs:],
                       preferred_element_type=F32))
        o_ref[0, :, h * hw:(h + 1) * hw] = o_ref[0, :, h * hw:(h + 1) * hw] + y


def _s5_direction(layer, direction, u, u8, a_re, a_im, ldt, b_re, b_im, ct_re, ct_im, c_stack):
    b, r, w = u.shape
    nt = r // TILE
    rows = r // S5_BLOCK
    wide = S5_BLOCK * w
    tile_rows = TILE // S5_BLOCK
    ns = SSM_STATE
    reverse = direction == 1

    def dir_spec(shape):
        nd = len(shape)
        return pl.BlockSpec((1, 1) + tuple(shape), lambda *_: (layer, direction) + (0,) * nd,
                            pipeline_mode=pl.Buffered(1))

    y_ctx, state = pl.pallas_call(
        functools.partial(_s5_ctx_kernel, reverse=reverse),
        grid=(b,),
        in_specs=[pl.BlockSpec((1, TILE, w), lambda i: (i, nt - 1, 0)),
                  dir_spec((1, ns)), dir_spec((1, ns)), dir_spec((1, ns)),
                  dir_spec((w, ns)), dir_spec((w, ns)), dir_spec((2 * ns, w))],
        out_specs=[pl.BlockSpec((1, tile_rows, wide), lambda i: (i, 0, 0)),
                   pl.BlockSpec((1, 2, SUBLANES, ns), lambda i: (i, 0, 0, 0))],
        out_shape=[jax.ShapeDtypeStruct((b, tile_rows, wide), F32),
                   jax.ShapeDtypeStruct((b, 2, SUBLANES, ns), F32)],
        scratch_shapes=[pltpu.VMEM((w, 2 * ns), BF16),
                        pltpu.VMEM((8, SUBLANES, ns), F32),
                        pltpu.VMEM((TILE, 2 * ns), F32),
                        pltpu.VMEM((w // LANES, TILE, LANES), F32)],
        compiler_params=_params(("parallel",)),
        name="s5_context",
    )(u, a_re, a_im, ldt, b_re, b_im, c_stack)

    nchunks = (r - TILE) // (S5_BLOCK * S5_CHUNK_ROWS)

    def chunk(c, i):
        return (i, nchunks - 1 - c if reverse else c, 0)

    y_lat = pl.pallas_call(
        functools.partial(_s5_block_kernel, reverse=reverse),
        grid=(nchunks, b),
        in_specs=[pl.BlockSpec((1, S5_CHUNK_ROWS, wide), chunk),
                  pl.BlockSpec((1, 2, SUBLANES, ns), lambda c, i: (i, 0, 0, 0)),
                  dir_spec((1, ns)), dir_spec((1, ns)), dir_spec((1, ns)),
                  dir_spec((w, ns)), dir_spec((w, ns)), dir_spec((w, ns)), dir_spec((w, ns))],
        out_specs=pl.BlockSpec((1, S5_CHUNK_ROWS, wide), chunk),
        out_shape=jax.ShapeDtypeStruct((b, rows - tile_rows, wide), F32),
        scratch_shapes=[pltpu.VMEM((w // LANES, wide // (w // LANES), 2 * ns // (w // LANES)), BF16),
                        pltpu.VMEM((w // LANES, 2 * ns // (w // LANES), wide // (w // LANES)), BF16),
                        pltpu.VMEM((w // LANES, wide // (w // LANES), wide // (w // LANES)), BF16),
                        pltpu.VMEM((8, SUBLANES, ns), F32),
                        pltpu.VMEM((b, 2, SUBLANES, ns), F32),
                        pltpu.VMEM((S5_CHUNK_ROWS, 2 * ns), F32)],
        compiler_params=_params(("arbitrary", "arbitrary"), vmem=S5_VMEM_LIMIT),
        name="s5_blocks_reverse" if reverse else "s5_blocks_forward",
    )(u8, state, a_re, a_im, ldt, b_re, b_im, ct_re, ct_im)
    return y_lat, y_ctx


def _fourier_outer_kernel(w_ref, x_ref, g_ref):
    l2n, rows, w = x_ref.shape[1:]
    x = x_ref[0].reshape(l2n, rows * w).astype(BF16)
    g = jnp.dot(w_ref[...], x, preferred_element_type=F32)
    g_ref[0] = g.reshape(g_ref.shape[1:]).astype(BF16)


def _fourier_outer(f, w_outer):
    b, nt, _, w = f.shape
    l2n = nt - 1
    flat = TILE * w
    return pl.pallas_call(
        _fourier_outer_kernel,
        grid=(b, flat // FOURIER_CHUNK),
        in_specs=[_const_spec((2 * l2n, l2n)),
                  pl.BlockSpec((1, l2n, FOURIER_CHUNK // w, w), lambda i, j: (i, 0, j, 0))],
        out_specs=pl.BlockSpec((1, 2 * l2n, FOURIER_CHUNK // w, w), lambda i, j: (i, 0, j, 0)),
        out_shape=jax.ShapeDtypeStruct((b, 2 * l2n, TILE, w), BF16),
        compiler_params=_params(("parallel", "parallel")),
        name="fourier_outer",
    )(w_outer, f)


def _fourier_tail(y_re, y_im, c64_ref, wf_ref):
    w = FNET_WIDTH
    z = (jnp.dot(y_re.astype(BF16), c64_ref[:w], preferred_element_type=F32)
         + jnp.dot(y_im.astype(BF16), c64_ref[w:], preferred_element_type=F32))
    return jnp.dot(z.astype(BF16), wf_ref[0], preferred_element_type=F32)


def _fourier_inner_kernel(gr_ref, gi_ref, m_ref, c64_ref, wf_ref, o_ref):
    w = FNET_WIDTH
    ys = [jnp.dot(m_ref[i, :, :TILE], gr_ref[0, i], preferred_element_type=F32)
          + jnp.dot(m_ref[i, :, TILE:], gi_ref[0, i], preferred_element_type=F32)
          for i in range(FOURIER_GROUP)]
    for i, y in enumerate(ys):
        o_ref[0, :, i * w:(i + 1) * w] = _fourier_tail(y[:TILE], y[TILE:], c64_ref, wf_ref)


def _fourier_inner(layer, g, m_tab, c64_tab, wf_blk):
    b, two_l2n, _, w = g.shape
    l2n = two_l2n // 2
    steps = l2n // FOURIER_GROUP
    g4 = g
    return pl.pallas_call(
        _fourier_inner_kernel,
        grid=(b, steps),
        in_specs=[
            pl.BlockSpec((1, FOURIER_GROUP, TILE, w), lambda i, j: (i, j, 0, 0)),
            pl.BlockSpec((1, FOURIER_GROUP, TILE, w), lambda i, j: (i, steps + j, 0, 0)),
            pl.BlockSpec((FOURIER_GROUP, 2 * TILE, 2 * TILE), lambda i, j: (j, 0, 0)),
            _const_spec((2 * w, w)),
            _const_spec((w, w), layer),
        ],
        out_specs=pl.BlockSpec((1, TILE, FOURIER_GROUP * w), lambda i, j: (i, 0, j)),
        out_shape=jax.ShapeDtypeStruct((b, TILE, l2n * w), F32),
        compiler_params=_params(("parallel", "parallel")),
        name="fourier_inner",
    )(g4, g4, m_tab, c64_tab, wf_blk)


def _fourier_ctx_kernel(x_ref, m_ref, c64_ref, wf_ref, o_ref, z_ref, *, l2n):
    y = jnp.dot(m_ref[...], x_ref[0, 0].astype(BF16), preferred_element_type=F32)
    z = _fourier_tail(y[:TILE], y[TILE:], c64_ref, wf_ref)
    _rows_to_blocked(z, z_ref, o_ref.at[0], l2n)


def _fourier_context(layer, f, m_ctx, c64_tab, wf_blk):
    b, nt, _, w = f.shape
    l2n = nt - 1
    return pl.pallas_call(
        functools.partial(_fourier_ctx_kernel, l2n=l2n),
        grid=(b,),
        in_specs=[
            pl.BlockSpec((1, 1, TILE, w), lambda i: (i, nt - 1, 0, 0)),
            _const_spec((2 * TILE, TILE)),
            _const_spec((2 * w, w)),
            _const_spec((w, w), layer),
        ],
        out_specs=pl.BlockSpec((1, TILE // l2n, l2n * w), lambda i: (i, 0, 0)),
        out_shape=jax.ShapeDtypeStruct((b, TILE // l2n, l2n * w), F32),
        scratch_shapes=[pltpu.VMEM((w // LANES, TILE, LANES), F32)],
        compiler_params=_params(("parallel",)),
        name="fourier_context",
    )(f, m_ctx, c64_tab, wf_blk)


def _outffn_kernel(x_ref, xc_ref, a_ref, yf_ref, yb_ref, yfc_ref, ybc_ref, u_ref, fn_ref, fnc_ref, mod_ref,
                   dsk_ref, wglu_ref, bglu_ref, wo_ref, g2_ref, w1_ref, w2_ref, o_ref, ytok_ref,
                   ftok_ref, *, l2n, nt):
    d = D_MODEL
    bsz = x_ref.shape[0]
    o1 = ATTN_WIDTH
    o2 = ATTN_WIDTH + SSM_WIDTH
    is_ctx = pl.program_id(0) == nt - 1

    def mixing_head(i):
        mod = mod_ref[0, pl.ds(jnp.where(is_ctx, bsz, i), 1), :]
        y_blk = jnp.where(is_ctx, yfc_ref[i] + ybc_ref[i], yf_ref[i] + yb_ref[i])
        y_scan = _blocked_to_rows(y_blk, ytok_ref.at[i], S5_BLOCK, tile_major=True)
        fnet = _blocked_to_rows(jnp.where(is_ctx, fnc_ref[i], fn_ref[i]), ftok_ref.at[i], l2n)
        hg = jax.nn.gelu(y_scan + dsk_ref[0] * u_ref[i])
        zg = jnp.dot(hg.astype(BF16), wglu_ref[0], preferred_element_type=F32) + bglu_ref[0]
        s5 = (hg * jax.nn.sigmoid(zg)).astype(BF16)
        mix = (jnp.dot(a_ref[i], wo_ref[0, :o1], preferred_element_type=F32)
               + jnp.dot(s5, wo_ref[0, o1:o2], preferred_element_type=F32)
               + jnp.dot(fnet.astype(BF16), wo_ref[0, o2:], preferred_element_type=F32))
        x1 = jnp.where(is_ctx, xc_ref[i], x_ref[i]) + mod[:, 2 * d:3 * d] * mix
        ms = jnp.mean(x1 * x1, axis=-1, keepdims=True)
        hn = x1 * lax.rsqrt(ms + EPS) * g2_ref[0]
        h = (hn * (1.0 + mod[:, 4 * d:5 * d]) + mod[:, 3 * d:4 * d]).astype(BF16)
        return x1, h, mod[:, 5 * d:]

    def mlp(i, x1, h, gate):
        acc = jnp.zeros((TILE, d), F32)
        for cc in range(D_FF // FF_CHUNK):
            sl = slice(cc * FF_CHUNK, (cc + 1) * FF_CHUNK)
            t = jnp.dot(h, w1_ref[0, :, sl], preferred_element_type=F32)
            t = jnp.square(jnp.maximum(t, 0.0))
            acc = acc + jnp.dot(t.astype(BF16), w2_ref[0, sl, :], preferred_element_type=F32)
        o_ref[i] = x1 + gate * acc

    heads = [mixing_head(i) for i in range(bsz)]
    for i, (x1, h, gate) in enumerate(heads):
        mlp(i, x1, h, gate)


def _out_ffn(layer, x_lat, x_ctx, r, attn, y_fwd, y_bwd, u, fn, fn_ctx, mods, d_skip, w_glu, b_glu,
             w_out, g_norm2, w_ff1, w_ff2, latent_only):
    b, _, d = x_lat.shape
    nt = r // TILE
    l2n = nt - 1
    depth = w_out.shape[0]
    s5_shape = (b, TILE // S5_BLOCK, S5_BLOCK * SSM_WIDTH)
    fn_shape = (b, TILE // l2n, l2n * FNET_WIDTH)

    def latent(shape):
        return pl.BlockSpec(shape, lambda t: (0, jnp.minimum(t, nt - 2), 0))

    def context(shape):
        return pl.BlockSpec(shape, lambda t: (0, 0, 0))

    def tile(width):
        return pl.BlockSpec((b, TILE, width), lambda t: (0, t, 0))

    return pl.pallas_call(
        functools.partial(_outffn_kernel, l2n=l2n, nt=nt),
        grid=(nt - 1 if latent_only else nt,),
        in_specs=_stream_specs(x_lat, x_ctx, nt) + [
            tile(ATTN_WIDTH), latent(s5_shape), latent(s5_shape), context(s5_shape),
            context(s5_shape), tile(SSM_WIDTH), latent(fn_shape), context(fn_shape),
            _const_spec((SUBLANES, 6 * d), layer),
            _const_spec((1, SSM_WIDTH), layer),
            _const_spec((SSM_WIDTH, SSM_WIDTH), layer),
            _const_spec((1, SSM_WIDTH), layer),
            _const_spec((d, d), layer),
            _const_spec((1, d), layer),
            _const_spec((d, D_FF), layer),
            _const_spec((D_FF, d), layer),
        ],
        out_specs=tile(d),
        out_shape=jax.ShapeDtypeStruct((b, r - TILE if latent_only else r, d), F32),
        scratch_shapes=[pltpu.VMEM((b, SSM_WIDTH // LANES, TILE, LANES), F32),
                        pltpu.VMEM((b, FNET_WIDTH // LANES, TILE, LANES), F32)],
        input_output_aliases={0: 0} if x_lat.shape[1] == r and not latent_only else {},
        compiler_params=_params(("parallel",)),
        name="out_ffn",
    )(x_lat, x_ctx, attn, y_fwd[0], y_bwd[0], y_fwd[1], y_bwd[1], u, fn, fn_ctx, mods, d_skip, w_glu, b_glu,
      w_out, g_norm2.reshape(depth, 1, d), w_ff1, w_ff2)


def _head_lane_tables():
    j = np.arange(LANES)
    half = j // 64
    mp = (j // 32) % 2
    idx = j % 32
    src_in_head = mp * HEAD_DIM + half * 32 + idx
    gain_idx = half * 32 + idx
    return src_in_head, gain_idx, mp


def _in_proj_column_order():
    src_in_head, _, _ = _head_lane_tables()
    qk = np.concatenate([h * LANES + src_in_head for h in range(N_HEADS)])
    return np.concatenate([qk, QK_WIDTH + qk, np.arange(2 * QK_WIDTH, IN_WIDTH)])


def _same_map_matrix():
    i = np.arange(256)
    head = i // LANES
    mp = ((i % LANES) // 32) % 2
    same = (head[:, None] == head[None, :]) & (mp[:, None] == mp[None, :])
    return same.astype(np.float32)


def _rope_tables(seq, ctx_len):
    t = jnp.arange(seq)
    row = (t // GRID_W).astype(F32)
    col = (t % GRID_W).astype(F32)
    n_freq = HEAD_DIM // 4
    inv = jnp.power(ROPE_BASE, -jnp.arange(n_freq, dtype=F32) / n_freq)
    ang = jnp.concatenate([row[:, None] * inv, col[:, None] * inv], axis=-1)
    cos = jnp.tile(jnp.cos(ang), (1, 4))
    sign = np.where(np.arange(LANES) < LANES // 2, -1.0, 1.0).astype(np.float32)
    sin = jnp.tile(jnp.sin(ang), (1, 4)) * sign
    cos = jnp.concatenate([cos, jnp.ones((ctx_len, LANES), F32)], axis=0)
    sin = jnp.concatenate([sin, jnp.zeros((ctx_len, LANES), F32)], axis=0)
    return cos, sin


def _dft_tables(length):
    l2n = length // TILE
    k2 = np.arange(l2n)
    phi = 2.0 * np.pi * ((k2[:, None] * k2[None, :]) % l2n) / l2n
    w_outer = np.concatenate([np.cos(phi), -np.sin(phi)], axis=0).astype(np.float32)
    l1 = np.arange(TILE)
    alpha = 2.0 * np.pi * ((k2[:, None] * l1[None, :]) % length) / length
    beta = 2.0 * np.pi * ((l1[:, None] * l1[None, :]) % TILE) / TILE
    ca = jnp.asarray(np.cos(alpha).astype(np.float32))[:, None, :]
    sa = jnp.asarray(np.sin(alpha).astype(np.float32))[:, None, :]
    cb = jnp.asarray(np.cos(beta).astype(np.float32))[None]
    sb = jnp.asarray(np.sin(beta).astype(np.float32))[None]
    norm = 1.0 / math.sqrt(length)
    ct = (ca * cb - sa * sb) * norm
    st = (sa * cb + ca * sb) * norm
    m = jnp.concatenate([jnp.concatenate([ct, st], axis=2),
                         jnp.concatenate([-st, ct], axis=2)], axis=1).astype(BF16)
    return jnp.asarray(w_outer).astype(BF16), m


def _tile_dft_table():
    l1 = np.arange(TILE)
    beta = 2.0 * np.pi * ((l1[:, None] * l1[None, :]) % TILE) / TILE
    tab = np.concatenate([np.cos(beta), -np.sin(beta)], axis=0) / math.sqrt(TILE)
    return jnp.asarray(tab.astype(np.float32)).astype(BF16)


def _channel_dft_table():
    c = np.arange(FNET_C)
    th = 2.0 * np.pi * ((c[:, None] * c[None, :]) % FNET_C) / FNET_C
    eye = np.eye(FNET_G)
    norm = 1.0 / math.sqrt(FNET_C)
    cblk = np.kron(eye, np.cos(th)) * norm
    sblk = np.kron(eye, np.sin(th)) * norm
    return jnp.asarray(np.concatenate([cblk, sblk], axis=0).astype(np.float32)).astype(BF16)


def _block_diag(blocks):
    g = blocks.shape[-3]
    eye = jnp.eye(g, dtype=blocks.dtype)
    out = jnp.einsum('...gab,gh->...gahb', blocks, eye)
    return out.reshape(blocks.shape[:-3] + (g * blocks.shape[-2], g * blocks.shape[-1]))


def kernel(x, c, ctx, c_ctx, w_mod, b_mod, g_norm1, w_in, g_qnorm, g_knorm, lam_q1, lam_k1, lam_q2, lam_k2, g_subln, ssm_a_re, ssm_a_im, ssm_log_dt, ssm_b_re, ssm_b_im, ssm_c_re, ssm_c_im, ssm_d, w_glu, b_glu, w_fnet, w_out, g_norm2, w_ff1, w_ff2):
    bsz, seq, d = x.shape
    ctx_len = ctx.shape[1]
    depth = w_mod.shape[0]
    assert d == D_MODEL and ctx_len == TILE and seq % TILE == 0 and seq % GRID_W == 0
    assert bsz + 1 <= SUBLANES

    assert (seq // TILE) % FOURIER_GROUP == 0 and seq % (S5_BLOCK * S5_CHUNK_ROWS) == 0
    r_tot = seq + ctx_len
    x_lat, x_ctx = x, ctx
    act = jnp.concatenate([c, c_ctx[None], jnp.zeros((SUBLANES - bsz - 1, d), F32)], axis=0)
    mods = _modulation(act, w_mod, b_mod)

    w_in_p = jnp.take(w_in, jnp.asarray(_in_proj_column_order()), axis=2).astype(BF16)
    _, gain_idx, _ = _head_lane_tables()
    gq = jnp.tile(g_qnorm[:, gain_idx], (1, N_HEADS)).reshape(depth, 1, QK_WIDTH)
    gk = jnp.tile(g_knorm[:, gain_idx], (1, N_HEADS)).reshape(depth, 1, QK_WIDTH)
    e_mat = jnp.asarray(_same_map_matrix()).astype(BF16)
    cos_t, sin_t = _rope_tables(seq, ctx_len)
    lam_init = np.array([0.8 - 0.6 * math.exp(-0.3 * i) for i in range(depth)], np.float32)
    pad = jnp.zeros((depth, LANES - HEAD_DIM), F32)
    lam_rows = [jnp.concatenate([v, pad], axis=1) for v in (lam_q1, lam_k1, lam_q2, lam_k2)]
    const_row = np.zeros((depth, LANES), np.float32)
    const_row[:, 0] = lam_init
    const_row[:, 1] = 1.0 - lam_init
    lam_p = jnp.stack(lam_rows + [jnp.asarray(const_row)]
                      + [jnp.zeros((depth, LANES), F32)] * (SUBLANES - 5), axis=1)
    g_sub = g_subln.reshape(depth, 1, V_DIM)

    a_re = ssm_a_re.reshape(depth, 2, 1, SSM_STATE)
    a_im = ssm_a_im.reshape(depth, 2, 1, SSM_STATE)
    ldt = jnp.repeat(ssm_log_dt, SSM_N, axis=-1).reshape(depth, 2, 1, SSM_STATE)
    b_re = _block_diag(jnp.swapaxes(ssm_b_re, -1, -2))
    b_im = _block_diag(jnp.swapaxes(ssm_b_im, -1, -2))
    c_stack = jnp.concatenate([_block_diag(jnp.swapaxes(ssm_c_re, -1, -2)),
                               -_block_diag(jnp.swapaxes(ssm_c_im, -1, -2))],
                              axis=2).astype(BF16)
    ct_re = _block_diag(ssm_c_re)
    ct_im = _block_diag(ssm_c_im)
    d_skip = ssm_d.reshape(depth, 1, SSM_WIDTH)
    w_glu_b = w_glu.astype(BF16)
    b_glu_r = b_glu.reshape(depth, 1, SSM_WIDTH)

    w_outer, m_lat = _dft_tables(seq)
    m_ctx = _tile_dft_table()
    c64_tab = _channel_dft_table()
    wf_blk = _block_diag(w_fnet).astype(BF16)

    w_out_b = w_out.astype(BF16)
    w_ff1_b = w_ff1.astype(BF16)
    w_ff2_b = w_ff2.astype(BF16)

    for layer in range(depth):
        q, kt, ve, u, u8, f = _in_projection(layer, x_lat, x_ctx, r_tot, mods, g_norm1, w_in_p, gq,
                                             gk, e_mat, cos_t, sin_t)
        attn = _attention(layer, q, kt, ve, lam_p, g_sub)
        y_fwd, y_bwd = [_s5_direction(layer, dr, u, u8, a_re, a_im, ldt, b_re, b_im, ct_re,
                                      ct_im, c_stack) for dr in range(2)]
        fn = _fourier_inner(layer, _fourier_outer(f, w_outer), m_lat, c64_tab, wf_blk)
        fn_ctx = _fourier_context(layer, f, m_ctx, c64_tab, wf_blk)
        xs = _out_ffn(layer, x_lat, x_ctx, r_tot, attn, y_fwd, y_bwd, u, fn, fn_ctx, mods, d_skip,
                      w_glu_b, b_glu_r, w_out_b, g_norm2, w_ff1_b, w_ff2_b,
                      latent_only=layer == depth - 1)
        x_lat = x_ctx = xs
    return xs
```

```python
import functools
import math

import numpy as np
import jax
import jax.numpy as jnp
from jax import lax
from jax.experimental import pallas as pl
from jax.experimental.pallas import tpu as pltpu

F32 = jnp.float32
BF16 = jnp.bfloat16

D_MODEL = 1024
GRID_W = 64
N_HEADS = 4
HEAD_DIM = 64
V_DIM = 2 * HEAD_DIM
QK_WIDTH = N_HEADS * 2 * HEAD_DIM
ATTN_WIDTH = N_HEADS * V_DIM
SSM_WIDTH = D_MODEL // 4
SSM_P = 16
SSM_G = SSM_WIDTH // SSM_P
SSM_N = 64
SSM_STATE = SSM_G * SSM_N
FNET_WIDTH = D_MODEL // 4
FNET_G = 4
FNET_C = FNET_WIDTH // FNET_G
IN_WIDTH = 2 * QK_WIDTH + ATTN_WIDTH + SSM_WIDTH + FNET_WIDTH
D_FF = 4 * D_MODEL
ROPE_BASE = 10000.0
EPS = 1e-6
SCALE = HEAD_DIM ** -0.5
LOG2E = math.log2(math.e)

TILE = 256
LANES = 128
SUBLANES = 8
FF_CHUNK = 1024
MOD_BLOCK = 1536
FOURIER_CHUNK = 8192
FOURIER_GROUP = 4
S5_BLOCK = 8
S5_CHUNK_ROWS = 256
VMEM_LIMIT = 48 * 1024 * 1024
S5_VMEM_LIMIT = 56 * 1024 * 1024

def _const_spec(shape, layer=None):
    nd = len(shape)
    if layer is None:
        return pl.BlockSpec(shape, lambda *_: (0,) * nd, pipeline_mode=pl.Buffered(1))
    return pl.BlockSpec((1,) + tuple(shape), lambda *_: (layer,) + (0,) * nd,
                        pipeline_mode=pl.Buffered(1))


def _params(sem, vmem=None):
    return pltpu.CompilerParams(dimension_semantics=sem, vmem_limit_bytes=vmem or VMEM_LIMIT)


def _blocked_lane(j, h, period, nh, tile_major):
    return ((h * period + j) if tile_major else (j * nh + h)) * LANES


def _rows_to_blocked(tok, scr_ref, blk_ref, period, tile_major=False):
    n, w = tok.shape
    nh = w // LANES
    for h in range(nh):
        scr_ref[h] = tok[:, h * LANES:(h + 1) * LANES]
    for j in range(period):
        for h in range(nh):
            lo = _blocked_lane(j, h, period, nh, tile_major)
            blk_ref[:, lo:lo + LANES] = scr_ref[h, pl.ds(j, n // period, stride=period), :]


def _blocked_to_rows(blk, scr_ref, period, tile_major=False):
    nh, n, _ = scr_ref.shape
    for j in range(period):
        for h in range(nh):
            lo = _blocked_lane(j, h, period, nh, tile_major)
            scr_ref[h, pl.ds(j, n // period, stride=period), :] = blk[:, lo:lo + LANES]
    return jnp.concatenate([scr_ref[h] for h in range(nh)], axis=1)


def _mod_kernel(act_ref, w_ref, b_ref, o_ref):
    a = act_ref[...]
    a = a * jax.nn.sigmoid(a)
    o_ref[0] = jnp.dot(a.astype(BF16), w_ref[0].astype(BF16),
                       preferred_element_type=F32) + b_ref[0]


def _modulation(act, w_mod, b_mod):
    depth, d, n = w_mod.shape
    bn = MOD_BLOCK
    return pl.pallas_call(
        _mod_kernel,
        grid=(depth, n // bn),
        in_specs=[pl.BlockSpec((SUBLANES, d), lambda l, j: (0, 0)),
                  pl.BlockSpec((1, d, bn), lambda l, j: (l, 0, j)),
                  pl.BlockSpec((1, 1, bn), lambda l, j: (l, 0, j))],
        out_specs=pl.BlockSpec((1, SUBLANES, bn), lambda l, j: (l, 0, j)),
        out_shape=jax.ShapeDtypeStruct((depth, SUBLANES, n), F32),
        compiler_params=_params(("parallel", "parallel")),
        name="modulation",
    )(act, w_mod, b_mod.reshape(depth, 1, n))


def _inproj_kernel(x_ref, mod_ref, g1_ref, w_ref, gq_ref, gk_ref, e_ref, cos_ref, sin_ref,
                   q_ref, kt_ref, ve_ref, u_ref, u8_ref, f_ref, stage_ref, *, nt):
    d = D_MODEL
    bsz = x_ref.shape[0]
    is_ctx = pl.program_id(0) == nt - 1
    e = e_ref[...]
    cos = cos_ref[...]
    sin = sin_ref[...]

    def project(i):
        x = x_ref[i]
        ms = jnp.mean(x * x, axis=-1, keepdims=True)
        xn = x * lax.rsqrt(ms + EPS) * g1_ref[0]
        mod = mod_ref[0, pl.ds(jnp.where(is_ctx, bsz, i), 1), :]
        h = xn * (1.0 + mod[:, d:2 * d]) + mod[:, :d]
        return jnp.dot(h.astype(BF16), w_ref[0], preferred_element_type=F32)

    def qk_norm(z, g):
        sq = z * z
        hi = sq.astype(BF16)
        lo = (sq - hi.astype(F32)).astype(BF16)
        parts = []
        for j in range(QK_WIDTH // 256):
            sl = slice(j * 256, (j + 1) * 256)
            parts.append(jnp.dot(hi[:, sl], e, preferred_element_type=F32)
                         + jnp.dot(lo[:, sl], e, preferred_element_type=F32))
        ssum = jnp.concatenate(parts, axis=-1)
        return z * lax.rsqrt(ssum * (1.0 / HEAD_DIM) + EPS) * g

    def rope(zh):
        return zh * cos + pltpu.roll(zh, LANES // 2, 1) * sin

    row_map = (lax.broadcasted_iota(jnp.int32, (LANES, TILE), 0) // 32) % 2
    ones = jnp.ones((TILE, LANES), BF16)

    def finish(i, proj):
        qn = qk_norm(proj[:, :QK_WIDTH], gq_ref[0])
        kn = qk_norm(proj[:, QK_WIDTH:2 * QK_WIDTH], gk_ref[0])
        for hh in range(N_HEADS):
            sl = slice(hh * LANES, (hh + 1) * LANES)
            q_ref[i, :, sl] = (rope(qn[:, sl]) * (SCALE * LOG2E)).astype(BF16)
            kt = rope(kn[:, sl]).T
            kt_ref[i, hh, 0] = jnp.where(row_map == 0, kt, 0.0).astype(BF16)
            kt_ref[i, hh, 1] = jnp.where(row_map == 1, kt, 0.0).astype(BF16)
            vo = 2 * QK_WIDTH + hh * V_DIM
            ve_ref[i, hh, :, :V_DIM] = proj[:, vo:vo + V_DIM].astype(BF16)
            ve_ref[i, hh, :, V_DIM:] = ones
        uo = 2 * QK_WIDTH + ATTN_WIDTH
        u_ref[i] = proj[:, uo:uo + SSM_WIDTH]
        _rows_to_blocked(proj[:, uo:uo + SSM_WIDTH], stage_ref, u8_ref.at[i], S5_BLOCK,
                         tile_major=True)
        f_ref[i] = proj[:, uo + SSM_WIDTH:]

    projs = [project(i) for i in range(bsz)]
    for i, proj in enumerate(projs):
        finish(i, proj)


def _in_projection(layer, xs, mods, g_norm1, w_in, gq, gk, e_mat, cos_t, sin_t):
    b, r, d = xs.shape
    nt = r // TILE
    depth = w_in.shape[0]
    return pl.pallas_call(
        functools.partial(_inproj_kernel, nt=nt),
        grid=(nt,),
        in_specs=[
            pl.BlockSpec((b, TILE, d), lambda t: (0, t, 0)),
            _const_spec((SUBLANES, 6 * d), layer),
            _const_spec((1, d), layer),
            _const_spec((d, IN_WIDTH), layer),
            _const_spec((1, QK_WIDTH), layer),
            _const_spec((1, QK_WIDTH), layer),
            _const_spec((256, 256)),
            pl.BlockSpec((TILE, LANES), lambda t: (t, 0)),
            pl.BlockSpec((TILE, LANES), lambda t: (t, 0)),
        ],
        out_specs=[
            pl.BlockSpec((b, TILE, QK_WIDTH), lambda t: (0, t, 0)),
            pl.BlockSpec((b, N_HEADS, 2, LANES, TILE), lambda t: (0, 0, 0, 0, t)),
            pl.BlockSpec((b, N_HEADS, TILE, 2 * V_DIM), lambda t: (0, 0, t, 0)),
            pl.BlockSpec((b, TILE, SSM_WIDTH), lambda t: (0, t, 0)),
            pl.BlockSpec((b, TILE // S5_BLOCK, S5_BLOCK * SSM_WIDTH), lambda t: (0, t, 0)),
            pl.BlockSpec((b, TILE, FNET_WIDTH), lambda t: (0, t, 0)),
        ],
        out_shape=[
            jax.ShapeDtypeStruct((b, r, QK_WIDTH), BF16),
            jax.ShapeDtypeStruct((b, N_HEADS, 2, LANES, r), BF16),
            jax.ShapeDtypeStruct((b, N_HEADS, r, 2 * V_DIM), BF16),
            jax.ShapeDtypeStruct((b, r, SSM_WIDTH), F32),
            jax.ShapeDtypeStruct((b, r // S5_BLOCK, S5_BLOCK * SSM_WIDTH), F32),
            jax.ShapeDtypeStruct((b, r, FNET_WIDTH), F32),
        ],
        scratch_shapes=[pltpu.VMEM((SSM_WIDTH // LANES, TILE, LANES), F32)],
        compiler_params=_params(("parallel",)),
        name="in_projection",
    )(xs, mods, g_norm1.reshape(depth, 1, d), w_in, gq, gk, e_mat, cos_t, sin_t)


def _attn_kernel(q_ref, kt_ref, ve_ref, lam_ref, gs_ref, o_ref, *, nt, ctx_len):
    t = pl.program_id(2)
    q = q_ref[0]
    r = ve_ref.shape[2]

    def attend(k0):
        outs = []
        ve = ve_ref[0, 0, k0:, :]
        scores = [jnp.dot(q, kt_ref[0, 0, mp, :, k0:], preferred_element_type=F32)
                  for mp in range(2)]
        for s in scores:
            p = jnp.exp2(s - jnp.max(s, axis=1, keepdims=True))
            acc = jnp.dot(p.astype(BF16), ve, preferred_element_type=F32)
            outs.append(acc[:, :V_DIM] / acc[:, V_DIM:])
        lp = lam_ref[0]
        s1 = jnp.sum(lp[0:1] * lp[1:2], axis=-1, keepdims=True)
        s2 = jnp.sum(lp[2:3] * lp[3:4], axis=-1, keepdims=True)
        lam = jnp.exp(s1) - jnp.exp(s2) + lp[4:5, 0:1]
        a = outs[0] - lam * outs[1]
        a = a * lax.rsqrt(jnp.mean(a * a, axis=-1, keepdims=True) + EPS)
        o_ref[0] = (a * gs_ref[0] * lp[4:5, 1:2]).astype(BF16)

    @pl.when(t < nt - 1)
    def _latent_queries():
        attend(0)

    @pl.when(t == nt - 1)
    def _context_queries():
        attend(r - ctx_len)


def _attention(layer, q, kt, ve, lam_p, g_subln):
    b, r, _ = q.shape
    nt = r // TILE
    return pl.pallas_call(
        functools.partial(_attn_kernel, nt=nt, ctx_len=TILE),
        grid=(b, N_HEADS, nt),
        in_specs=[
            pl.BlockSpec((1, TILE, LANES), lambda i, h, t: (i, t, h)),
            pl.BlockSpec((1, 1, 2, LANES, r), lambda i, h, t: (i, h, 0, 0, 0),
                         pipeline_mode=pl.Buffered(1)),
            pl.BlockSpec((1, 1, r, 2 * V_DIM), lambda i, h, t: (i, h, 0, 0),
                         pipeline_mode=pl.Buffered(1)),
            _const_spec((SUBLANES, LANES), layer),
            _const_spec((1, V_DIM), layer),
        ],
        out_specs=pl.BlockSpec((1, TILE, V_DIM), lambda i, h, t: (i, t, h)),
        out_shape=jax.ShapeDtypeStruct((b, r, ATTN_WIDTH), BF16),
        compiler_params=_params(("parallel", "parallel", "arbitrary")),
        name="diff_attention",
    )(q, kt, ve, lam_p, g_subln)


def _zoh(are_ref, aim_ref, ldt_ref):
    a_re = are_ref[0, 0]
    a_im = aim_ref[0, 0]
    dt = jnp.exp(ldt_ref[0, 0])
    mag = jnp.exp(dt * a_re)
    ang = dt * a_im
    ab_re = mag * jnp.cos(ang)
    ab_im = mag * jnp.sin(ang)
    den = a_re * a_re + a_im * a_im
    n_re = ab_re - 1.0
    f_re = (n_re * a_re + ab_im * a_im) / den
    f_im = (ab_im * a_re - n_re * a_im) / den
    return ab_re, ab_im, f_re, f_im


def _complex_powers(base_re, base_im, n):
    pows = [(jnp.ones_like(base_re), jnp.zeros_like(base_im))]
    for _ in range(n):
        pr, pi = pows[-1]
        pows.append((pr * base_re - pi * base_im, pr * base_im + pi * base_re))
    return pows


def _fill_scan_tables(pw_ref, base_re, base_im, reverse):
    shape = (SUBLANES, SSM_STATE)
    pows = _complex_powers(base_re, base_im, SUBLANES)
    row = lax.broadcasted_iota(jnp.int32, shape, 0)
    zero = jnp.zeros(shape, F32)
    for idx, k in enumerate((1, 2, 4)):
        mask = (row + k <= SUBLANES - 1) if reverse else (row >= k)
        pw_ref[2 * idx] = jnp.where(mask, jnp.broadcast_to(pows[k][0], shape), zero)
        pw_ref[2 * idx + 1] = jnp.where(mask, jnp.broadcast_to(pows[k][1], shape), zero)
    pcr = zero
    pci = zero
    for tt in range(SUBLANES):
        e = (SUBLANES - tt) if reverse else tt + 1
        pcr = jnp.where(row == tt, jnp.broadcast_to(pows[e][0], shape), pcr)
        pci = jnp.where(row == tt, jnp.broadcast_to(pows[e][1], shape), pci)
    pw_ref[6] = pcr
    pw_ref[7] = pci


def _scan_rows(s_ref, pw_ref, carry, nrows, reverse, exclusive, groups=None):
    ns = SSM_STATE
    shape = (SUBLANES, ns)
    ngroups = nrows // SUBLANES
    edge = (SUBLANES - 1) if reverse else 0

    def group(g, carry):
        cr, ci = carry
        gi = (ngroups - 1 - g) if reverse else g
        r0 = pl.multiple_of(gi * SUBLANES, SUBLANES)
        xr = s_ref[pl.ds(r0, SUBLANES), :ns]
        xi = s_ref[pl.ds(r0, SUBLANES), ns:]
        for idx, k in enumerate((1, 2, 4)):
            sh = (SUBLANES - k) if reverse else k
            sr = pltpu.roll(xr, sh, 0)
            si = pltpu.roll(xi, sh, 0)
            pr = pw_ref[2 * idx]
            pi = pw_ref[2 * idx + 1]
            xr, xi = xr + pr * sr - pi * si, xi + pr * si + pi * sr
        pr = pw_ref[6]
        pi = pw_ref[7]
        xr, xi = xr + pr * cr - pi * ci, xi + pr * ci + pi * cr
        if exclusive:
            row = lax.broadcasted_iota(jnp.int32, shape, 0)
            sh = (SUBLANES - 1) if reverse else 1
            er = jnp.where(row == edge, cr, pltpu.roll(xr, sh, 0))
            ei = jnp.where(row == edge, ci, pltpu.roll(xi, sh, 0))
        else:
            er, ei = xr, xi
        s_ref[pl.ds(r0, SUBLANES), :ns] = er
        s_ref[pl.ds(r0, SUBLANES), ns:] = ei
        last = 0 if reverse else SUBLANES - 1
        return (jnp.broadcast_to(xr[last:last + 1], shape),
                jnp.broadcast_to(xi[last:last + 1], shape))

    if groups is None:
        return lax.fori_loop(0, ngroups, group, carry)
    for g in groups:
        carry = group(g, carry)
    return carry


def _s5_ctx_kernel(u_ref, are_ref, aim_ref, ldt_ref, bre_ref, bim_ref, c_ref,
                   o_ref, st_ref, bbar_ref, pw_ref, s_ref, y_ref, *, reverse):
    ns = SSM_STATE
    ab_re, ab_im, f_re, f_im = _zoh(are_ref, aim_ref, ldt_ref)
    bre = bre_ref[0, 0]
    bim = bim_ref[0, 0]
    bbar_ref[:, :ns] = (f_re * bre - f_im * bim).astype(BF16)
    bbar_ref[:, ns:] = (f_re * bim + f_im * bre).astype(BF16)
    _fill_scan_tables(pw_ref, ab_re, ab_im, reverse)
    s_ref[...] = jnp.dot(u_ref[0].astype(BF16), bbar_ref[...], preferred_element_type=F32)
    zero = jnp.zeros((SUBLANES, ns), F32)
    cr, ci = _scan_rows(s_ref, pw_ref, (zero, zero), TILE, reverse, exclusive=False)
    st_ref[0, 0] = cr
    st_ref[0, 1] = ci
    y = jnp.dot(s_ref[...].astype(BF16), c_ref[0, 0], preferred_element_type=F32)
    _rows_to_blocked(y, y_ref, o_ref.at[0], S5_BLOCK, tile_major=True)


def _s5_block_kernel(u8_ref, s0_ref, are_ref, aim_ref, ldt_ref, bre_ref, bim_ref, ctre_ref,
                     ctim_ref, o_ref, wx_ref, wc_ref, wt_ref, pw_ref, carry_ref, s_ref,
                     *, reverse):
    ns = SSM_STATE
    w = SSM_WIDTH
    nb = S5_BLOCK
    nh = w // LANES
    hs = ns // nh
    hw = nb * LANES
    c = pl.program_id(0)
    b = pl.program_id(1)

    @pl.when((c == 0) & (b == 0))
    def _build_maps():
        ab_re, ab_im, f_re, f_im = _zoh(are_ref, aim_ref, ldt_ref)
        bre = bre_ref[0, 0]
        bim = bim_ref[0, 0]
        bb_re = f_re * bre - f_im * bim
        bb_im = f_re * bim + f_im * bre
        pows = _complex_powers(ab_re, ab_im, nb)
        ctre = ctre_ref[0, 0]
        ctim = ctim_ref[0, 0]
        ct_stack = jnp.concatenate([ctre, -ctim], axis=1).astype(BF16)
        chan = [slice(h * LANES, (h + 1) * LANES) for h in range(nh)]
        stat = [slice(h * hs, (h + 1) * hs) for h in range(nh)]
        tok = [slice(t * LANES, (t + 1) * LANES) for t in range(nb)]
        taps = []
        for e in range(nb):
            pr, pi = pows[e]
            xr = pr * bb_re - pi * bb_im
            xi = pr * bb_im + pi * bb_re
            i = e if reverse else nb - 1 - e
            for h in range(nh):
                wx_ref[h, tok[i], :hs] = xr[chan[h], stat[h]].astype(BF16)
                wx_ref[h, tok[i], hs:] = xi[chan[h], stat[h]].astype(BF16)
            xk = jnp.concatenate([xr, xi], axis=1).astype(BF16)
            taps.append(lax.dot_general(xk, ct_stack, (((1,), (1,)), ((), ())),
                                        preferred_element_type=F32).astype(BF16))
        zero_blk = jnp.zeros((LANES, LANES), BF16)
        for i in range(nb):
            for j in range(nb):
                lag = (i - j) if reverse else (j - i)
                for h in range(nh):
                    wt_ref[h, tok[i], tok[j]] = taps[lag][chan[h], chan[h]] if lag >= 0 else zero_blk
        for j in range(nb):
            pr, pi = pows[nb - j] if reverse else pows[j + 1]
            c_re = ctre * pr - ctim * pi
            c_im = -(ctre * pi + ctim * pr)
            for h in range(nh):
                wc_ref[h, :hs, tok[j]] = c_re[chan[h], stat[h]].T.astype(BF16)
                wc_ref[h, hs:, tok[j]] = c_im[chan[h], stat[h]].T.astype(BF16)
        _fill_scan_tables(pw_ref, pows[nb][0], pows[nb][1], reverse)

    @pl.when(c == 0)
    def _load_state():
        carry_ref[b] = s0_ref[0]

    u8 = u8_ref[0].astype(BF16)
    for h in range(nh):
        x = jnp.dot(u8[:, h * hw:(h + 1) * hw], wx_ref[h], preferred_element_type=F32)
        s_ref[:, h * hs:(h + 1) * hs] = x[:, :hs]
        s_ref[:, ns + h * hs:ns + (h + 1) * hs] = x[:, hs:]
    carry = (carry_ref[b, 0], carry_ref[b, 1])
    pairs = nb // 2
    per = (S5_CHUNK_ROWS // SUBLANES) // (nh * pairs)
    for h in range(nh):
        for jp in range(pairs):
            j0 = 2 * jp
            rows = slice(j0 * LANES, hw) if reverse else slice(0, (j0 + 2) * LANES)
            cols = slice(j0 * LANES, (j0 + 2) * LANES)
            o_ref[0, :, h * hw + cols.start:h * hw + cols.stop] = jnp.dot(
                u8[:, h * hw + rows.start:h * hw + rows.stop], wt_ref[h, rows, cols],
                preferred_element_type=F32)
            k = h * pairs + jp
            carry = _scan_rows(s_ref, pw_ref, carry, S5_CHUNK_ROWS, reverse, exclusive=True,
                               groups=range(k * per, (k + 1) * per))
    carry_ref[b, 0] = carry[0]
    carry_ref[b, 1] = carry[1]
    for h in range(nh):
        y = (jnp.dot(s_ref[:, h * hs:(h + 1) * hs].astype(BF16), wc_ref[h, :hs],
                     preferred_element_type=F32)
             + jnp.dot(s_ref[:, ns + h * hs:ns + (h + 1) * hs].astype(BF16), wc_ref[h, hs:],
                       preferred_element_type=F32))
        o_ref[0, :, h * hw:(h + 1) * hw] = o_ref[0, :, h * hw:(h + 1) * hw] + y


def _s5_direction(layer, direction, u, u8, a_re, a_im, ldt, b_re, b_im, ct_re, ct_im, c_stack):
    b, r, w = u.shape
    nt = r // TILE
    rows = r // S5_BLOCK
    wide = S5_BLOCK * w
    tile_rows = TILE // S5_BLOCK
    ns = SSM_STATE
    reverse = direction == 1

    def dir_spec(shape):
        nd = len(shape)
        return pl.BlockSpec((1, 1) + tuple(shape), lambda *_: (layer, direction) + (0,) * nd,
                            pipeline_mode=pl.Buffered(1))

    y_ctx, state = pl.pallas_call(
        functools.partial(_s5_ctx_kernel, reverse=reverse),
        grid=(b,),
        in_specs=[pl.BlockSpec((1, TILE, w), lambda i: (i, nt - 1, 0)),
                  dir_spec((1, ns)), dir_spec((1, ns)), dir_spec((1, ns)),
                  dir_spec((w, ns)), dir_spec((w, ns)), dir_spec((2 * ns, w))],
        out_specs=[pl.BlockSpec((1, tile_rows, wide), lambda i: (i, 0, 0)),
                   pl.BlockSpec((1, 2, SUBLANES, ns), lambda i: (i, 0, 0, 0))],
        out_shape=[jax.ShapeDtypeStruct((b, tile_rows, wide), F32),
                   jax.ShapeDtypeStruct((b, 2, SUBLANES, ns), F32)],
        scratch_shapes=[pltpu.VMEM((w, 2 * ns), BF16),
                        pltpu.VMEM((8, SUBLANES, ns), F32),
                        pltpu.VMEM((TILE, 2 * ns), F32),
                        pltpu.VMEM((w // LANES, TILE, LANES), F32)],
        compiler_params=_params(("parallel",)),
        name="s5_context",
    )(u, a_re, a_im, ldt, b_re, b_im, c_stack)

    nchunks = (r - TILE) // (S5_BLOCK * S5_CHUNK_ROWS)

    def chunk(c, i):
        return (i, nchunks - 1 - c if reverse else c, 0)

    y_lat = pl.pallas_call(
        functools.partial(_s5_block_kernel, reverse=reverse),
        grid=(nchunks, b),
        in_specs=[pl.BlockSpec((1, S5_CHUNK_ROWS, wide), chunk),
                  pl.BlockSpec((1, 2, SUBLANES, ns), lambda c, i: (i, 0, 0, 0)),
                  dir_spec((1, ns)), dir_spec((1, ns)), dir_spec((1, ns)),
                  dir_spec((w, ns)), dir_spec((w, ns)), dir_spec((w, ns)), dir_spec((w, ns))],
        out_specs=pl.BlockSpec((1, S5_CHUNK_ROWS, wide), chunk),
        out_shape=jax.ShapeDtypeStruct((b, rows - tile_rows, wide), F32),
        scratch_shapes=[pltpu.VMEM((w // LANES, wide // (w // LANES), 2 * ns // (w // LANES)), BF16),
                        pltpu.VMEM((w // LANES, 2 * ns // (w // LANES), wide // (w // LANES)), BF16),
                        pltpu.VMEM((w // LANES, wide // (w // LANES), wide // (w // LANES)), BF16),
                        pltpu.VMEM((8, SUBLANES, ns), F32),
                        pltpu.VMEM((b, 2, SUBLANES, ns), F32),
                        pltpu.VMEM((S5_CHUNK_ROWS, 2 * ns), F32)],
        compiler_params=_params(("arbitrary", "arbitrary"), vmem=S5_VMEM_LIMIT),
        name="s5_blocks_reverse" if reverse else "s5_blocks_forward",
    )(u8, state, a_re, a_im, ldt, b_re, b_im, ct_re, ct_im)
    return y_lat, y_ctx


def _fourier_outer_kernel(w_ref, x_ref, g_ref):
    l2n, rows, w = x_ref.shape[1:]
    x = x_ref[0].reshape(l2n, rows * w).astype(BF16)
    g = jnp.dot(w_ref[...], x, preferred_element_type=F32)
    g_ref[0] = g.reshape(g_ref.shape[1:]).astype(BF16)


def _fourier_outer(f, w_outer):
    b, r, w = f.shape
    nt = r // TILE
    l2n = nt - 1
    flat = TILE * w
    return pl.pallas_call(
        _fourier_outer_kernel,
        grid=(b, flat // FOURIER_CHUNK),
        in_specs=[_const_spec((2 * l2n, l2n)),
                  pl.BlockSpec((1, l2n, FOURIER_CHUNK // w, w), lambda i, j: (i, 0, j, 0))],
        out_specs=pl.BlockSpec((1, 2 * l2n, FOURIER_CHUNK // w, w), lambda i, j: (i, 0, j, 0)),
        out_shape=jax.ShapeDtypeStruct((b, 2 * l2n, TILE, w), BF16),
        compiler_params=_params(("parallel", "parallel")),
        name="fourier_outer",
    )(w_outer, f.reshape(b, nt, TILE, w))


def _fourier_tail(y_re, y_im, c64_ref, wf_ref):
    w = FNET_WIDTH
    z = (jnp.dot(y_re.astype(BF16), c64_ref[:w], preferred_element_type=F32)
         + jnp.dot(y_im.astype(BF16), c64_ref[w:], preferred_element_type=F32))
    return jnp.dot(z.astype(BF16), wf_ref[0], preferred_element_type=F32)


def _fourier_inner_kernel(gr_ref, gi_ref, m_ref, c64_ref, wf_ref, o_ref):
    w = FNET_WIDTH
    ys = [jnp.dot(m_ref[i, :, :TILE], gr_ref[0, i], preferred_element_type=F32)
          + jnp.dot(m_ref[i, :, TILE:], gi_ref[0, i], preferred_element_type=F32)
          for i in range(FOURIER_GROUP)]
    for i, y in enumerate(ys):
        o_ref[0, :, i * w:(i + 1) * w] = _fourier_tail(y[:TILE], y[TILE:], c64_ref, wf_ref)


def _fourier_inner(layer, g, m_tab, c64_tab, wf_blk):
    b, two_l2n, _, w = g.shape
    l2n = two_l2n // 2
    steps = l2n // FOURIER_GROUP
    return pl.pallas_call(
        _fourier_inner_kernel,
        grid=(b, steps),
        in_specs=[
            pl.BlockSpec((1, FOURIER_GROUP, TILE, w), lambda i, j: (i, j, 0, 0)),
            pl.BlockSpec((1, FOURIER_GROUP, TILE, w), lambda i, j: (i, steps + j, 0, 0)),
            pl.BlockSpec((FOURIER_GROUP, 2 * TILE, 2 * TILE), lambda i, j: (j, 0, 0)),
            _const_spec((2 * w, w)),
            _const_spec((w, w), layer),
        ],
        out_specs=pl.BlockSpec((1, TILE, FOURIER_GROUP * w), lambda i, j: (i, 0, j)),
        out_shape=jax.ShapeDtypeStruct((b, TILE, l2n * w), F32),
        compiler_params=_params(("parallel", "parallel")),
        name="fourier_inner",
    )(g, g, m_tab, c64_tab, wf_blk)


def _fourier_ctx_kernel(x_ref, m_ref, c64_ref, wf_ref, o_ref, z_ref, *, l2n):
    y = jnp.dot(m_ref[...], x_ref[0].astype(BF16), preferred_element_type=F32)
    z = _fourier_tail(y[:TILE], y[TILE:], c64_ref, wf_ref)
    _rows_to_blocked(z, z_ref, o_ref.at[0], l2n)


def _fourier_context(layer, f, m_ctx, c64_tab, wf_blk):
    b, r, w = f.shape
    nt = r // TILE
    l2n = nt - 1
    return pl.pallas_call(
        functools.partial(_fourier_ctx_kernel, l2n=l2n),
        grid=(b,),
        in_specs=[
            pl.BlockSpec((1, TILE, w), lambda i: (i, nt - 1, 0)),
            _const_spec((2 * TILE, TILE)),
            _const_spec((2 * w, w)),
            _const_spec((w, w), layer),
        ],
        out_specs=pl.BlockSpec((1, TILE // l2n, l2n * w), lambda i: (i, 0, 0)),
        out_shape=jax.ShapeDtypeStruct((b, TILE // l2n, l2n * w), F32),
        scratch_shapes=[pltpu.VMEM((w // LANES, TILE, LANES), F32)],
        compiler_params=_params(("parallel",)),
        name="fourier_context",
    )(f, m_ctx, c64_tab, wf_blk)


def _outffn_kernel(x_ref, a_ref, yf_ref, yb_ref, yfc_ref, ybc_ref, u_ref, fn_ref, fnc_ref, mod_ref,
                   dsk_ref, wglu_ref, bglu_ref, wo_ref, g2_ref, w1_ref, w2_ref, o_ref, ytok_ref,
                   ftok_ref, *, l2n, nt):
    d = D_MODEL
    bsz = x_ref.shape[0]
    o1 = ATTN_WIDTH
    o2 = ATTN_WIDTH + SSM_WIDTH
    is_ctx = pl.program_id(0) == nt - 1

    def mixing_head(i):
        mod = mod_ref[0, pl.ds(jnp.where(is_ctx, bsz, i), 1), :]
        y_blk = jnp.where(is_ctx, yfc_ref[i] + ybc_ref[i], yf_ref[i] + yb_ref[i])
        y_scan = _blocked_to_rows(y_blk, ytok_ref.at[i], S5_BLOCK, tile_major=True)
        fnet = _blocked_to_rows(jnp.where(is_ctx, fnc_ref[i], fn_ref[i]), ftok_ref.at[i], l2n)
        hg = jax.nn.gelu(y_scan + dsk_ref[0] * u_ref[i])
        zg = jnp.dot(hg.astype(BF16), wglu_ref[0], preferred_element_type=F32) + bglu_ref[0]
        s5 = (hg * jax.nn.sigmoid(zg)).astype(BF16)
        mix = (jnp.dot(a_ref[i], wo_ref[0, :o1], preferred_element_type=F32)
               + jnp.dot(s5, wo_ref[0, o1:o2], preferred_element_type=F32)
               + jnp.dot(fnet.astype(BF16), wo_ref[0, o2:], preferred_element_type=F32))
        x1 = x_ref[i] + mod[:, 2 * d:3 * d] * mix
        ms = jnp.mean(x1 * x1, axis=-1, keepdims=True)
        hn = x1 * lax.rsqrt(ms + EPS) * g2_ref[0]
        h = (hn * (1.0 + mod[:, 4 * d:5 * d]) + mod[:, 3 * d:4 * d]).astype(BF16)
        return x1, h, mod[:, 5 * d:]

    def mlp(i, x1, h, gate):
        acc = jnp.zeros((TILE, d), F32)
        for cc in range(D_FF // FF_CHUNK):
            sl = slice(cc * FF_CHUNK, (cc + 1) * FF_CHUNK)
            t = jnp.dot(h, w1_ref[0, :, sl], preferred_element_type=F32)
            t = jnp.square(jnp.maximum(t, 0.0))
            acc = acc + jnp.dot(t.astype(BF16), w2_ref[0, sl, :], preferred_element_type=F32)
        o_ref[i] = x1 + gate * acc

    heads = [mixing_head(i) for i in range(bsz)]
    for i, (x1, h, gate) in enumerate(heads):
        mlp(i, x1, h, gate)


def _out_ffn(layer, xs, attn, y_fwd, y_bwd, u, fn, fn_ctx, mods, d_skip, w_glu, b_glu, w_out,
             g_norm2, w_ff1, w_ff2, latent_only):
    b, r, d = xs.shape
    nt = r // TILE
    l2n = nt - 1
    depth = w_out.shape[0]
    s5_shape = (b, TILE // S5_BLOCK, S5_BLOCK * SSM_WIDTH)
    fn_shape = (b, TILE // l2n, l2n * FNET_WIDTH)

    def latent(shape):
        return pl.BlockSpec(shape, lambda t: (0, jnp.minimum(t, nt - 2), 0))

    def context(shape):
        return pl.BlockSpec(shape, lambda t: (0, 0, 0))

    def tile(width):
        return pl.BlockSpec((b, TILE, width), lambda t: (0, t, 0))

    return pl.pallas_call(
        functools.partial(_outffn_kernel, l2n=l2n, nt=nt),
        grid=(nt - 1 if latent_only else nt,),
        in_specs=[
            tile(d), tile(ATTN_WIDTH), latent(s5_shape), latent(s5_shape), context(s5_shape),
            context(s5_shape), tile(SSM_WIDTH), latent(fn_shape), context(fn_shape),
            _const_spec((SUBLANES, 6 * d), layer),
            _const_spec((1, SSM_WIDTH), layer),
            _const_spec((SSM_WIDTH, SSM_WIDTH), layer),
            _const_spec((1, SSM_WIDTH), layer),
            _const_spec((d, d), layer),
            _const_spec((1, d), layer),
            _const_spec((d, D_FF), layer),
            _const_spec((D_FF, d), layer),
        ],
        out_specs=tile(d),
        out_shape=jax.ShapeDtypeStruct((b, r - TILE if latent_only else r, d), F32),
        scratch_shapes=[pltpu.VMEM((b, SSM_WIDTH // LANES, TILE, LANES), F32),
                        pltpu.VMEM((b, FNET_WIDTH // LANES, TILE, LANES), F32)],
        input_output_aliases={} if latent_only else {0: 0},
        compiler_params=_params(("parallel",)),
        name="out_ffn",
    )(xs, attn, y_fwd[0], y_bwd[0], y_fwd[1], y_bwd[1], u, fn, fn_ctx, mods, d_skip, w_glu, b_glu,
      w_out, g_norm2.reshape(depth, 1, d), w_ff1, w_ff2)


def _head_lane_tables():
    j = np.arange(LANES)
    half = j // 64
    mp = (j // 32) % 2
    idx = j % 32
    src_in_head = mp * HEAD_DIM + half * 32 + idx
    gain_idx = half * 32 + idx
    return src_in_head, gain_idx, mp


def _in_proj_column_order():
    src_in_head, _, _ = _head_lane_tables()
    qk = np.concatenate([h * LANES + src_in_head for h in range(N_HEADS)])
    return np.concatenate([qk, QK_WIDTH + qk, np.arange(2 * QK_WIDTH, IN_WIDTH)])


def _same_map_matrix():
    i = np.arange(256)
    head = i // LANES
    mp = ((i % LANES) // 32) % 2
    same = (head[:, None] == head[None, :]) & (mp[:, None] == mp[None, :])
    return same.astype(np.float32)


def _rope_tables(seq, ctx_len):
    t = jnp.arange(seq)
    row = (t // GRID_W).astype(F32)
    col = (t % GRID_W).astype(F32)
    n_freq = HEAD_DIM // 4
    inv = jnp.power(ROPE_BASE, -jnp.arange(n_freq, dtype=F32) / n_freq)
    ang = jnp.concatenate([row[:, None] * inv, col[:, None] * inv], axis=-1)
    cos = jnp.tile(jnp.cos(ang), (1, 4))
    sign = np.where(np.arange(LANES) < LANES // 2, -1.0, 1.0).astype(np.float32)
    sin = jnp.tile(jnp.sin(ang), (1, 4)) * sign
    cos = jnp.concatenate([cos, jnp.ones((ctx_len, LANES), F32)], axis=0)
    sin = jnp.concatenate([sin, jnp.zeros((ctx_len, LANES), F32)], axis=0)
    return cos, sin


def _dft_tables(length):
    l2n = length // TILE
    k2 = np.arange(l2n)
    phi = 2.0 * np.pi * ((k2[:, None] * k2[None, :]) % l2n) / l2n
    w_outer = np.concatenate([np.cos(phi), -np.sin(phi)], axis=0).astype(np.float32)
    l1 = np.arange(TILE)
    alpha = 2.0 * np.pi * ((k2[:, None] * l1[None, :]) % length) / length
    beta = 2.0 * np.pi * ((l1[:, None] * l1[None, :]) % TILE) / TILE
    ca = jnp.asarray(np.cos(alpha).astype(np.float32))[:, None, :]
    sa = jnp.asarray(np.sin(alpha).astype(np.float32))[:, None, :]
    cb = jnp.asarray(np.cos(beta).astype(np.float32))[None]
    sb = jnp.asarray(np.sin(beta).astype(np.float32))[None]
    norm = 1.0 / math.sqrt(length)
    ct = (ca * cb - sa * sb) * norm
    st = (sa * cb + ca * sb) * norm
    m = jnp.concatenate([jnp.concatenate([ct, st], axis=2),
                         jnp.concatenate([-st, ct], axis=2)], axis=1).astype(BF16)
    return jnp.asarray(w_outer).astype(BF16), m


def _tile_dft_table():
    l1 = np.arange(TILE)
    beta = 2.0 * np.pi * ((l1[:, None] * l1[None, :]) % TILE) / TILE
    tab = np.concatenate([np.cos(beta), -np.sin(beta)], axis=0) / math.sqrt(TILE)
    return jnp.asarray(tab.astype(np.float32)).astype(BF16)


def _channel_dft_table():
    c = np.arange(FNET_C)
    th = 2.0 * np.pi * ((c[:, None] * c[None, :]) % FNET_C) / FNET_C
    eye = np.eye(FNET_G)
    norm = 1.0 / math.sqrt(FNET_C)
    cblk = np.kron(eye, np.cos(th)) * norm
    sblk = np.kron(eye, np.sin(th)) * norm
    return jnp.asarray(np.concatenate([cblk, sblk], axis=0).astype(np.float32)).astype(BF16)


def _block_diag(blocks):
    lead = blocks.shape[:-3]
    g, a, b = blocks.shape[-3:]
    tiled = jnp.tile(blocks.reshape(lead + (g * a, b)), (1,) * len(lead) + (1, g))
    same = (np.arange(g * a)[:, None] // a) == (np.arange(g * b)[None, :] // b)
    return jnp.where(jnp.asarray(same), tiled, jnp.zeros((), blocks.dtype))


def kernel(x, c, ctx, c_ctx, w_mod, b_mod, g_norm1, w_in, g_qnorm, g_knorm, lam_q1, lam_k1, lam_q2, lam_k2, g_subln, ssm_a_re, ssm_a_im, ssm_log_dt, ssm_b_re, ssm_b_im, ssm_c_re, ssm_c_im, ssm_d, w_glu, b_glu, w_fnet, w_out, g_norm2, w_ff1, w_ff2):
    bsz, seq, d = x.shape
    ctx_len = ctx.shape[1]
    depth = w_mod.shape[0]
    assert d == D_MODEL and ctx_len == TILE and seq % TILE == 0 and seq % GRID_W == 0
    assert bsz + 1 <= SUBLANES

    assert (seq // TILE) % FOURIER_GROUP == 0 and seq % (S5_BLOCK * S5_CHUNK_ROWS) == 0
    xs = jnp.concatenate([x, ctx], axis=1)
    act = jnp.concatenate([c, c_ctx[None], jnp.zeros((SUBLANES - bsz - 1, d), F32)], axis=0)
    mods = _modulation(act, w_mod, b_mod)

    w_in_p = jnp.take(w_in, jnp.asarray(_in_proj_column_order()), axis=2).astype(BF16)
    _, gain_idx, _ = _head_lane_tables()
    gq = jnp.tile(g_qnorm[:, gain_idx], (1, N_HEADS)).reshape(depth, 1, QK_WIDTH)
    gk = jnp.tile(g_knorm[:, gain_idx], (1, N_HEADS)).reshape(depth, 1, QK_WIDTH)
    e_mat = jnp.asarray(_same_map_matrix()).astype(BF16)
    cos_t, sin_t = _rope_tables(seq, ctx_len)
    lam_init = np.array([0.8 - 0.6 * math.exp(-0.3 * i) for i in range(depth)], np.float32)
    pad = jnp.zeros((depth, LANES - HEAD_DIM), F32)
    lam_rows = [jnp.concatenate([v, pad], axis=1) for v in (lam_q1, lam_k1, lam_q2, lam_k2)]
    const_row = np.zeros((depth, LANES), np.float32)
    const_row[:, 0] = lam_init
    const_row[:, 1] = 1.0 - lam_init
    lam_p = jnp.stack(lam_rows + [jnp.asarray(const_row)]
                      + [jnp.zeros((depth, LANES), F32)] * (SUBLANES - 5), axis=1)
    g_sub = g_subln.reshape(depth, 1, V_DIM)

    a_re = ssm_a_re.reshape(depth, 2, 1, SSM_STATE)
    a_im = ssm_a_im.reshape(depth, 2, 1, SSM_STATE)
    ldt = jnp.repeat(ssm_log_dt, SSM_N, axis=-1).reshape(depth, 2, 1, SSM_STATE)
    b_re = _block_diag(jnp.swapaxes(ssm_b_re, -1, -2))
    b_im = _block_diag(jnp.swapaxes(ssm_b_im, -1, -2))
    c_stack = jnp.concatenate([_block_diag(jnp.swapaxes(ssm_c_re, -1, -2)),
                               -_block_diag(jnp.swapaxes(ssm_c_im, -1, -2))],
                              axis=2).astype(BF16)
    ct_re = _block_diag(ssm_c_re)
    ct_im = _block_diag(ssm_c_im)
    d_skip = ssm_d.reshape(depth, 1, SSM_WIDTH)
    w_glu_b = w_glu.astype(BF16)
    b_glu_r = b_glu.reshape(depth, 1, SSM_WIDTH)

    w_outer, m_lat = _dft_tables(seq)
    m_ctx = _tile_dft_table()
    c64_tab = _channel_dft_table()
    wf_blk = _block_diag(w_fnet).astype(BF16)

    w_out_b = w_out.astype(BF16)
    w_ff1_b = w_ff1.astype(BF16)
    w_ff2_b = w_ff2.astype(BF16)

    for layer in range(depth):
        q, kt, ve, u, u8, f = _in_projection(layer, xs, mods, g_norm1, w_in_p, gq, gk, e_mat,
                                         cos_t, sin_t)
        attn = _attention(layer, q, kt, ve, lam_p, g_sub)
        y_fwd, y_bwd = [_s5_direction(layer, dr, u, u8, a_re, a_im, ldt, b_re, b_im, ct_re,
                                      ct_im, c_stack) for dr in range(2)]
        fn = _fourier_inner(layer, _fourier_outer(f, w_outer), m_lat, c64_tab, wf_blk)
        fn_ctx = _fourier_context(layer, f, m_ctx, c64_tab, wf_blk)
        xs = _out_ffn(layer, xs, attn, y_fwd, y_bwd, u, fn, fn_ctx, mods, d_skip, w_glu_b, b_glu_r,
                      w_out_b, g_norm2, w_ff1_b, w_ff2_b, latent_only=layer == depth - 1)
    return xs
```

```python
import functools
import math

import numpy as np
import jax
import jax.numpy as jnp
from jax import lax
from jax.experimental import pallas as pl
from jax.experimental.pallas import tpu as pltpu

F32 = jnp.float32
BF16 = jnp.bfloat16

D_MODEL = 1024
GRID_W = 64
N_HEADS = 4
HEAD_DIM = 64
V_DIM = 2 * HEAD_DIM
QK_WIDTH = N_HEADS * 2 * HEAD_DIM
ATTN_WIDTH = N_HEADS * V_DIM
SSM_WIDTH = D_MODEL // 4
SSM_P = 16
SSM_G = SSM_WIDTH // SSM_P
SSM_N = 64
SSM_STATE = SSM_G * SSM_N
FNET_WIDTH = D_MODEL // 4
FNET_G = 4
FNET_C = FNET_WIDTH // FNET_G
IN_WIDTH = 2 * QK_WIDTH + ATTN_WIDTH + SSM_WIDTH + FNET_WIDTH
D_FF = 4 * D_MODEL
ROPE_BASE = 10000.0
EPS = 1e-6
SCALE = HEAD_DIM ** -0.5
LOG2E = math.log2(math.e)

TILE = 256
LANES = 128
SUBLANES = 8
FF_CHUNK = 1024
MOD_BLOCK = 1536
FOURIER_CHUNK = 8192
FOURIER_GROUP = 4
S5_BLOCK = 8
S5_CHUNK_ROWS = 256
VMEM_LIMIT = 48 * 1024 * 1024
S5_VMEM_LIMIT = 56 * 1024 * 1024
ATTN_VMEM_LIMIT = 56 * 1024 * 1024

def _const_spec(shape, layer=None):
    nd = len(shape)
    if layer is None:
        return pl.BlockSpec(shape, lambda *_: (0,) * nd, pipeline_mode=pl.Buffered(1))
    return pl.BlockSpec((1,) + tuple(shape), lambda *_: (layer,) + (0,) * nd,
                        pipeline_mode=pl.Buffered(1))


def _params(sem, vmem=None):
    return pltpu.CompilerParams(dimension_semantics=sem, vmem_limit_bytes=vmem or VMEM_LIMIT)


def _blocked_lane(j, h, period, nh, tile_major):
    return ((h * period + j) if tile_major else (j * nh + h)) * LANES


def _rows_to_blocked(tok, scr_ref, blk_ref, period, tile_major=False):
    n, w = tok.shape
    nh = w // LANES
    for h in range(nh):
        scr_ref[h] = tok[:, h * LANES:(h + 1) * LANES]
    for j in range(period):
        for h in range(nh):
            lo = _blocked_lane(j, h, period, nh, tile_major)
            blk_ref[:, lo:lo + LANES] = scr_ref[h, pl.ds(j, n // period, stride=period), :]


def _blocked_to_rows(blk, scr_ref, period, tile_major=False):
    nh, n, _ = scr_ref.shape
    for j in range(period):
        for h in range(nh):
            lo = _blocked_lane(j, h, period, nh, tile_major)
            scr_ref[h, pl.ds(j, n // period, stride=period), :] = blk[:, lo:lo + LANES]
    return jnp.concatenate([scr_ref[h] for h in range(nh)], axis=1)


def _mod_kernel(act_ref, w_ref, b_ref, o_ref):
    a = act_ref[...]
    a = a * jax.nn.sigmoid(a)
    o_ref[0] = jnp.dot(a.astype(BF16), w_ref[0].astype(BF16),
                       preferred_element_type=F32) + b_ref[0]


def _modulation(act, w_mod, b_mod):
    depth, d, n = w_mod.shape
    bn = MOD_BLOCK
    return pl.pallas_call(
        _mod_kernel,
        grid=(depth, n // bn),
        in_specs=[pl.BlockSpec((SUBLANES, d), lambda l, j: (0, 0)),
                  pl.BlockSpec((1, d, bn), lambda l, j: (l, 0, j)),
                  pl.BlockSpec((1, 1, bn), lambda l, j: (l, 0, j))],
        out_specs=pl.BlockSpec((1, SUBLANES, bn), lambda l, j: (l, 0, j)),
        out_shape=jax.ShapeDtypeStruct((depth, SUBLANES, n), F32),
        compiler_params=_params(("parallel", "parallel")),
        name="modulation",
    )(act, w_mod, b_mod.reshape(depth, 1, n))


def _inproj_kernel(x_ref, mod_ref, g1_ref, w_ref, gq_ref, gk_ref, e_ref, cos_ref, sin_ref,
                   q_ref, kt_ref, ve_ref, u_ref, u8_ref, f_ref, stage_ref, *, nt):
    d = D_MODEL
    bsz = x_ref.shape[0]
    is_ctx = pl.program_id(0) == nt - 1
    e = e_ref[...]
    cos = cos_ref[...]
    sin = sin_ref[...]

    def project(i):
        x = x_ref[i]
        ms = jnp.mean(x * x, axis=-1, keepdims=True)
        xn = x * lax.rsqrt(ms + EPS) * g1_ref[0]
        mod = mod_ref[0, pl.ds(jnp.where(is_ctx, bsz, i), 1), :]
        h = xn * (1.0 + mod[:, d:2 * d]) + mod[:, :d]
        return jnp.dot(h.astype(BF16), w_ref[0], preferred_element_type=F32)

    def qk_norm(z, g):
        sq = z * z
        hi = sq.astype(BF16)
        lo = (sq - hi.astype(F32)).astype(BF16)
        parts = []
        for j in range(QK_WIDTH // 256):
            sl = slice(j * 256, (j + 1) * 256)
            parts.append(jnp.dot(hi[:, sl], e, preferred_element_type=F32)
                         + jnp.dot(lo[:, sl], e, preferred_element_type=F32))
        ssum = jnp.concatenate(parts, axis=-1)
        return z * lax.rsqrt(ssum * (1.0 / HEAD_DIM) + EPS) * g

    def rope(zh):
        return zh * cos + pltpu.roll(zh, LANES // 2, 1) * sin

    row_map = (lax.broadcasted_iota(jnp.int32, (LANES, TILE), 0) // 32) % 2
    ones = jnp.ones((TILE, LANES), BF16)

    def finish(i, proj):
        qn = qk_norm(proj[:, :QK_WIDTH], gq_ref[0])
        kn = qk_norm(proj[:, QK_WIDTH:2 * QK_WIDTH], gk_ref[0])
        for hh in range(N_HEADS):
            sl = slice(hh * LANES, (hh + 1) * LANES)
            q_ref[i, :, sl] = (rope(qn[:, sl]) * (SCALE * LOG2E)).astype(BF16)
            kt = rope(kn[:, sl]).T
            kt_ref[i, hh, 0] = jnp.where(row_map == 0, kt, 0.0).astype(BF16)
            kt_ref[i, hh, 1] = jnp.where(row_map == 1, kt, 0.0).astype(BF16)
            vo = 2 * QK_WIDTH + hh * V_DIM
            ve_ref[i, hh, :, :V_DIM] = proj[:, vo:vo + V_DIM].astype(BF16)
            ve_ref[i, hh, :, V_DIM:] = ones
        uo = 2 * QK_WIDTH + ATTN_WIDTH
        u_ref[i] = proj[:, uo:uo + SSM_WIDTH]
        _rows_to_blocked(proj[:, uo:uo + SSM_WIDTH], stage_ref, u8_ref.at[i], S5_BLOCK,
                         tile_major=True)
        f_ref[i] = proj[:, uo + SSM_WIDTH:]

    projs = [project(i) for i in range(bsz)]
    for i, proj in enumerate(projs):
        finish(i, proj)


def _in_projection(layer, xs, mods, g_norm1, w_in, gq, gk, e_mat, cos_t, sin_t):
    b, r, d = xs.shape
    nt = r // TILE
    depth = w_in.shape[0]
    return pl.pallas_call(
        functools.partial(_inproj_kernel, nt=nt),
        grid=(nt,),
        in_specs=[
            pl.BlockSpec((b, TILE, d), lambda t: (0, t, 0)),
            _const_spec((SUBLANES, 6 * d), layer),
            _const_spec((1, d), layer),
            _const_spec((d, IN_WIDTH), layer),
            _const_spec((1, QK_WIDTH), layer),
            _const_spec((1, QK_WIDTH), layer),
            _const_spec((256, 256)),
            pl.BlockSpec((TILE, LANES), lambda t: (t, 0)),
            pl.BlockSpec((TILE, LANES), lambda t: (t, 0)),
        ],
        out_specs=[
            pl.BlockSpec((b, TILE, QK_WIDTH), lambda t: (0, t, 0)),
            pl.BlockSpec((b, N_HEADS, 2, LANES, TILE), lambda t: (0, 0, 0, 0, t)),
            pl.BlockSpec((b, N_HEADS, TILE, 2 * V_DIM), lambda t: (0, 0, t, 0)),
            pl.BlockSpec((b, TILE, SSM_WIDTH), lambda t: (0, t, 0)),
            pl.BlockSpec((b, TILE // S5_BLOCK, S5_BLOCK * SSM_WIDTH), lambda t: (0, t, 0)),
            pl.BlockSpec((b, TILE, FNET_WIDTH), lambda t: (0, t, 0)),
        ],
        out_shape=[
            jax.ShapeDtypeStruct((b, r, QK_WIDTH), BF16),
            jax.ShapeDtypeStruct((b, N_HEADS, 2, LANES, r), BF16),
            jax.ShapeDtypeStruct((b, N_HEADS, r, 2 * V_DIM), BF16),
            jax.ShapeDtypeStruct((b, r, SSM_WIDTH), F32),
            jax.ShapeDtypeStruct((b, r // S5_BLOCK, S5_BLOCK * SSM_WIDTH), F32),
            jax.ShapeDtypeStruct((b, r, FNET_WIDTH), F32),
        ],
        scratch_shapes=[pltpu.VMEM((SSM_WIDTH // LANES, TILE, LANES), F32)],
        compiler_params=_params(("parallel",)),
        name="in_projection",
    )(xs, mods, g_norm1.reshape(depth, 1, d), w_in, gq, gk, e_mat, cos_t, sin_t)


def _attn_kernel(q_ref, kt_ref, ve_ref, lam_ref, gs_ref, o_ref, *, nt, ctx_len):
    t = pl.program_id(2)
    q = q_ref[0]
    r = ve_ref.shape[2]

    def attend(k0):
        outs = []
        ve = ve_ref[0, 0, k0:, :]
        scores = [jnp.dot(q, kt_ref[0, 0, mp, :, k0:], preferred_element_type=F32)
                  for mp in range(2)]
        for s in scores:
            p = jnp.exp2(s - jnp.max(s, axis=1, keepdims=True))
            acc = jnp.dot(p.astype(BF16), ve, preferred_element_type=F32)
            outs.append(acc[:, :V_DIM] / acc[:, V_DIM:])
        lp = lam_ref[0]
        s1 = jnp.sum(lp[0:1] * lp[1:2], axis=-1, keepdims=True)
        s2 = jnp.sum(lp[2:3] * lp[3:4], axis=-1, keepdims=True)
        lam = jnp.exp(s1) - jnp.exp(s2) + lp[4:5, 0:1]
        a = outs[0] - lam * outs[1]
        a = a * lax.rsqrt(jnp.mean(a * a, axis=-1, keepdims=True) + EPS)
        o_ref[0] = (a * gs_ref[0] * lp[4:5, 1:2]).astype(BF16)

    @pl.when(t < nt - 1)
    def _latent_queries():
        attend(0)

    @pl.when(t == nt - 1)
    def _context_queries():
        attend(r - ctx_len)


def _attention(layer, q, kt, ve, lam_p, g_subln):
    b, r, _ = q.shape
    nt = r // TILE
    return pl.pallas_call(
        functools.partial(_attn_kernel, nt=nt, ctx_len=TILE),
        grid=(b, N_HEADS, nt),
        in_specs=[
            pl.BlockSpec((1, TILE, LANES), lambda i, h, t: (i, t, h)),
            pl.BlockSpec((1, 1, 2, LANES, r), lambda i, h, t: (i, h, 0, 0, 0)),
            pl.BlockSpec((1, 1, r, 2 * V_DIM), lambda i, h, t: (i, h, 0, 0)),
            _const_spec((SUBLANES, LANES), layer),
            _const_spec((1, V_DIM), layer),
        ],
        out_specs=pl.BlockSpec((1, TILE, V_DIM), lambda i, h, t: (i, t, h)),
        out_shape=jax.ShapeDtypeStruct((b, r, ATTN_WIDTH), BF16),
        compiler_params=_params(("parallel", "parallel", "arbitrary"), vmem=ATTN_VMEM_LIMIT),
        name="diff_attention",
    )(q, kt, ve, lam_p, g_subln)


def _zoh(are_ref, aim_ref, ldt_ref):
    a_re = are_ref[0, 0]
    a_im = aim_ref[0, 0]
    dt = jnp.exp(ldt_ref[0, 0])
    mag = jnp.exp(dt * a_re)
    ang = dt * a_im
    ab_re = mag * jnp.cos(ang)
    ab_im = mag * jnp.sin(ang)
    den = a_re * a_re + a_im * a_im
    n_re = ab_re - 1.0
    f_re = (n_re * a_re + ab_im * a_im) / den
    f_im = (ab_im * a_re - n_re * a_im) / den
    return ab_re, ab_im, f_re, f_im


def _complex_powers(base_re, base_im, n):
    pows = [(jnp.ones_like(base_re), jnp.zeros_like(base_im))]
    for _ in range(n):
        pr, pi = pows[-1]
        pows.append((pr * base_re - pi * base_im, pr * base_im + pi * base_re))
    return pows


def _fill_scan_tables(pw_ref, base_re, base_im, reverse):
    shape = (SUBLANES, SSM_STATE)
    pows = _complex_powers(base_re, base_im, SUBLANES)
    row = lax.broadcasted_iota(jnp.int32, shape, 0)
    zero = jnp.zeros(shape, F32)
    for idx, k in enumerate((1, 2, 4)):
        mask = (row + k <= SUBLANES - 1) if reverse else (row >= k)
        pw_ref[2 * idx] = jnp.where(mask, jnp.broadcast_to(pows[k][0], shape), zero)
        pw_ref[2 * idx + 1] = jnp.where(mask, jnp.broadcast_to(pows[k][1], shape), zero)
    pcr = zero
    pci = zero
    for tt in range(SUBLANES):
        e = (SUBLANES - tt) if reverse else tt + 1
        pcr = jnp.where(row == tt, jnp.broadcast_to(pows[e][0], shape), pcr)
        pci = jnp.where(row == tt, jnp.broadcast_to(pows[e][1], shape), pci)
    pw_ref[6] = pcr
    pw_ref[7] = pci


def _scan_rows(s_ref, pw_ref, carry, nrows, reverse, exclusive, groups=None):
    ns = SSM_STATE
    shape = (SUBLANES, ns)
    ngroups = nrows // SUBLANES
    edge = (SUBLANES - 1) if reverse else 0

    def group(g, carry):
        cr, ci = carry
        gi = (ngroups - 1 - g) if reverse else g
        r0 = pl.multiple_of(gi * SUBLANES, SUBLANES)
        xr = s_ref[pl.ds(r0, SUBLANES), :ns]
        xi = s_ref[pl.ds(r0, SUBLANES), ns:]
        for idx, k in enumerate((1, 2, 4)):
            sh = (SUBLANES - k) if reverse else k
            sr = pltpu.roll(xr, sh, 0)
            si = pltpu.roll(xi, sh, 0)
            pr = pw_ref[2 * idx]
            pi = pw_ref[2 * idx + 1]
            xr, xi = xr + pr * sr - pi * si, xi + pr * si + pi * sr
        pr = pw_ref[6]
        pi = pw_ref[7]
        xr, xi = xr + pr * cr - pi * ci, xi + pr * ci + pi * cr
        if exclusive:
            row = lax.broadcasted_iota(jnp.int32, shape, 0)
            sh = (SUBLANES - 1) if reverse else 1
            er = jnp.where(row == edge, cr, pltpu.roll(xr, sh, 0))
            ei = jnp.where(row == edge, ci, pltpu.roll(xi, sh, 0))
        else:
            er, ei = xr, xi
        s_ref[pl.ds(r0, SUBLANES), :ns] = er
        s_ref[pl.ds(r0, SUBLANES), ns:] = ei
        last = 0 if reverse else SUBLANES - 1
        return (jnp.broadcast_to(xr[last:last + 1], shape),
                jnp.broadcast_to(xi[last:last + 1], shape))

    if groups is None:
        return lax.fori_loop(0, ngroups, group, carry)
    for g in groups:
        carry = group(g, carry)
    return carry


def _s5_ctx_kernel(u_ref, are_ref, aim_ref, ldt_ref, bre_ref, bim_ref, c_ref,
                   o_ref, st_ref, bbar_ref, pw_ref, s_ref, y_ref, *, reverse):
    ns = SSM_STATE
    ab_re, ab_im, f_re, f_im = _zoh(are_ref, aim_ref, ldt_ref)
    bre = bre_ref[0, 0]
    bim = bim_ref[0, 0]
    bbar_ref[:, :ns] = (f_re * bre - f_im * bim).astype(BF16)
    bbar_ref[:, ns:] = (f_re * bim + f_im * bre).astype(BF16)
    _fill_scan_tables(pw_ref, ab_re, ab_im, reverse)
    s_ref[...] = jnp.dot(u_ref[0].astype(BF16), bbar_ref[...], preferred_element_type=F32)
    zero = jnp.zeros((SUBLANES, ns), F32)
    cr, ci = _scan_rows(s_ref, pw_ref, (zero, zero), TILE, reverse, exclusive=False)
    st_ref[0, 0] = cr
    st_ref[0, 1] = ci
    y = jnp.dot(s_ref[...].astype(BF16), c_ref[0, 0], preferred_element_type=F32)
    _rows_to_blocked(y, y_ref, o_ref.at[0], S5_BLOCK, tile_major=True)


def _s5_block_kernel(u8_ref, s0_ref, are_ref, aim_ref, ldt_ref, bre_ref, bim_ref, ctre_ref,
                     ctim_ref, o_ref, wx_ref, wc_ref, wt_ref, pw_ref, carry_ref, s_ref,
                     *, reverse):
    ns = SSM_STATE
    w = SSM_WIDTH
    nb = S5_BLOCK
    nh = w // LANES
    hs = ns // nh
    hw = nb * LANES
    c = pl.program_id(0)
    b = pl.program_id(1)

    @pl.when((c == 0) & (b == 0))
    def _build_maps():
        ab_re, ab_im, f_re, f_im = _zoh(are_ref, aim_ref, ldt_ref)
        bre = bre_ref[0, 0]
        bim = bim_ref[0, 0]
        bb_re = f_re * bre - f_im * bim
        bb_im = f_re * bim + f_im * bre
        pows = _complex_powers(ab_re, ab_im, nb)
        ctre = ctre_ref[0, 0]
        ctim = ctim_ref[0, 0]
        ct_stack = jnp.concatenate([ctre, -ctim], axis=1).astype(BF16)
        chan = [slice(h * LANES, (h + 1) * LANES) for h in range(nh)]
        stat = [slice(h * hs, (h + 1) * hs) for h in range(nh)]
        tok = [slice(t * LANES, (t + 1) * LANES) for t in range(nb)]
        taps = []
        for e in range(nb):
            pr, pi = pows[e]
            xr = pr * bb_re - pi * bb_im
            xi = pr * bb_im + pi * bb_re
            i = e if reverse else nb - 1 - e
            for h in range(nh):
                wx_ref[h, tok[i], :hs] = xr[chan[h], stat[h]].astype(BF16)
                wx_ref[h, tok[i], hs:] = xi[chan[h], stat[h]].astype(BF16)
            xk = jnp.concatenate([xr, xi], axis=1).astype(BF16)
            taps.append(lax.dot_general(xk, ct_stack, (((1,), (1,)), ((), ())),
                                        preferred_element_type=F32).astype(BF16))
        zero_blk = jnp.zeros((LANES, LANES), BF16)
        for i in range(nb):
            for j in range(nb):
                lag = (i - j) if reverse else (j - i)
                for h in range(nh):
                    wt_ref[h, tok[i], tok[j]] = taps[lag][chan[h], chan[h]] if lag >= 0 else zero_blk
        for j in range(nb):
            pr, pi = pows[nb - j] if reverse else pows[j + 1]
            c_re = ctre * pr - ctim * pi
            c_im = -(ctre * pi + ctim * pr)
            for h in range(nh):
                wc_ref[h, :hs, tok[j]] = c_re[chan[h], stat[h]].T.astype(BF16)
                wc_ref[h, hs:, tok[j]] = c_im[chan[h], stat[h]].T.astype(BF16)
        _fill_scan_tables(pw_ref, pows[nb][0], pows[nb][1], reverse)

    @pl.when(c == 0)
    def _load_state():
        carry_ref[b] = s0_ref[0]

    u8 = u8_ref[0].astype(BF16)
    for h in range(nh):
        x = jnp.dot(u8[:, h * hw:(h + 1) * hw], wx_ref[h], preferred_element_type=F32)
        s_ref[:, h * hs:(h + 1) * hs] = x[:, :hs]
        s_ref[:, ns + h * hs:ns + (h + 1) * hs] = x[:, hs:]
    carry = (carry_ref[b, 0], carry_ref[b, 1])
    pairs = nb // 2
    per = (S5_CHUNK_ROWS // SUBLANES) // (nh * pairs)
    for h in range(nh):
        for jp in range(pairs):
            j0 = 2 * jp
            rows = slice(j0 * LANES, hw) if reverse else slice(0, (j0 + 2) * LANES)
            cols = slice(j0 * LANES, (j0 + 2) * LANES)
            o_ref[0, :, h * hw + cols.start:h * hw + cols.stop] = jnp.dot(
                u8[:, h * hw + rows.start:h * hw + rows.stop], wt_ref[h, rows, cols],
                preferred_element_type=F32)
            k = h * pairs + jp
            carry = _scan_rows(s_ref, pw_ref, carry, S5_CHUNK_ROWS, reverse, exclusive=True,
                               groups=range(k * per, (k + 1) * per))
    carry_ref[b, 0] = carry[0]
    carry_ref[b, 1] = carry[1]
    for h in range(nh):
        y = (jnp.dot(s_ref[:, h * hs:(h + 1) * hs].astype(BF16), wc_ref[h, :hs],
                     preferred_element_type=F32)
             + jnp.dot(s_ref[:, ns + h * hs:ns + (h + 1) * hs].astype(BF16), wc_ref[h, hs:],
                       preferred_element_type=F32))
        o_ref[0, :, h * hw:(h + 1) * hw] = o_ref[0, :, h * hw:(h + 1) * hw] + y


def _s5_direction(layer, direction, u, u8, a_re, a_im, ldt, b_re, b_im, ct_re, ct_im, c_stack):
    b, r, w = u.shape
    nt = r // TILE
    rows = r // S5_BLOCK
    wide = S5_BLOCK * w
    tile_rows = TILE // S5_BLOCK
    ns = SSM_STATE
    reverse = direction == 1

    def dir_spec(shape):
        nd = len(shape)
        return pl.BlockSpec((1, 1) + tuple(shape), lambda *_: (layer, direction) + (0,) * nd,
                            pipeline_mode=pl.Buffered(1))

    y_ctx, state = pl.pallas_call(
        functools.partial(_s5_ctx_kernel, reverse=reverse),
        grid=(b,),
        in_specs=[pl.BlockSpec((1, TILE, w), lambda i: (i, nt - 1, 0)),
                  dir_spec((1, ns)), dir_spec((1, ns)), dir_spec((1, ns)),
                  dir_spec((w, ns)), dir_spec((w, ns)), dir_spec((2 * ns, w))],
        out_specs=[pl.BlockSpec((1, tile_rows, wide), lambda i: (i, 0, 0)),
                   pl.BlockSpec((1, 2, SUBLANES, ns), lambda i: (i, 0, 0, 0))],
        out_shape=[jax.ShapeDtypeStruct((b, tile_rows, wide), F32),
                   jax.ShapeDtypeStruct((b, 2, SUBLANES, ns), F32)],
        scratch_shapes=[pltpu.VMEM((w, 2 * ns), BF16),
                        pltpu.VMEM((8, SUBLANES, ns), F32),
                        pltpu.VMEM((TILE, 2 * ns), F32),
                        pltpu.VMEM((w // LANES, TILE, LANES), F32)],
        compiler_params=_params(("parallel",)),
        name="s5_context",
    )(u, a_re, a_im, ldt, b_re, b_im, c_stack)

    nchunks = (r - TILE) // (S5_BLOCK * S5_CHUNK_ROWS)

    def chunk(c, i):
        return (i, nchunks - 1 - c if reverse else c, 0)

    y_lat = pl.pallas_call(
        functools.partial(_s5_block_kernel, reverse=reverse),
        grid=(nchunks, b),
        in_specs=[pl.BlockSpec((1, S5_CHUNK_ROWS, wide), chunk),
                  pl.BlockSpec((1, 2, SUBLANES, ns), lambda c, i: (i, 0, 0, 0)),
                  dir_spec((1, ns)), dir_spec((1, ns)), dir_spec((1, ns)),
                  dir_spec((w, ns)), dir_spec((w, ns)), dir_spec((w, ns)), dir_spec((w, ns))],
        out_specs=pl.BlockSpec((1, S5_CHUNK_ROWS, wide), chunk),
        out_shape=jax.ShapeDtypeStruct((b, rows - tile_rows, wide), F32),
        scratch_shapes=[pltpu.VMEM((w // LANES, wide // (w // LANES), 2 * ns // (w // LANES)), BF16),
                        pltpu.VMEM((w // LANES, 2 * ns // (w // LANES), wide // (w // LANES)), BF16),
                        pltpu.VMEM((w // LANES, wide // (w // LANES), wide // (w // LANES)), BF16),
                        pltpu.VMEM((8, SUBLANES, ns), F32),
                        pltpu.VMEM((b, 2, SUBLANES, ns), F32),
                        pltpu.VMEM((S5_CHUNK_ROWS, 2 * ns), F32)],
        compiler_params=_params(("arbitrary", "arbitrary"), vmem=S5_VMEM_LIMIT),
        name="s5_blocks_reverse" if reverse else "s5_blocks_forward",
    )(u8, state, a_re, a_im, ldt, b_re, b_im, ct_re, ct_im)
    return y_lat, y_ctx


def _fourier_outer_kernel(w_ref, x_ref, g_ref):
    l2n, rows, w = x_ref.shape[1:]
    x = x_ref[0].reshape(l2n, rows * w).astype(BF16)
    g = jnp.dot(w_ref[...], x, preferred_element_type=F32)
    g_ref[0] = g.reshape(g_ref.shape[1:]).astype(BF16)


def _fourier_outer(f, w_outer):
    b, r, w = f.shape
    nt = r // TILE
    l2n = nt - 1
    flat = TILE * w
    return pl.pallas_call(
        _fourier_outer_kernel,
        grid=(b, flat // FOURIER_CHUNK),
        in_specs=[_const_spec((2 * l2n, l2n)),
                  pl.BlockSpec((1, l2n, FOURIER_CHUNK // w, w), lambda i, j: (i, 0, j, 0))],
        out_specs=pl.BlockSpec((1, 2 * l2n, FOURIER_CHUNK // w, w), lambda i, j: (i, 0, j, 0)),
        out_shape=jax.ShapeDtypeStruct((b, 2 * l2n, TILE, w), BF16),
        compiler_params=_params(("parallel", "parallel")),
        name="fourier_outer",
    )(w_outer, f.reshape(b, nt, TILE, w))


def _fourier_tail(y_re, y_im, c64_ref, wf_ref):
    w = FNET_WIDTH
    z = (jnp.dot(y_re.astype(BF16), c64_ref[:w], preferred_element_type=F32)
         + jnp.dot(y_im.astype(BF16), c64_ref[w:], preferred_element_type=F32))
    return jnp.dot(z.astype(BF16), wf_ref[0], preferred_element_type=F32)


def _fourier_inner_kernel(gr_ref, gi_ref, m_ref, c64_ref, wf_ref, o_ref):
    w = FNET_WIDTH
    ys = [jnp.dot(m_ref[i, :, :TILE], gr_ref[0, i], preferred_element_type=F32)
          + jnp.dot(m_ref[i, :, TILE:], gi_ref[0, i], preferred_element_type=F32)
          for i in range(FOURIER_GROUP)]
    for i, y in enumerate(ys):
        o_ref[0, :, i * w:(i + 1) * w] = _fourier_tail(y[:TILE], y[TILE:], c64_ref, wf_ref)


def _fourier_inner(layer, g, m_tab, c64_tab, wf_blk):
    b, two_l2n, _, w = g.shape
    l2n = two_l2n // 2
    steps = l2n // FOURIER_GROUP
    return pl.pallas_call(
        _fourier_inner_kernel,
        grid=(b, steps),
        in_specs=[
            pl.BlockSpec((1, FOURIER_GROUP, TILE, w), lambda i, j: (i, j, 0, 0)),
            pl.BlockSpec((1, FOURIER_GROUP, TILE, w), lambda i, j: (i, steps + j, 0, 0)),
            pl.BlockSpec((FOURIER_GROUP, 2 * TILE, 2 * TILE), lambda i, j: (j, 0, 0)),
            _const_spec((2 * w, w)),
            _const_spec((w, w), layer),
        ],
        out_specs=pl.BlockSpec((1, TILE, FOURIER_GROUP * w), lambda i, j: (i, 0, j)),
        out_shape=jax.ShapeDtypeStruct((b, TILE, l2n * w), F32),
        compiler_params=_params(("parallel", "parallel")),
        name="fourier_inner",
    )(g, g, m_tab, c64_tab, wf_blk)


def _fourier_ctx_kernel(x_ref, m_ref, c64_ref, wf_ref, o_ref, z_ref, *, l2n):
    y = jnp.dot(m_ref[...], x_ref[0].astype(BF16), preferred_element_type=F32)
    z = _fourier_tail(y[:TILE], y[TILE:], c64_ref, wf_ref)
    _rows_to_blocked(z, z_ref, o_ref.at[0], l2n)


def _fourier_context(layer, f, m_ctx, c64_tab, wf_blk):
    b, r, w = f.shape
    nt = r // TILE
    l2n = nt - 1
    return pl.pallas_call(
        functools.partial(_fourier_ctx_kernel, l2n=l2n),
        grid=(b,),
        in_specs=[
            pl.BlockSpec((1, TILE, w), lambda i: (i, nt - 1, 0)),
            _const_spec((2 * TILE, TILE)),
            _const_spec((2 * w, w)),
            _const_spec((w, w), layer),
        ],
        out_specs=pl.BlockSpec((1, TILE // l2n, l2n * w), lambda i: (i, 0, 0)),
        out_shape=jax.ShapeDtypeStruct((b, TILE // l2n, l2n * w), F32),
        scratch_shapes=[pltpu.VMEM((w // LANES, TILE, LANES), F32)],
        compiler_params=_params(("parallel",)),
        name="fourier_context",
    )(f, m_ctx, c64_tab, wf_blk)


def _outffn_kernel(x_ref, a_ref, yf_ref, yb_ref, yfc_ref, ybc_ref, u_ref, fn_ref, fnc_ref, mod_ref,
                   dsk_ref, wglu_ref, bglu_ref, wo_ref, g2_ref, w1_ref, w2_ref, o_ref, ytok_ref,
                   ftok_ref, *, l2n, nt):
    d = D_MODEL
    bsz = x_ref.shape[0]
    o1 = ATTN_WIDTH
    o2 = ATTN_WIDTH + SSM_WIDTH
    is_ctx = pl.program_id(0) == nt - 1

    def mixing_head(i):
        mod = mod_ref[0, pl.ds(jnp.where(is_ctx, bsz, i), 1), :]
        y_blk = jnp.where(is_ctx, yfc_ref[i] + ybc_ref[i], yf_ref[i] + yb_ref[i])
        y_scan = _blocked_to_rows(y_blk, ytok_ref.at[i], S5_BLOCK, tile_major=True)
        fnet = _blocked_to_rows(jnp.where(is_ctx, fnc_ref[i], fn_ref[i]), ftok_ref.at[i], l2n)
        hg = jax.nn.gelu(y_scan + dsk_ref[0] * u_ref[i])
        zg = jnp.dot(hg.astype(BF16), wglu_ref[0], preferred_element_type=F32) + bglu_ref[0]
        s5 = (hg * jax.nn.sigmoid(zg)).astype(BF16)
        mix = (jnp.dot(a_ref[i], wo_ref[0, :o1], preferred_element_type=F32)
               + jnp.dot(s5, wo_ref[0, o1:o2], preferred_element_type=F32)
               + jnp.dot(fnet.astype(BF16), wo_ref[0, o2:], preferred_element_type=F32))
        x1 = x_ref[i] + mod[:, 2 * d:3 * d] * mix
        ms = jnp.mean(x1 * x1, axis=-1, keepdims=True)
        hn = x1 * lax.rsqrt(ms + EPS) * g2_ref[0]
        h = (hn * (1.0 + mod[:, 4 * d:5 * d]) + mod[:, 3 * d:4 * d]).astype(BF16)
        return x1, h, mod[:, 5 * d:]

    def mlp(i, x1, h, gate):
        acc = jnp.zeros((TILE, d), F32)
        for cc in range(D_FF // FF_CHUNK):
            sl = slice(cc * FF_CHUNK, (cc + 1) * FF_CHUNK)
            t = jnp.dot(h, w1_ref[0, :, sl], preferred_element_type=F32)
            t = jnp.square(jnp.maximum(t, 0.0))
            acc = acc + jnp.dot(t.astype(BF16), w2_ref[0, sl, :], preferred_element_type=F32)
        o_ref[i] = x1 + gate * acc

    heads = [mixing_head(i) for i in range(bsz)]
    for i, (x1, h, gate) in enumerate(heads):
        mlp(i, x1, h, gate)


def _out_ffn(layer, xs, attn, y_fwd, y_bwd, u, fn, fn_ctx, mods, d_skip, w_glu, b_glu, w_out,
             g_norm2, w_ff1, w_ff2, latent_only):
    b, r, d = xs.shape
    nt = r // TILE
    l2n = nt - 1
    depth = w_out.shape[0]
    s5_shape = (b, TILE // S5_BLOCK, S5_BLOCK * SSM_WIDTH)
    fn_shape = (b, TILE // l2n, l2n * FNET_WIDTH)

    def latent(shape):
        return pl.BlockSpec(shape, lambda t: (0, jnp.minimum(t, nt - 2), 0))

    def context(shape):
        return pl.BlockSpec(shape, lambda t: (0, 0, 0))

    def tile(width):
        return pl.BlockSpec((b, TILE, width), lambda t: (0, t, 0))

    return pl.pallas_call(
        functools.partial(_outffn_kernel, l2n=l2n, nt=nt),
        grid=(nt - 1 if latent_only else nt,),
        in_specs=[
            tile(d), tile(ATTN_WIDTH), latent(s5_shape), latent(s5_shape), context(s5_shape),
            context(s5_shape), tile(SSM_WIDTH), latent(fn_shape), context(fn_shape),
            _const_spec((SUBLANES, 6 * d), layer),
            _const_spec((1, SSM_WIDTH), layer),
            _const_spec((SSM_WIDTH, SSM_WIDTH), layer),
            _const_spec((1, SSM_WIDTH), layer),
            _const_spec((d, d), layer),
            _const_spec((1, d), layer),
            _const_spec((d, D_FF), layer),
            _const_spec((D_FF, d), layer),
        ],
        out_specs=tile(d),
        out_shape=jax.ShapeDtypeStruct((b, r - TILE if latent_only else r, d), F32),
        scratch_shapes=[pltpu.VMEM((b, SSM_WIDTH // LANES, TILE, LANES), F32),
                        pltpu.VMEM((b, FNET_WIDTH // LANES, TILE, LANES), F32)],
        input_output_aliases={} if latent_only else {0: 0},
        compiler_params=_params(("parallel",)),
        name="out_ffn",
    )(xs, attn, y_fwd[0], y_bwd[0], y_fwd[1], y_bwd[1], u, fn, fn_ctx, mods, d_skip, w_glu, b_glu,
      w_out, g_norm2.reshape(depth, 1, d), w_ff1, w_ff2)


def _head_lane_tables():
    j = np.arange(LANES)
    half = j // 64
    mp = (j // 32) % 2
    idx = j % 32
    src_in_head = mp * HEAD_DIM + half * 32 + idx
    gain_idx = half * 32 + idx
    return src_in_head, gain_idx, mp


def _in_proj_column_order():
    src_in_head, _, _ = _head_lane_tables()
    qk = np.concatenate([h * LANES + src_in_head for h in range(N_HEADS)])
    return np.concatenate([qk, QK_WIDTH + qk, np.arange(2 * QK_WIDTH, IN_WIDTH)])


def _same_map_matrix():
    i = np.arange(256)
    head = i // LANES
    mp = ((i % LANES) // 32) % 2
    same = (head[:, None] == head[None, :]) & (mp[:, None] == mp[None, :])
    return same.astype(np.float32)


def _rope_tables(seq, ctx_len):
    t = jnp.arange(seq)
    row = (t // GRID_W).astype(F32)
    col = (t % GRID_W).astype(F32)
    n_freq = HEAD_DIM // 4
    inv = jnp.power(ROPE_BASE, -jnp.arange(n_freq, dtype=F32) / n_freq)
    ang = jnp.concatenate([row[:, None] * inv, col[:, None] * inv], axis=-1)
    cos = jnp.tile(jnp.cos(ang), (1, 4))
    sign = np.where(np.arange(LANES) < LANES // 2, -1.0, 1.0).astype(np.float32)
    sin = jnp.tile(jnp.sin(ang), (1, 4)) * sign
    cos = jnp.concatenate([cos, jnp.ones((ctx_len, LANES), F32)], axis=0)
    sin = jnp.concatenate([sin, jnp.zeros((ctx_len, LANES), F32)], axis=0)
    return cos, sin


def _dft_tables(length):
    l2n = length // TILE
    k2 = np.arange(l2n)
    phi = 2.0 * np.pi * ((k2[:, None] * k2[None, :]) % l2n) / l2n
    w_outer = np.concatenate([np.cos(phi), -np.sin(phi)], axis=0).astype(np.float32)
    l1 = np.arange(TILE)
    alpha = 2.0 * np.pi * ((k2[:, None] * l1[None, :]) % length) / length
    beta = 2.0 * np.pi * ((l1[:, None] * l1[None, :]) % TILE) / TILE
    ca = jnp.asarray(np.cos(alpha).astype(np.float32))[:, None, :]
    sa = jnp.asarray(np.sin(alpha).astype(np.float32))[:, None, :]
    cb = jnp.asarray(np.cos(beta).astype(np.float32))[None]
    sb = jnp.asarray(np.sin(beta).astype(np.float32))[None]
    norm = 1.0 / math.sqrt(length)
    ct = (ca * cb - sa * sb) * norm
    st = (sa * cb + ca * sb) * norm
    m = jnp.concatenate([jnp.concatenate([ct, st], axis=2),
                         jnp.concatenate([-st, ct], axis=2)], axis=1).astype(BF16)
    return jnp.asarray(w_outer).astype(BF16), m


def _tile_dft_table():
    l1 = np.arange(TILE)
    beta = 2.0 * np.pi * ((l1[:, None] * l1[None, :]) % TILE) / TILE
    tab = np.concatenate([np.cos(beta), -np.sin(beta)], axis=0) / math.sqrt(TILE)
    return jnp.asarray(tab.astype(np.float32)).astype(BF16)


def _channel_dft_table():
    c = np.arange(FNET_C)
    th = 2.0 * np.pi * ((c[:, None] * c[None, :]) % FNET_C) / FNET_C
    eye = np.eye(FNET_G)
    norm = 1.0 / math.sqrt(FNET_C)
    cblk = np.kron(eye, np.cos(th)) * norm
    sblk = np.kron(eye, np.sin(th)) * norm
    return jnp.asarray(np.concatenate([cblk, sblk], axis=0).astype(np.float32)).astype(BF16)


def _block_diag(blocks):
    lead = blocks.shape[:-3]
    g, a, b = blocks.shape[-3:]
    tiled = jnp.tile(blocks.reshape(lead + (g * a, b)), (1,) * len(lead) + (1, g))
    same = (np.arange(g * a)[:, None] // a) == (np.arange(g * b)[None, :] // b)
    return jnp.where(jnp.asarray(same), tiled, jnp.zeros((), blocks.dtype))


def kernel(x, c, ctx, c_ctx, w_mod, b_mod, g_norm1, w_in, g_qnorm, g_knorm, lam_q1, lam_k1, lam_q2, lam_k2, g_subln, ssm_a_re, ssm_a_im, ssm_log_dt, ssm_b_re, ssm_b_im, ssm_c_re, ssm_c_im, ssm_d, w_glu, b_glu, w_fnet, w_out, g_norm2, w_ff1, w_ff2):
    bsz, seq, d = x.shape
    ctx_len = ctx.shape[1]
    depth = w_mod.shape[0]
    assert d == D_MODEL and ctx_len == TILE and seq % TILE == 0 and seq % GRID_W == 0
    assert bsz + 1 <= SUBLANES

    assert (seq // TILE) % FOURIER_GROUP == 0 and seq % (S5_BLOCK * S5_CHUNK_ROWS) == 0
    xs = jnp.concatenate([x, ctx], axis=1)
    act = jnp.concatenate([c, c_ctx[None], jnp.zeros((SUBLANES - bsz - 1, d), F32)], axis=0)
    mods = _modulation(act, w_mod, b_mod)

    w_in_p = jnp.take(w_in, jnp.asarray(_in_proj_column_order()), axis=2).astype(BF16)
    _, gain_idx, _ = _head_lane_tables()
    gq = jnp.tile(g_qnorm[:, gain_idx], (1, N_HEADS)).reshape(depth, 1, QK_WIDTH)
    gk = jnp.tile(g_knorm[:, gain_idx], (1, N_HEADS)).reshape(depth, 1, QK_WIDTH)
    e_mat = jnp.asarray(_same_map_matrix()).astype(BF16)
    cos_t, sin_t = _rope_tables(seq, ctx_len)
    lam_init = np.array([0.8 - 0.6 * math.exp(-0.3 * i) for i in range(depth)], np.float32)
    pad = jnp.zeros((depth, LANES - HEAD_DIM), F32)
    lam_rows = [jnp.concatenate([v, pad], axis=1) for v in (lam_q1, lam_k1, lam_q2, lam_k2)]
    const_row = np.zeros((depth, LANES), np.float32)
    const_row[:, 0] = lam_init
    const_row[:, 1] = 1.0 - lam_init
    lam_p = jnp.stack(lam_rows + [jnp.asarray(const_row)]
                      + [jnp.zeros((depth, LANES), F32)] * (SUBLANES - 5), axis=1)
    g_sub = g_subln.reshape(depth, 1, V_DIM)

    a_re = ssm_a_re.reshape(depth, 2, 1, SSM_STATE)
    a_im = ssm_a_im.reshape(depth, 2, 1, SSM_STATE)
    ldt = jnp.repeat(ssm_log_dt, SSM_N, axis=-1).reshape(depth, 2, 1, SSM_STATE)
    b_re = _block_diag(jnp.swapaxes(ssm_b_re, -1, -2))
    b_im = _block_diag(jnp.swapaxes(ssm_b_im, -1, -2))
    c_stack = jnp.concatenate([_block_diag(jnp.swapaxes(ssm_c_re, -1, -2)),
                               -_block_diag(jnp.swapaxes(ssm_c_im, -1, -2))],
                              axis=2).astype(BF16)
    ct_re = _block_diag(ssm_c_re)
    ct_im = _block_diag(ssm_c_im)
    d_skip = ssm_d.reshape(depth, 1, SSM_WIDTH)
    w_glu_b = w_glu.astype(BF16)
    b_glu_r = b_glu.reshape(depth, 1, SSM_WIDTH)

    w_outer, m_lat = _dft_tables(seq)
    m_ctx = _tile_dft_table()
    c64_tab = _channel_dft_table()
    wf_blk = _block_diag(w_fnet).astype(BF16)

    w_out_b = w_out.astype(BF16)
    w_ff1_b = w_ff1.astype(BF16)
    w_ff2_b = w_ff2.astype(BF16)

    for layer in range(depth):
        q, kt, ve, u, u8, f = _in_projection(layer, xs, mods, g_norm1, w_in_p, gq, gk, e_mat,
                                         cos_t, sin_t)
        attn = _attention(layer, q, kt, ve, lam_p, g_sub)
        y_fwd, y_bwd = [_s5_direction(layer, dr, u, u8, a_re, a_im, ldt, b_re, b_im, ct_re,
                                      ct_im, c_stack) for dr in range(2)]
        fn = _fourier_inner(layer, _fourier_outer(f, w_outer), m_lat, c64_tab, wf_blk)
        fn_ctx = _fourier_context(layer, f, m_ctx, c64_tab, wf_blk)
        xs = _out_ffn(layer, xs, attn, y_fwd, y_bwd, u, fn, fn_ctx, mods, d_skip, w_glu_b, b_glu_r,
                      w_out_b, g_norm2, w_ff1_b, w_ff2_b, latent_only=layer == depth - 1)
    return xs
```
